```python
import math
import jax, jax.numpy as jnp
from jax import lax
import numpy as np

D_MODEL = 1024
BATCH = 8
SEQ = 4096
DEPTH = 1
DEC_BATCH = 128
DEC_SEQ = 4
PAST_LEN = 8192
PAGE_SIZE = 128

NORM_EPS = 1e-6
SSD_D_INNER = D_MODEL
SSD_HEAD_DIM = 64
SSD_N_HEADS = SSD_D_INNER // SSD_HEAD_DIM
SSD_N_GROUPS = 4
SSD_D_STATE = 128
SSD_CONV = 4
SSD_CHUNK = 128
SSD_CONV_DIM = SSD_D_INNER + 2 * SSD_N_GROUPS * SSD_D_STATE
ATT_WINDOWS = (128, 512, 2048)
ATT_DILATIONS = (1, 4, 16)
ATT_N_GROUPS = 3
ATT_HEADS = 8
ATT_HEAD_DIM = 64
ATT_WIDTH = ATT_HEADS * ATT_HEAD_DIM
ROPE_DIM = ATT_HEAD_DIM // 4
ROPE_THETA = 500000.0
MEM_LEN = 256
MEM_HEADS = 4
MEM_HEAD_DIM = 128
MEM_WIDTH = MEM_HEADS * MEM_HEAD_DIM
N_BRANCHES = 3
IN_SIZES = (SSD_D_INNER, SSD_CONV_DIM, SSD_N_HEADS, ATT_N_GROUPS * 3 * ATT_WIDTH, ATT_WIDTH, MEM_WIDTH, MEM_WIDTH, N_BRANCHES * D_MODEL)
D_IN_PROJ = sum(IN_SIZES)
NEG_INF = -1e30

kernel_name = 'hybrid_ssd_dilated_attn_decoder_step'


def _rmsnorm(x, w):
    x32 = x.astype(jnp.float32)
    r = x32 * lax.rsqrt(jnp.mean(x32 * x32, axis=-1, keepdims=True) + NORM_EPS)
    return (r * w.astype(jnp.float32)).astype(x.dtype)


def _split_in(u):
    cuts = [int(c) for c in np.cumsum(IN_SIZES)[:-1]]
    return jnp.split(u, cuts, axis=-1)


def _rope(x, pos):
    half = ROPE_DIM // 2
    inv = jnp.power(ROPE_THETA, -jnp.arange(half, dtype=jnp.float32) * 2.0 / ROPE_DIM)
    ang = pos.astype(jnp.float32)[:, None] * inv[None, :]
    shape = (1, x.shape[1]) + (1,) * (x.ndim - 3) + (half,)
    cos = jnp.cos(ang).reshape(shape).astype(x.dtype)
    sin = jnp.sin(ang).reshape(shape).astype(x.dtype)
    x1, x2, rest = x[..., :half], x[..., half:ROPE_DIM], x[..., ROPE_DIM:]
    return jnp.concatenate([x1 * cos - x2 * sin, x2 * cos + x1 * sin, rest], axis=-1)


def _dilated_prompt(q, k, v, window, dil):
    b, s, h, dh = q.shape
    nk = window // dil
    length = s // dil
    nb = -(-length // nk)
    lp = nb * nk

    def to_blocks(a):
        a = a.reshape(b, length, dil, h, dh).transpose(0, 2, 1, 3, 4)
        a = jnp.pad(a, ((0, 0), (0, 0), (0, lp - length), (0, 0), (0, 0)))
        return a.reshape(b, dil, nb, nk, h, dh)

    def with_prev(a):
        prev = jnp.pad(a, ((0, 0), (0, 0), (1, 0), (0, 0), (0, 0), (0, 0)))[:, :, :nb]
        return jnp.concatenate([prev, a], axis=3)

    qb = to_blocks(q)
    kb = with_prev(to_blocks(k))
    vb = with_prev(to_blocks(v))
    sc = jnp.einsum('brnqhd,brnkhd->brnhqk', qb, kb, preferred_element_type=jnp.float32) * (dh ** -0.5)
    qi = jnp.arange(nk)[:, None]
    kj = jnp.arange(2 * nk)[None, :]
    rel = nk + qi - kj
    blk = jnp.arange(nb)[:, None, None]
    valid = (rel >= 0) & (rel <= nk) & ((blk - 1) * nk + kj >= 0)
    sc = jnp.where(valid[None, None, :, None], sc, NEG_INF)
    m = jnp.max(sc, axis=-1, keepdims=True)
    p = jnp.exp(sc - m)
    den = jnp.sum(p, axis=-1, keepdims=True)
    o = jnp.einsum('brnhqk,brnkhd->brnqhd', p.astype(v.dtype), vb, preferred_element_type=jnp.float32)
    o = o / jnp.swapaxes(den, 3, 4)
    lse = jnp.swapaxes((m + jnp.log(den))[..., 0], 3, 4)

    def from_blocks(a):
        a = a.reshape((b, dil, lp) + a.shape[4:])[:, :, :length]
        a = jnp.moveaxis(a, 1, 2)
        return a.reshape((b, s) + a.shape[3:])

    return from_blocks(o), from_blocks(lse)


def _dilated_sample(q, kv_new, buf, window, dil):
    b, t, h, dh = q.shape
    nk = window // dil
    wb = buf.shape[1]
    idx = wb + jnp.arange(t)[:, None] - dil * jnp.arange(nk + 1)[None, :]
    valid = idx >= 0
    kv_old = buf[:, jnp.clip(idx, 0, wb - 1)]
    kv_cur = kv_new.astype(buf.dtype)[:, jnp.clip(idx - wb, 0, t - 1)]
    kv_g = jnp.where((idx >= wb)[None, :, :, None, None, None], kv_cur, kv_old)
    sc = jnp.einsum('bthd,btkhd->bthk', q, kv_g[:, :, :, 0], preferred_element_type=jnp.float32) * (dh ** -0.5)
    sc = jnp.where(valid[None, :, None, :], sc, NEG_INF)
    m = jnp.max(sc, axis=-1, keepdims=True)
    p = jnp.exp(sc - m)
    den = jnp.sum(p, axis=-1, keepdims=True)
    o = jnp.einsum('bthk,btkhd->bthd', p.astype(kv_g.dtype), kv_g[:, :, :, 1], preferred_element_type=jnp.float32) / den
    lse = (m + jnp.log(den))[..., 0]
    return o, lse


def _mem_kv(mem, mem_norm_w, w_mem_kv):
    b, m, _ = mem.shape
    return (_rmsnorm(mem, mem_norm_w) @ w_mem_kv).reshape(b, m, 2, MEM_HEADS, MEM_HEAD_DIM)


def _mem_attn(q, kv):
    b, t = q.shape[0], q.shape[1]
    sc = jnp.einsum('bthd,bmhd->bhtm', q, kv[:, :, 0], preferred_element_type=jnp.float32) * (MEM_HEAD_DIM ** -0.5)
    p = jax.nn.softmax(sc, axis=-1)
    o = jnp.einsum('bhtm,bmhd->bthd', p.astype(kv.dtype), kv[:, :, 1])
    return o.reshape(b, t, MEM_WIDTH)


def _ssd_scan(x, dt, a_neg, bm, cm, init_state):
    b, t, h, p = x.shape
    g, n = bm.shape[2], bm.shape[3]
    e = h // g
    q = min(SSD_CHUNK, t)
    nc = -(-t // q)
    pad = nc * q - t

    def padf(a):
        return jnp.pad(a, ((0, 0), (0, pad)) + ((0, 0),) * (a.ndim - 2))

    x, dt, bm, cm = padf(x), padf(dt), padf(bm), padf(cm)
    xdt = (x * dt[..., None]).reshape(b, nc, q, g, e, p)
    acs = jnp.cumsum((dt * a_neg).reshape(b, nc, q, g, e), axis=2)
    bm = bm.reshape(b, nc, q, g, n)
    cm = cm.reshape(b, nc, q, g, n)
    diff = acs[:, :, :, None] - acs[:, :, None, :]
    tril = jnp.tril(jnp.ones((q, q), dtype=bool))[None, None, :, :, None, None]
    lmat = jnp.exp(jnp.where(tril, diff, -jnp.inf))
    cb = jnp.einsum('bclgn,bcsgn->bclsg', cm, bm)
    y_diag = jnp.einsum('bclsg,bclsge,bcsgep->bclgep', cb, lmat, xdt)
    decay_out = jnp.exp(acs[:, :, -1:] - acs)
    st = jnp.einsum('bclgn,bclge,bclgep->bcgepn', bm, decay_out, xdt)
    chunk_decay = jnp.exp(acs[:, :, -1])

    def step(s, inp):
        dec, add = inp
        return s * dec[..., None, None] + add, s

    final, prev = lax.scan(step, init_state.reshape(b, g, e, p, n),
                           (chunk_decay.transpose(1, 0, 2, 3), st.transpose(1, 0, 2, 3, 4, 5)))
    prev = prev.transpose(1, 0, 2, 3, 4, 5)
    y_off = jnp.einsum('bclgn,bcgepn,bclge->bclgep', cm, prev, jnp.exp(acs))
    y = (y_diag + y_off).reshape(b, nc * q, h, p)[:, :t]
    return y, final.reshape(b, h, p, n)


def _ssd_branch(z, xbc, dt_raw, conv_prev, ssm_prev, conv_w, conv_b, dt_bias, a_log, d_skip, ssd_norm_w):
    b, t, _ = xbc.shape
    xp = jnp.concatenate([conv_prev.astype(xbc.dtype), xbc], axis=1)
    acc = conv_b + xp[:, 0:t] * conv_w[0]
    for kk in range(1, SSD_CONV):
        acc = acc + xp[:, kk:kk + t] * conv_w[kk]
    xc = jax.nn.silu(acc)
    new_conv = xp[:, xp.shape[1] - (SSD_CONV - 1):]
    gn = SSD_N_GROUPS * SSD_D_STATE
    xs = xc[..., :SSD_D_INNER].reshape(b, t, SSD_N_HEADS, SSD_HEAD_DIM).astype(jnp.float32)
    bm = xc[..., SSD_D_INNER:SSD_D_INNER + gn].reshape(b, t, SSD_N_GROUPS, SSD_D_STATE).astype(jnp.float32)
    cm = xc[..., SSD_D_INNER + gn:].reshape(b, t, SSD_N_GROUPS, SSD_D_STATE).astype(jnp.float32)
    dt = jax.nn.softplus(dt_raw.astype(jnp.float32) + dt_bias.astype(jnp.float32))
    a_neg = -jnp.exp(a_log.astype(jnp.float32))
    y, new_ssm = _ssd_scan(xs, dt, a_neg, bm, cm, ssm_prev.astype(jnp.float32))
    y = y + d_skip.astype(jnp.float32)[:, None] * xs
    y = y.reshape(b, t, SSD_D_INNER).astype(z.dtype)
    y = _rmsnorm(y * jax.nn.silu(z), ssd_norm_w)
    return y, new_conv, new_ssm.astype(ssm_prev.dtype)


def _layer(x, pos0, mem_kv, conv_prev, ssm_prev, win_bufs, norm_in_w, w_in, conv_w, conv_b, dt_bias, a_log,
           d_skip, ssd_norm_w, w_br_ssd, w_br_att, w_br_mem, w_out, norm_f_w):
    b, t, _ = x.shape
    h = _rmsnorm(x, norm_in_w)
    u = h @ w_in
    z_ssd, xbc, dt_raw, qkv, z_att, q_mem, z_mem, gate_raw = _split_in(u)
    y_ssd, new_conv, new_ssm = _ssd_branch(z_ssd, xbc, dt_raw, conv_prev, ssm_prev, conv_w, conv_b,
                                           dt_bias, a_log, d_skip, ssd_norm_w)
    qkv = qkv.reshape(b, t, ATT_N_GROUPS, 3, ATT_HEADS, ATT_HEAD_DIM)
    pos = pos0 + jnp.arange(t)
    q = _rope(qkv[:, :, :, 0], pos)
    k = _rope(qkv[:, :, :, 1], pos)
    v = qkv[:, :, :, 2]
    kv = jnp.stack([k, v], axis=3)
    outs, lses, new_win = [], [], []
    for gi in range(ATT_N_GROUPS):
        w, d = ATT_WINDOWS[gi], ATT_DILATIONS[gi]
        if win_bufs is None:
            o, l = _dilated_prompt(q[:, :, gi], k[:, :, gi], v[:, :, gi], w, d)
            new_win.append(kv[:, t - min(w, t):, gi])
        else:
            o, l = _dilated_sample(q[:, :, gi], kv[:, :, gi], win_bufs[gi], w, d)
            new_win.append(kv[:, :, gi])
        outs.append(o)
        lses.append(l)
    wts = jax.nn.softmax(jnp.stack(lses), axis=0)
    o_att = jnp.sum(wts[..., None] * jnp.stack(outs), axis=0).reshape(b, t, ATT_WIDTH).astype(x.dtype)
    y_att = o_att * jax.nn.silu(z_att)
    y_mem = _mem_attn(q_mem.reshape(b, t, MEM_HEADS, MEM_HEAD_DIM), mem_kv).astype(x.dtype) * jax.nn.silu(z_mem)
    gates = jax.nn.sigmoid(gate_raw).reshape(b, t, N_BRANCHES, D_MODEL)
    merged = (gates[:, :, 0] * (y_ssd @ w_br_ssd) + gates[:, :, 1] * (y_att @ w_br_att)
              + gates[:, :, 2] * (y_mem @ w_br_mem))
    y = _rmsnorm(x + merged @ w_out, norm_f_w)
    return y, new_win, new_conv, new_ssm


def setup_inputs(seed: int = 0) -> dict:
    key = jax.random.key(seed)
    ks = jax.random.split(key, 32)
    f32 = jnp.float32

    def nrm(k, shape, scale):
        return scale * jax.random.normal(k, shape, f32)

    dt0 = jnp.exp(jax.random.uniform(ks[0], (SSD_N_HEADS,), f32, math.log(1e-3), math.log(1e-1)))
    win_shape = lambda w: (DEC_BATCH, min(w, PAST_LEN), 2, ATT_HEADS, ATT_HEAD_DIM)
    return {
        'x_prompt': nrm(ks[1], (BATCH, SEQ, D_MODEL), 1.0),
        'x_sample': nrm(ks[2], (DEC_BATCH, DEC_SEQ, D_MODEL), 1.0),
        'mem_prompt': nrm(ks[3], (BATCH, MEM_LEN, D_MODEL), 1.0),
        'cache_win128_kv': nrm(ks[4], win_shape(ATT_WINDOWS[0]), 1.0),
        'cache_win512_kv': nrm(ks[5], win_shape(ATT_WINDOWS[1]), 1.0),
        'cache_win2048_kv': nrm(ks[6], win_shape(ATT_WINDOWS[2]), 1.0),
        'cache_mem_kv': nrm(ks[7], (DEC_BATCH, MEM_LEN, 2, MEM_HEADS, MEM_HEAD_DIM), 1.0),
        'state_conv': nrm(ks[8], (DEC_BATCH, SSD_CONV - 1, SSD_CONV_DIM), 1.0),
        'state_ssm': nrm(ks[9], (DEC_BATCH, SSD_N_HEADS, SSD_HEAD_DIM, SSD_D_STATE), 0.1),
        'norm_in_w': 1.0 + nrm(ks[10], (D_MODEL,), 0.02),
        'w_in': nrm(ks[11], (D_MODEL, D_IN_PROJ), D_MODEL ** -0.5),
        'conv_w': nrm(ks[12], (SSD_CONV, SSD_CONV_DIM), SSD_CONV ** -0.5),
        'conv_b': nrm(ks[13], (SSD_CONV_DIM,), 0.01),
        'dt_bias': dt0 + jnp.log(-jnp.expm1(-dt0)),
        'a_log': jnp.log(jax.random.uniform(ks[14], (SSD_N_HEADS,), f32, 1.0, 16.0)),
        'd_skip': 1.0 + nrm(ks[15], (SSD_N_HEADS,), 0.1),
        'ssd_norm_w': 1.0 + nrm(ks[16], (SSD_D_INNER,), 0.02),
        'mem_norm_w': 1.0 + nrm(ks[17], (D_MODEL,), 0.02),
        'w_mem_kv': nrm(ks[18], (D_MODEL, 2 * MEM_WIDTH), D_MODEL ** -0.5),
        'w_br_ssd': nrm(ks[19], (SSD_D_INNER, D_MODEL), SSD_D_INNER ** -0.5),
        'w_br_att': nrm(ks[20], (ATT_WIDTH, D_MODEL), ATT_WIDTH ** -0.5),
        'w_br_mem': nrm(ks[21], (MEM_WIDTH, D_MODEL), MEM_WIDTH ** -0.5),
        'w_out': nrm(ks[22], (D_MODEL, D_MODEL), D_MODEL ** -0.5),
        'norm_f_w': 1.0 + nrm(ks[23], (D_MODEL,), 0.02),
    }


def reference(x_prompt, x_sample, mem_prompt, cache_win128_kv, cache_win512_kv, cache_win2048_kv, cache_mem_kv,
              state_conv, state_ssm, norm_in_w, w_in, conv_w, conv_b, dt_bias, a_log, d_skip, ssd_norm_w,
              mem_norm_w, w_mem_kv, w_br_ssd, w_br_att, w_br_mem, w_out, norm_f_w):
    bp = x_prompt.shape[0]
    p_mem_kv = _mem_kv(mem_prompt, mem_norm_w, w_mem_kv)
    conv0 = jnp.zeros((bp, SSD_CONV - 1, SSD_CONV_DIM), x_prompt.dtype)
    ssm0 = jnp.zeros((bp, SSD_N_HEADS, SSD_HEAD_DIM, SSD_D_STATE), jnp.float32)
    y_prompt, p_win, p_conv, p_ssm = _layer(
        x_prompt, 0, p_mem_kv, conv0, ssm0, None, norm_in_w, w_in, conv_w, conv_b, dt_bias, a_log, d_skip,
        ssd_norm_w, w_br_ssd, w_br_att, w_br_mem, w_out, norm_f_w)
    y_sample, s_win, s_conv, s_ssm = _layer(
        x_sample, PAST_LEN, cache_mem_kv, state_conv, state_ssm, (cache_win128_kv, cache_win512_kv, cache_win2048_kv),
        norm_in_w, w_in, conv_w, conv_b, dt_bias, a_log, d_skip, ssd_norm_w, w_br_ssd, w_br_att, w_br_mem,
        w_out, norm_f_w)
    return (y_prompt, y_sample, p_win[0], p_win[1], p_win[2], p_mem_kv, p_conv, p_ssm,
            s_win[0], s_win[1], s_win[2], s_conv, s_ssm)
```

```python
import functools
import math

import jax
import jax.numpy as jnp
from jax import lax
from jax.experimental import pallas as pl
from jax.experimental.pallas import tpu as pltpu

F32 = jnp.float32
BF16 = jnp.bfloat16

D_MODEL = 1024
NORM_EPS = 1e-6
SSD_HEAD_DIM = 64
SSD_N_HEADS = 16
SSD_N_GROUPS = 4
SSD_D_STATE = 128
SSD_CONV = 4
SSD_CHUNK = 128
SSD_CONV_DIM = 2048
ATT_WINDOWS = (128, 512, 2048)
ATT_DILATIONS = (1, 4, 16)
ATT_HEADS = 8
ATT_HEAD_DIM = 64
ATT_WIDTH = 512
ROPE_DIM = 16
ROPE_THETA = 500000.0
MEM_LEN = 256
MEM_HEADS = 4
MEM_HEAD_DIM = 128
MEM_WIDTH = 512
NEG_INF = -1e30
PAST_LEN = 8192

LANES = 128
COL_TILE = 512
N_COL_TILES = 24
QKV_TILE0 = 6
REST_TILE0 = 15
VMEM_LIMIT = 56 * 1024 * 1024


def _cparams(sem):
    return pltpu.CompilerParams(dimension_semantics=sem, vmem_limit_bytes=VMEM_LIMIT)


def _dot(a, b):
    return jnp.dot(a, b, preferred_element_type=F32)


def _dot_nt(a, b):
    return lax.dot_general(a, b, (((1,), (1,)), ((), ())), preferred_element_type=F32)


def _rms(x, w):
    return x * lax.rsqrt(jnp.mean(x * x, axis=-1, keepdims=True) + NORM_EPS) * w


def _sigmoid(x):
    return 1.0 / (1.0 + jnp.exp(-x))


def _silu(x):
    return x * _sigmoid(x)


def _softplus(x):
    return jnp.maximum(x, 0.0) + jnp.log(1.0 + jnp.exp(-jnp.abs(x)))


def _split3(x):
    hi = x.astype(BF16)
    r1 = x - hi.astype(F32)
    mid = r1.astype(BF16)
    lo = (r1 - mid.astype(F32)).astype(BF16)
    return hi, mid, lo


def _dot_exact_lhs01(m01, x):
    hi, mid, lo = _split3(x)
    return _dot(m01, hi) + _dot(m01, mid) + _dot(m01, lo)


def _dot_exact_rhs01(x, m01):
    hi, mid, lo = _split3(x)
    return _dot(hi, m01) + _dot(mid, m01) + _dot(lo, m01)


def _mem_kv_kernel(x_ref, nw_ref, w_ref, o_ref, ob_ref):
    h = _rms(x_ref[...], nw_ref[...]).astype(BF16)
    acc = _dot(h, w_ref[...])
    o_ref[...] = acc
    ob_ref[...] = acc.astype(BF16)


def _mem_kv(mem2d, nw, w_bf):
    m = mem2d.shape[0]
    tm = 256
    return pl.pallas_call(
        _mem_kv_kernel,
        grid=(m // tm,),
        in_specs=[
            pl.BlockSpec((tm, D_MODEL), lambda i: (i, 0)),
            pl.BlockSpec((1, D_MODEL), lambda i: (0, 0)),
            pl.BlockSpec((D_MODEL, 2 * MEM_WIDTH), lambda i: (0, 0)),
        ],
        out_specs=[
            pl.BlockSpec((tm, 2 * MEM_WIDTH), lambda i: (i, 0)),
            pl.BlockSpec((tm, 2 * MEM_WIDTH), lambda i: (i, 0)),
        ],
        out_shape=[
            jax.ShapeDtypeStruct((m, 2 * MEM_WIDTH), F32),
            jax.ShapeDtypeStruct((m, 2 * MEM_WIDTH), BF16),
        ],
        compiler_params=_cparams(("parallel",)),
        name="mem_kv",
    )(mem2d, nw, w_bf)


def _rope_tables(pos):
    half = ROPE_DIM // 2
    inv = jnp.power(ROPE_THETA, -jnp.arange(half, dtype=F32) * 2.0 / ROPE_DIM)
    ang = pos.astype(F32)[:, None] * inv[None, :]
    cos, sin = jnp.cos(ang), jnp.sin(ang)
    m = jnp.arange(LANES) % ATT_HEAD_DIM
    idx = m % half
    c = jnp.where(m[None, :] < ROPE_DIM, cos[:, idx], 1.0)
    s1 = jnp.where(m[None, :] < half, -sin[:, idx], 0.0)
    s2 = jnp.where((m[None, :] >= half) & (m[None, :] < ROPE_DIM), sin[:, idx], 0.0)
    return c.astype(F32), s1.astype(F32), s2.astype(F32)


def _in_proj_kernel(x_ref, nw_ref, w_ref, wdt_ref, c_ref, s1_ref, s2_ref,
                    zx_ref, q0_ref, q1_ref, q2_ref, rest_ref, dt_ref,
                    h_ref, acc_ref, tmp_ref, *, tm, dils):
    j = pl.program_id(1)
    nslab = COL_TILE // LANES

    @pl.when(j == 0)
    def _():
        h = _rms(x_ref[...], nw_ref[...]).astype(BF16)
        h_ref[...] = h
        dt_ref[...] = _dot(h, wdt_ref[...])

    acc = _dot(h_ref[...], w_ref[...])
    for c in range(nslab):
        acc_ref[c] = acc[:, c * LANES:(c + 1) * LANES]

    def plain(out_ref):
        for c in range(nslab):
            out_ref[:, c * LANES:(c + 1) * LANES] = acc_ref[c].astype(out_ref.dtype)

    @pl.when(j < QKV_TILE0)
    def _():
        plain(zx_ref)

    @pl.when(j >= REST_TILE0)
    def _():
        plain(rest_ref)

    for g, (out_ref, d) in enumerate(zip((q0_ref, q1_ref, q2_ref), dils)):
        for kind in range(3):

            @pl.when(j == QKV_TILE0 + 3 * g + kind)
            def _(out_ref=out_ref, d=d, kind=kind):
                for c in range(nslab):
                    a = acc_ref[c]
                    if kind < 2:
                        a = (a * c_ref[...] + pltpu.roll(a, LANES - ROPE_DIM // 2, 1) * s1_ref[...]
                             + pltpu.roll(a, ROPE_DIM // 2, 1) * s2_ref[...])
                    if kind == 0:
                        a = a * (ATT_HEAD_DIM ** -0.5)
                    lanes = slice(c * LANES, (c + 1) * LANES)
                    if d == 1:
                        out_ref[0, :, lanes] = a.astype(out_ref.dtype)
                    elif d == 4:
                        acc_ref[c] = a
                        for r in range(4):
                            out_ref[r, :, lanes] = acc_ref[c, pl.ds(r, tm // 4, stride=4), :].astype(out_ref.dtype)
                    else:
                        acc_ref[c] = a
                        q4 = tm // 4
                        for r1 in range(4):
                            tmp_ref[c, r1 * q4:(r1 + 1) * q4, :] = acc_ref[c, pl.ds(r1, q4, stride=4), :]
                        for r1 in range(4):
                            for r2 in range(4):
                                out_ref[r1 + 4 * r2, :, lanes] = tmp_ref[
                                    c, pl.ds(r1 * q4 + r2, tm // 16, stride=4), :].astype(out_ref.dtype)


def _in_proj(x2d, nw, w_main, w_dt, tabs, *, seq, tm, dils, out_dtype):
    t = x2d.shape[0]
    nb = t // seq
    tiles_per_seq = seq // tm
    tab_blocks = tabs[0].shape[0] // tm

    def qkv_spec(g, d):
        def imap(i, j):
            return (i // tiles_per_seq, 0, i % tiles_per_seq, jnp.clip(j - (QKV_TILE0 + 3 * g), 0, 2))
        return pl.BlockSpec((None, d, tm // d, COL_TILE), imap)

    tab_spec = pl.BlockSpec((tm, LANES), lambda i, j: (i % tab_blocks, 0))
    kern = functools.partial(_in_proj_kernel, tm=tm, dils=dils)
    return pl.pallas_call(
        kern,
        grid=(t // tm, N_COL_TILES),
        in_specs=[
            pl.BlockSpec((tm, D_MODEL), lambda i, j: (i, 0)),
            pl.BlockSpec((1, D_MODEL), lambda i, j: (0, 0)),
            pl.BlockSpec((D_MODEL, COL_TILE), lambda i, j: (0, j)),
            pl.BlockSpec((D_MODEL, LANES), lambda i, j: (0, 0)),
            tab_spec, tab_spec, tab_spec,
        ],
        out_specs=[
            pl.BlockSpec((tm, COL_TILE), lambda i, j: (i, jnp.minimum(j, QKV_TILE0 - 1))),
            qkv_spec(0, dils[0]), qkv_spec(1, dils[1]), qkv_spec(2, dils[2]),
            pl.BlockSpec((tm, COL_TILE), lambda i, j: (i, jnp.maximum(j - REST_TILE0, 0))),
            pl.BlockSpec((tm, LANES), lambda i, j: (i, 0)),
        ],
        out_shape=[
            jax.ShapeDtypeStruct((t, QKV_TILE0 * COL_TILE), out_dtype),
            jax.ShapeDtypeStruct((nb, dils[0], seq // dils[0], 3 * COL_TILE), out_dtype),
            jax.ShapeDtypeStruct((nb, dils[1], seq // dils[1], 3 * COL_TILE), out_dtype),
            jax.ShapeDtypeStruct((nb, dils[2], seq // dils[2], 3 * COL_TILE), out_dtype),
            jax.ShapeDtypeStruct((t, (N_COL_TILES - REST_TILE0) * COL_TILE), out_dtype),
            jax.ShapeDtypeStruct((t, LANES), F32),
        ],
        scratch_shapes=[
            pltpu.VMEM((tm, D_MODEL), BF16),
            pltpu.VMEM((COL_TILE // LANES, tm, LANES), F32),
            pltpu.VMEM((COL_TILE // LANES, tm, LANES), F32),
        ],
        compiler_params=_cparams(("parallel", "arbitrary")),
        name="in_proj",
    )(x2d, nw, w_main, w_dt, *tabs)


def _ssd_small(dt_raw, dtb_row, a_row, dtb_col, a_col, same01, same01_t, tot01, tot01_t):
    dt = _softplus(dt_raw + dtb_row)
    dta = dt * a_row
    acs = _dot_exact_lhs01(same01, dta)
    tot = _dot_exact_lhs01(tot01, dta)
    dtd = dt * jnp.exp(tot - acs)
    dt_raw_t = dt_raw.T
    dt_t = _softplus(dt_raw_t + dtb_col)
    dta_t = dt_t * a_col
    acs_t = _dot_exact_rhs01(dta_t, same01_t)
    tot_t = _dot_exact_rhs01(dta_t, tot01_t)
    dtd_t = dt_t * jnp.exp(tot_t - acs_t)
    return dt, acs, tot, dtd, dt_t, acs_t, dtd_t


def _ssd_pairs(x, bm, cm, acs, acs_t, dt_t, dtd_t, mask, st_ref, y_ref_write, st_scale_row):
    lane = lax.broadcasted_iota(jnp.int32, (1, LANES), 1)
    half = [lane < ATT_HEAD_DIM, lane >= ATT_HEAD_DIM]
    bm_t = [bm[:, g * SSD_D_STATE:(g + 1) * SSD_D_STATE].T for g in range(SSD_N_GROUPS)]
    cms = [cm[:, g * SSD_D_STATE:(g + 1) * SSD_D_STATE] for g in range(SSD_N_GROUPS)]
    cb = [_dot(cms[g].astype(BF16), bm_t[g].astype(BF16)) for g in range(SSD_N_GROUPS)]
    for p in range(SSD_N_HEADS // 2):
        g = p // 2
        xp = x[:, p * LANES:(p + 1) * LANES]
        if st_ref is not None:
            stp = st_ref[:, p * LANES:(p + 1) * LANES]
        y_pair = jnp.zeros((SSD_CHUNK, LANES), F32)
        st_add = jnp.zeros((SSD_D_STATE, LANES), F32)
        for hh in range(2):
            h = 2 * p + hh
            col = acs[:, h:h + 1]
            row = acs_t[h:h + 1, :]
            lmat = jnp.exp(jnp.where(mask, col - row, NEG_INF))
            m = cb[g] * lmat * dt_t[h:h + 1, :]
            xm = jnp.where(half[hh], xp, 0.0).astype(BF16)
            if st_ref is not None:
                cx = cms[g] * jnp.exp(col)
                lhs = jnp.concatenate([m, cx], axis=1).astype(BF16)
                rhs = jnp.concatenate([xm, jnp.where(half[hh], stp, 0.0).astype(BF16)], axis=0)
                y_pair = y_pair + _dot(lhs, rhs)
                bt_h = (bm_t[g] * dtd_t[h:h + 1, :]).astype(BF16)
                st_add = st_add + _dot(bt_h, xm)
            else:
                y_pair = y_pair + _dot(m.astype(BF16), xm)
        y_ref_write(p, y_pair)
        if st_ref is not None:
            scale = jnp.where(half[0], st_scale_row[:, 2 * p:2 * p + 1], st_scale_row[:, 2 * p + 1:2 * p + 2])
            st_ref[:, p * LANES:(p + 1) * LANES] = stp * scale + st_add


def _ssd_prompt_kernel(z_ref, xs_ref, bc_ref, dt_ref, cw_ref, cb_ref, dtb_ref, a_ref, dtbc_ref, ac_ref,
                       dsk_ref, nw_ref, y_ref, ssm_ref, xbuf, st_ref, ybuf):
    c = pl.program_id(1)
    q = SSD_CHUNK
    hist = 8

    @pl.when(c == 0)
    def _():
        xbuf[0:hist, :] = jnp.zeros((hist, SSD_CONV_DIM), F32)
        st_ref[...] = jnp.zeros(st_ref.shape, F32)

    xbuf[hist:hist + q, 0:D_MODEL] = xs_ref[...].astype(F32)
    xbuf[hist:hist + q, D_MODEL:SSD_CONV_DIM] = bc_ref[...].astype(F32)
    acc = cb_ref[...] + cw_ref[SSD_CONV - 1:SSD_CONV, :] * xbuf[hist:hist + q, :]
    for s in range(1, SSD_CONV):
        acc = acc + cw_ref[SSD_CONV - 1 - s:SSD_CONV - s, :] * xbuf[hist - s:hist - s + q, :]
    xbuf[0:hist, :] = xbuf[q:q + hist, :]
    xc = _silu(acc)
    x = xc[:, :D_MODEL]
    bm = xc[:, D_MODEL:D_MODEL + 512]
    cm = xc[:, D_MODEL + 512:]

    ri = lax.broadcasted_iota(jnp.int32, (q, q), 0)
    ci = lax.broadcasted_iota(jnp.int32, (q, q), 1)
    tril = ri >= ci
    tril01 = jnp.where(tril, 1.0, 0.0).astype(BF16)
    triu01 = jnp.where(ri <= ci, 1.0, 0.0).astype(BF16)
    ones01 = jnp.ones((q, q), BF16)
    dt, acs, tot, dtd, dt_t, acs_t, dtd_t = _ssd_small(
        dt_ref[...], dtb_ref[...], a_ref[...], dtbc_ref[...], ac_ref[...], tril01, triu01, ones01, ones01)
    decay_row = jnp.exp(tot[0:1, :])

    def write_y(p, y_pair):
        ybuf[:, p * LANES:(p + 1) * LANES] = y_pair

    _ssd_pairs(x, bm, cm, acs, acs_t, dt_t, dtd_t, tril, st_ref, write_y, decay_row)

    y = (ybuf[...] + dsk_ref[...] * x) * _silu(z_ref[...].astype(F32))
    y_ref[...] = _rms(y, nw_ref[...]).astype(y_ref.dtype)

    @pl.when(c == pl.num_programs(1) - 1)
    def _():
        for p in range(D_MODEL // LANES):
            ssm_ref[p * LANES:(p + 1) * LANES, :] = st_ref[:, p * LANES:(p + 1) * LANES].T


def _ssd_prompt(zx, dt, cw, cb, dtb, a_row, dtb_col, a_col, dsk, nw, *, nbatch, seq):
    nc = seq // SSD_CHUNK
    q = SSD_CHUNK
    row = lambda b, c: b * nc + c
    vec = lambda n: pl.BlockSpec((1, n), lambda b, c: (0, 0))
    col = pl.BlockSpec((LANES, 1), lambda b, c: (0, 0))
    return pl.pallas_call(
        _ssd_prompt_kernel,
        grid=(nbatch, nc),
        in_specs=[
            pl.BlockSpec((q, D_MODEL), lambda b, c: (row(b, c), 0)),
            pl.BlockSpec((q, D_MODEL), lambda b, c: (row(b, c), 1)),
            pl.BlockSpec((q, D_MODEL), lambda b, c: (row(b, c), 2)),
            pl.BlockSpec((q, LANES), lambda b, c: (row(b, c), 0)),
            pl.BlockSpec((SSD_CONV, SSD_CONV_DIM), lambda b, c: (0, 0)),
            vec(SSD_CONV_DIM), vec(LANES), vec(LANES), col, col, vec(D_MODEL), vec(D_MODEL),
        ],
        out_specs=[
            pl.BlockSpec((q, D_MODEL), lambda b, c: (row(b, c), 0)),
            pl.BlockSpec((None, D_MODEL, SSD_D_STATE), lambda b, c: (b, 0, 0)),
        ],
        out_shape=[
            jax.ShapeDtypeStruct((nbatch * seq, D_MODEL), BF16),
            jax.ShapeDtypeStruct((nbatch, D_MODEL, SSD_D_STATE), F32),
        ],
        scratch_shapes=[
            pltpu.VMEM((q + 8, SSD_CONV_DIM), F32),
            pltpu.VMEM((SSD_D_STATE, D_MODEL), F32),
            pltpu.VMEM((q, D_MODEL), F32),
        ],
        compiler_params=_cparams(("parallel", "arbitrary")),
        name="ssd_prompt",
    )(zx, zx, zx, dt, cw, cb, dtb, a_row, dtb_col, a_col, dsk, nw)


def _expand_heads(v, e01):
    return _dot_exact_rhs01(v, e01)


def _ssd_sample_rows_kernel(xs_ref, bc_ref, dt_ref, hist_ref, cw_ref, cb_ref, dtb_ref, a_ref, dtbc_ref, ac_ref,
                            dsk_ref, e01_ref,
                            ypart_ref, ea_ref, xdt_ref, bm_ref, cm_ref, cd_ref, ybuf, *, tdec):
    q = SSD_CHUNK
    x_raw = jnp.concatenate([xs_ref[...], bc_ref[...]], axis=1)
    hist = hist_ref[...]
    rowi = lax.broadcasted_iota(jnp.int32, (q, 1), 0)
    tpos = rowi % tdec
    acc = cb_ref[...] + cw_ref[SSD_CONV - 1:SSD_CONV, :] * x_raw
    for s in range(1, SSD_CONV):
        prev = jnp.where(tpos >= s, pltpu.roll(x_raw, s, 0), pltpu.roll(hist, q - tdec + s, 0))
        acc = acc + cw_ref[SSD_CONV - 1 - s:SSD_CONV - s, :] * prev
    xc = _silu(acc)
    x = xc[:, :D_MODEL]
    bm = xc[:, D_MODEL:D_MODEL + 512]
    cm = xc[:, D_MODEL + 512:]

    ri = lax.broadcasted_iota(jnp.int32, (q, q), 0)
    ci = lax.broadcasted_iota(jnp.int32, (q, q), 1)
    same = (ri // tdec) == (ci // tdec)
    mask = same & (ri >= ci)
    low01 = jnp.where(mask, 1.0, 0.0).astype(BF16)
    up01 = jnp.where(same & (ri <= ci), 1.0, 0.0).astype(BF16)
    same01 = jnp.where(same, 1.0, 0.0).astype(BF16)
    dt, acs, tot, dtd, dt_t, acs_t, dtd_t = _ssd_small(
        dt_ref[...], dtb_ref[...], a_ref[...], dtbc_ref[...], ac_ref[...], low01, up01, same01, same01)

    def write_y(p, y_pair):
        ybuf[:, p * LANES:(p + 1) * LANES] = y_pair

    _ssd_pairs(x, bm, cm, acs, acs_t, dt_t, dtd_t, mask, None, write_y, None)

    e01 = e01_ref[...]
    ypart_ref[...] = ybuf[...] + dsk_ref[...] * x
    ea_ref[...] = _expand_heads(jnp.exp(acs), e01)
    xd = x * _expand_heads(dtd, e01)
    for p in range(D_MODEL // LANES):
        xdt_ref[p * LANES:(p + 1) * LANES, :] = xd[:, p * LANES:(p + 1) * LANES].T.astype(BF16)
    bm_ref[...] = bm.astype(BF16)
    cm_ref[...] = cm.astype(BF16)
    cd_ref[...] = jnp.exp(tot)


def _ssd_sample_rows(zx, dt, hist, cw, cb, dtb, a_row, dtb_col, a_col, dsk, e01, *, tdec):
    t = zx.shape[0]
    q = SSD_CHUNK
    vec = lambda n: pl.BlockSpec((1, n), lambda i: (0, 0))
    col = pl.BlockSpec((LANES, 1), lambda i: (0, 0))
    kern = functools.partial(_ssd_sample_rows_kernel, tdec=tdec)
    return pl.pallas_call(
        kern,
        grid=(t // q,),
        in_specs=[
            pl.BlockSpec((q, D_MODEL), lambda i: (i, 1)),
            pl.BlockSpec((q, D_MODEL), lambda i: (i, 2)),
            pl.BlockSpec((q, LANES), lambda i: (i, 0)),
            pl.BlockSpec((q, SSD_CONV_DIM), lambda i: (i, 0)),
            pl.BlockSpec((SSD_CONV, SSD_CONV_DIM), lambda i: (0, 0)),
            vec(SSD_CONV_DIM), vec(LANES), vec(LANES), col, col, vec(D_MODEL),
            pl.BlockSpec((LANES, D_MODEL), lambda i: (0, 0)),
        ],
        out_specs=[
            pl.BlockSpec((q, D_MODEL), lambda i: (i, 0)),
            pl.BlockSpec((q, D_MODEL), lambda i: (i, 0)),
            pl.BlockSpec((D_MODEL, q), lambda i: (0, i)),
            pl.BlockSpec((q, 512), lambda i: (i, 0)),
            pl.BlockSpec((q, 512), lambda i: (i, 0)),
            pl.BlockSpec((q, LANES), lambda i: (i, 0)),
        ],
        out_shape=[
            jax.ShapeDtypeStruct((t, D_MODEL), F32),
            jax.ShapeDtypeStruct((t, D_MODEL), F32),
            jax.ShapeDtypeStruct((D_MODEL, t), BF16),
            jax.ShapeDtypeStruct((t, 512), BF16),
            jax.ShapeDtypeStruct((t, 512), BF16),
            jax.ShapeDtypeStruct((t, LANES), F32),
        ],
        scratch_shapes=[pltpu.VMEM((q, D_MODEL), F32)],
        compiler_params=_cparams(("parallel",)),
        name="ssd_sample_rows",
    )(zx, zx, dt, hist, cw, cb, dtb, a_row, dtb_col, a_col, dsk, e01)


def _ssd_sample_state_kernel(cd_ref, z_ref, ypart_ref, ea_ref, xdt_ref, bm_ref, cm_ref, st_ref, nw_ref,
                             y_ref, sto_ref, *, tdec, gseq):
    i = pl.program_id(0)
    q = SSD_CHUNK
    rows = gseq * tdec
    steps_per_blk = q // rows
    lane = lax.broadcasted_iota(jnp.int32, (1, q), 1)
    rg = 16
    rowi = lax.broadcasted_iota(jnp.int32, (rg, 1), 0)
    lane_base = (i % steps_per_blk) * rows
    seq_per_rg = rg // tdec
    yoff_groups = []
    for q8 in range(rows // rg):
        yoff8 = jnp.zeros((rg, D_MODEL), F32)
        for sj in range(seq_per_rg):
            jj = q8 * seq_per_rg + sj
            b = i * gseq + jj
            lmask = (lane >= lane_base + jj * tdec) & (lane < lane_base + (jj + 1) * tdec)
            parts = []
            for g in range(SSD_N_GROUPS):
                s_g = st_ref[jj, g * 256:(g + 1) * 256, :]
                c8 = cm_ref[q8 * rg:(q8 + 1) * rg, g * SSD_D_STATE:(g + 1) * SSD_D_STATE]
                parts.append(_dot_nt(c8, s_g.astype(BF16)))
                lhs = jnp.where(lmask, xdt_ref[g * 256:(g + 1) * 256, :], jnp.zeros((), BF16))
                add = _dot(lhs, bm_ref[:, g * SSD_D_STATE:(g + 1) * SSD_D_STATE])
                for hh in range(4):
                    h = 4 * g + hh
                    sl = slice(hh * SSD_HEAD_DIM, (hh + 1) * SSD_HEAD_DIM)
                    sto_ref[jj, g * 256 + hh * 64:g * 256 + (hh + 1) * 64, :] = (
                        s_g[sl, :] * cd_ref[b, h] + add[sl, :])
            yo = jnp.concatenate(parts, axis=1)
            in_seq = (rowi >= sj * tdec) & (rowi < (sj + 1) * tdec)
            yoff8 = jnp.where(in_seq, yo, yoff8)
        yoff_groups.append(yoff8)
    yoff = jnp.concatenate(yoff_groups, axis=0) if len(yoff_groups) > 1 else yoff_groups[0]
    y = (ypart_ref[...] + yoff * ea_ref[...]) * _silu(z_ref[...])
    y_ref[...] = _rms(y, nw_ref[...])


def _ssd_sample_state(cd, zx, ypart, ea, xdt, bmb, cmb, state, nw, *, tdec, gseq):
    t = zx.shape[0]
    nseq = t // tdec
    q = SSD_CHUNK
    rows = gseq * tdec
    spb = q // rows
    kern = functools.partial(_ssd_sample_state_kernel, tdec=tdec, gseq=gseq)
    return pl.pallas_call(
        kern,
        grid=(nseq // gseq,),
        in_specs=[
            pl.BlockSpec(memory_space=pltpu.SMEM),
            pl.BlockSpec((rows, D_MODEL), lambda i: (i, 0)),
            pl.BlockSpec((rows, D_MODEL), lambda i: (i, 0)),
            pl.BlockSpec((rows, D_MODEL), lambda i: (i, 0)),
            pl.BlockSpec((D_MODEL, q), lambda i: (0, i // spb)),
            pl.BlockSpec((q, 512), lambda i: (i // spb, 0)),
            pl.BlockSpec((rows, 512), lambda i: (i, 0)),
            pl.BlockSpec((gseq, D_MODEL, SSD_D_STATE), lambda i: (i, 0, 0)),
            pl.BlockSpec((1, D_MODEL), lambda i: (0, 0)),
        ],
        out_specs=[
            pl.BlockSpec((rows, D_MODEL), lambda i: (i, 0)),
            pl.BlockSpec((gseq, D_MODEL, SSD_D_STATE), lambda i: (i, 0, 0)),
        ],
        out_shape=[
            jax.ShapeDtypeStruct((t, D_MODEL), F32),
            jax.ShapeDtypeStruct((nseq, D_MODEL, SSD_D_STATE), F32),
        ],
        compiler_params=_cparams(("parallel",)),
        name="ssd_sample_state",
    )(cd, zx, ypart, ea, xdt, bmb, cmb, state, nw)


def _attn_prompt_kernel(q_ref, kc_ref, kp_ref, vc_ref, vp_ref, o_ref, lse_ref):
    n = pl.program_id(2)
    nk = 128
    q = q_ref[...]
    k = jnp.concatenate([kp_ref[...], kc_ref[...]], axis=0)
    v = jnp.concatenate([vp_ref[...], vc_ref[...]], axis=0)
    qi = lax.broadcasted_iota(jnp.int32, (nk, 2 * nk), 0)
    kj = lax.broadcasted_iota(jnp.int32, (nk, 2 * nk), 1)
    valid = (kj >= qi) & (kj <= qi + nk) & ((kj >= nk) | (n > 0))
    lane = lax.broadcasted_iota(jnp.int32, (1, LANES), 1)
    lse_full = jnp.zeros((nk, LANES), F32)
    for p in range(ATT_HEADS // 2):
        lanes = slice(p * LANES, (p + 1) * LANES)
        qp, kp, vp = q[:, lanes], k[:, lanes], v[:, lanes]
        o_pair = jnp.zeros((nk, LANES), F32)
        for hh in range(2):
            hm = (lane // ATT_HEAD_DIM) == hh
            qm = jnp.where(hm, qp, jnp.zeros((), qp.dtype))
            s = jnp.where(valid, _dot_nt(qm, kp), NEG_INF)
            m = jnp.max(s, axis=-1, keepdims=True)
            e = jnp.exp(s - m)
            den = jnp.sum(e, axis=-1, keepdims=True)
            o = _dot(e.astype(BF16), vp) / den
            o_pair = jnp.where(hm, o, o_pair)
            lse_full = jnp.where((lane // 16) == 2 * p + hh, m + jnp.log(den), lse_full)
        o_ref[:, lanes] = o_pair.astype(o_ref.dtype)
    lse_ref[...] = lse_full


def _attn_prompt(qkv, *, d):
    nb, _, length, _ = qkv.shape
    nk = 128
    nblk = length // nk
    cur = lambda which: pl.BlockSpec((None, None, nk, COL_TILE), lambda b, r, n: (b, r, n, which))
    prev = lambda which: pl.BlockSpec((None, None, nk, COL_TILE), lambda b, r, n: (b, r, jnp.maximum(n - 1, 0), which))
    return pl.pallas_call(
        _attn_prompt_kernel,
        grid=(nb, d, nblk),
        in_specs=[cur(0), cur(1), prev(1), cur(2), prev(2)],
        out_specs=[
            pl.BlockSpec((None, None, nk, ATT_WIDTH), lambda b, r, n: (b, r, n, 0)),
            pl.BlockSpec((None, None, nk, LANES), lambda b, r, n: (b, r, n, 0)),
        ],
        out_shape=[
            jax.ShapeDtypeStruct((nb, d, length, ATT_WIDTH), BF16),
            jax.ShapeDtypeStruct((nb, d, length, LANES), F32),
        ],
        compiler_params=_cparams(("parallel", "parallel", "arbitrary")),
        name=f"attn_prompt_d{d}",
    )(qkv, qkv, qkv, qkv, qkv)


def _attn_sample_kernel(q0_ref, q1_ref, q2_ref, qm_ref, c0_ref, c1_ref, c2_ref, cm_ref, oa_ref, om_ref, *, tdec):
    nh = ATT_HEADS
    rows8 = 8
    spg = rows8 // tdec
    r64 = nh * rows8
    rowi = lax.broadcasted_iota(jnp.int32, (r64, 1), 0)
    head_of_row = rowi // rows8
    r8 = rowi % rows8
    seq_q = r8 // tdec
    t_q = r8 % tdec
    lane512 = lax.broadcasted_iota(jnp.int32, (1, ATT_WIDTH), 1)
    headmask = (lane512 // ATT_HEAD_DIM) == head_of_row
    col8 = lax.broadcasted_iota(jnp.int32, (1, rows8), 1)
    seq_k = col8 // tdec
    t_k = col8 % tdec
    keyrow = lax.broadcasted_iota(jnp.int32, (1, 128), 1)

    per_group = []
    for g, (q_ref, c_ref) in enumerate(((q0_ref, c0_ref), (q1_ref, c1_ref), (q2_ref, c2_ref))):
        qkv = q_ref[...]
        qv = qkv[:, :ATT_WIDTH]
        kn = qkv[:, ATT_WIDTH:2 * ATT_WIDTH]
        vn = qkv[:, 2 * ATT_WIDTH:]
        q64 = jnp.where(headmask, jnp.concatenate([qv] * nh, axis=0), 0.0)
        q64b = q64.astype(BF16)
        s_new = _dot_nt(q64, kn)
        if g == 0:
            valid_new = (seq_k == seq_q) & (t_k <= t_q)
        else:
            valid_new = (seq_k == seq_q) & (t_k == t_q)
        s_new = jnp.where(valid_new, s_new, NEG_INF)
        m_new = jnp.max(s_new, axis=-1, keepdims=True)
        res = []
        for sj in range(spg):
            if g == 0:
                kc = c_ref[sj, :, 0:ATT_WIDTH].astype(BF16)
                s_c = jnp.where(keyrow >= t_q, _dot_nt(q64b, kc), NEG_INF)
            else:
                s_c = jnp.zeros((r64, 128), F32)
                for r in range(tdec):
                    kc = c_ref[sj, :, r * 2 * ATT_WIDTH:r * 2 * ATT_WIDTH + ATT_WIDTH].astype(BF16)
                    s_c = jnp.where(t_q == r, _dot_nt(q64b, kc), s_c)
            m = jnp.maximum(jnp.max(s_c, axis=-1, keepdims=True), m_new)
            e_c = jnp.exp(s_c - m)
            e_n = jnp.exp(s_new - m)
            den = jnp.sum(e_c, axis=-1, keepdims=True) + jnp.sum(e_n, axis=-1, keepdims=True)
            o = _dot(e_n, vn)
            if g == 0:
                vc = c_ref[sj, :, ATT_WIDTH:2 * ATT_WIDTH].astype(BF16)
                o = o + _dot(e_c.astype(BF16), vc)
            else:
                for r in range(tdec):
                    vc = c_ref[sj, :, r * 2 * ATT_WIDTH + ATT_WIDTH:(r + 1) * 2 * ATT_WIDTH].astype(BF16)
                    o = o + _dot(jnp.where(t_q == r, e_c, 0.0).astype(BF16), vc)
            res.append((o / den, m + jnp.log(den)))
        per_group.append(res)

    out = jnp.zeros((rows8, ATT_WIDTH), F32)
    row8 = lax.broadcasted_iota(jnp.int32, (rows8, 1), 0)
    for sj in range(spg):
        lses = [per_group[g][sj][1] for g in range(3)]
        mx = jnp.maximum(jnp.maximum(lses[0], lses[1]), lses[2])
        ws = [jnp.exp(l - mx) for l in lses]
        wsum = ws[0] + ws[1] + ws[2]
        om = (ws[0] * per_group[0][sj][0] + ws[1] * per_group[1][sj][0] + ws[2] * per_group[2][sj][0]) / wsum
        om = jnp.where(headmask, om, 0.0)
        o8 = om[0:rows8]
        for h in range(1, nh):
            o8 = o8 + om[h * rows8:(h + 1) * rows8]
        out = jnp.where((row8 // tdec) == sj, o8, out)
    oa_ref[...] = out

    r32 = MEM_HEADS * rows8
    rowm = lax.broadcasted_iota(jnp.int32, (r32, 1), 0)
    hmask = (lane512 // MEM_HEAD_DIM) == (rowm // rows8)
    qm = qm_ref[...] * (MEM_HEAD_DIM ** -0.5)
    q32 = jnp.where(hmask, jnp.concatenate([qm] * MEM_HEADS, axis=0), 0.0).astype(BF16)
    outm = jnp.zeros((rows8, MEM_WIDTH), F32)
    for sj in range(spg):
        km = cm_ref[sj, :, 0:MEM_WIDTH].astype(BF16)
        vm = cm_ref[sj, :, MEM_WIDTH:2 * MEM_WIDTH].astype(BF16)
        s = _dot_nt(q32, km)
        m = jnp.max(s, axis=-1, keepdims=True)
        e = jnp.exp(s - m)
        den = jnp.sum(e, axis=-1, keepdims=True)
        o = jnp.where(hmask, _dot(e.astype(BF16), vm) / den, 0.0)
        o8 = o[0:rows8]
        for h in range(1, MEM_HEADS):
            o8 = o8 + o[h * rows8:(h + 1) * rows8]
        outm = jnp.where((row8 // tdec) == sj, o8, outm)
    om_ref[...] = outm


def _attn_sample(q0, q1, q2, rest, c0, c1, c2, cmem, *, tdec):
    t = q0.shape[0]
    spg = 8 // tdec
    row_spec = pl.BlockSpec((8, 3 * COL_TILE), lambda i: (i, 0))
    kern = functools.partial(_attn_sample_kernel, tdec=tdec)
    return pl.pallas_call(
        kern,
        grid=(t // 8,),
        in_specs=[
            row_spec, row_spec, row_spec,
            pl.BlockSpec((8, COL_TILE), lambda i: (i, 1)),
            pl.BlockSpec((spg, ATT_WINDOWS[0], 2 * ATT_WIDTH), lambda i: (i, 0, 0)),
            pl.BlockSpec((spg, 128, tdec * 2 * ATT_WIDTH), lambda i: (i, 0, 0)),
            pl.BlockSpec((spg, 128, tdec * 2 * ATT_WIDTH), lambda i: (i, 0, 0)),
            pl.BlockSpec((spg, MEM_LEN, 2 * MEM_WIDTH), lambda i: (i, 0, 0)),
        ],
        out_specs=[
            pl.BlockSpec((8, ATT_WIDTH), lambda i: (i, 0)),
            pl.BlockSpec((8, MEM_WIDTH), lambda i: (i, 0)),
        ],
        out_shape=[
            jax.ShapeDtypeStruct((t, ATT_WIDTH), F32),
            jax.ShapeDtypeStruct((t, MEM_WIDTH), F32),
        ],
        compiler_params=_cparams(("parallel",)),
        name="attn_sample",
    )(q0, q1, q2, rest, c0, c1, c2, cmem)


def _final_tail(x, y_ssd, y_att, y_mem, gate_raw, wssd_ref, watt_ref, wmem_ref, wout_ref, nf_ref, y_ref):
    gates = _sigmoid(gate_raw)
    merged = (gates[:, 0:D_MODEL] * _dot(y_ssd.astype(BF16), wssd_ref[...])
              + gates[:, D_MODEL:2 * D_MODEL] * _dot(y_att.astype(BF16), watt_ref[...])
              + gates[:, 2 * D_MODEL:] * _dot(y_mem.astype(BF16), wmem_ref[...]))
    y_ref[...] = _rms(x + _dot(merged.astype(BF16), wout_ref[...]), nf_ref[...])


def _final_prompt_kernel(x_ref, ys_ref, o0_ref, o1_ref, o2_ref, l0_ref, l1_ref, l2_ref, rest_ref, mkv_ref,
                         wssd_ref, watt_ref, wmem_ref, wout_ref, nf_ref, e16_ref,
                         y_ref, obuf, lbuf, *, tm, dils):
    nslab = ATT_WIDTH // LANES
    o_nat, l_nat = [], []
    for g, (o_ref, l_ref, d) in enumerate(zip((o0_ref, o1_ref, o2_ref), (l0_ref, l1_ref, l2_ref), dils)):
        if d == 1:
            o_nat.append(o_ref[0].astype(F32))
            l_nat.append(l_ref[0])
            continue
        for r in range(d):
            blk = o_ref[r].astype(F32)
            for c in range(nslab):
                obuf[g, c, pl.ds(r, tm // d, stride=d), :] = blk[:, c * LANES:(c + 1) * LANES]
            lbuf[g, pl.ds(r, tm // d, stride=d), :] = l_ref[r]
        o_nat.append(jnp.concatenate([obuf[g, c] for c in range(nslab)], axis=1))
        l_nat.append(lbuf[g])
    mx = jnp.maximum(jnp.maximum(l_nat[0], l_nat[1]), l_nat[2])
    ws = [jnp.exp(l - mx) for l in l_nat]
    wsum = ws[0] + ws[1] + ws[2]
    e16 = e16_ref[...]
    o_att = jnp.zeros((tm, ATT_WIDTH), F32)
    for g in range(3):
        o_att = o_att + _dot_exact_rhs01(ws[g] / wsum, e16) * o_nat[g]

    rest = rest_ref[...]
    z_att = rest[:, 0:512].astype(F32)
    q_mem = rest[:, 512:1024]
    z_mem = rest[:, 1024:1536].astype(F32)
    gate_raw = rest[:, 1536:].astype(F32)
    y_att = o_att * _silu(z_att)

    outs = []
    for h in range(MEM_HEADS):
        lanes = slice(h * MEM_HEAD_DIM, (h + 1) * MEM_HEAD_DIM)
        kh = mkv_ref[:, h * MEM_HEAD_DIM:(h + 1) * MEM_HEAD_DIM]
        vh = mkv_ref[:, MEM_WIDTH + h * MEM_HEAD_DIM:MEM_WIDTH + (h + 1) * MEM_HEAD_DIM]
        s = _dot_nt(q_mem[:, lanes], kh) * (MEM_HEAD_DIM ** -0.5)
        m = jnp.max(s, axis=-1, keepdims=True)
        e = jnp.exp(s - m)
        den = jnp.sum(e, axis=-1, keepdims=True)
        outs.append(_dot(e.astype(BF16), vh) / den)
    y_mem = jnp.concatenate(outs, axis=1) * _silu(z_mem)

    _final_tail(x_ref[...], ys_ref[...], y_att, y_mem, gate_raw,
                wssd_ref, watt_ref, wmem_ref, wout_ref, nf_ref, y_ref)


def _final_prompt(x2d, y_ssd, o_g, l_g, rest, mkv_bf, wssd, watt, wmem, wout, nf, e16, *, seq, tm, dils):
    t = x2d.shape[0]
    tps = seq // tm

    def res_spec(d, width):
        return pl.BlockSpec((None, d, tm // d, width), lambda i: (i // tps, 0, i % tps, 0))

    full = lambda a: pl.BlockSpec(a.shape, lambda i: (0,) * a.ndim)
    kern = functools.partial(_final_prompt_kernel, tm=tm, dils=dils)
    return pl.pallas_call(
        kern,
        grid=(t // tm,),
        in_specs=[
            pl.BlockSpec((tm, D_MODEL), lambda i: (i, 0)),
            pl.BlockSpec((tm, D_MODEL), lambda i: (i, 0)),
            res_spec(dils[0], ATT_WIDTH), res_spec(dils[1], ATT_WIDTH), res_spec(dils[2], ATT_WIDTH),
            res_spec(dils[0], LANES), res_spec(dils[1], LANES), res_spec(dils[2], LANES),
            pl.BlockSpec((tm, rest.shape[1]), lambda i: (i, 0)),
            pl.BlockSpec((None, MEM_LEN, 2 * MEM_WIDTH), lambda i: (i // tps, 0, 0)),
            full(wssd), full(watt), full(wmem), full(wout), full(nf), full(e16),
        ],
        out_specs=pl.BlockSpec((tm, D_MODEL), lambda i: (i, 0)),
        out_shape=jax.ShapeDtypeStruct((t, D_MODEL), F32),
        scratch_shapes=[
            pltpu.VMEM((3, ATT_WIDTH // LANES, tm, LANES), F32),
            pltpu.VMEM((3, tm, LANES), F32),
        ],
        compiler_params=_cparams(("parallel",)),
        name="final_prompt",
    )(x2d, y_ssd, *o_g, *l_g, rest, mkv_bf, wssd, watt, wmem, wout, nf, e16)


def _final_sample_kernel(x_ref, ys_ref, oa_ref, om_ref, rest_ref,
                         wssd_ref, watt_ref, wmem_ref, wout_ref, nf_ref, y_ref):
    rest = rest_ref[...]
    y_att = oa_ref[...] * _silu(rest[:, 0:512])
    y_mem = om_ref[...] * _silu(rest[:, 1024:1536])
    _final_tail(x_ref[...], ys_ref[...], y_att, y_mem, rest[:, 1536:],
                wssd_ref, watt_ref, wmem_ref, wout_ref, nf_ref, y_ref)


def _final_sample(x2d, y_ssd, o_att, o_mem, rest, wssd, watt, wmem, wout, nf, *, tm):
    t = x2d.shape[0]
    full = lambda a: pl.BlockSpec(a.shape, lambda i: (0,) * a.ndim)
    rows = lambda w: pl.BlockSpec((tm, w), lambda i: (i, 0))
    return pl.pallas_call(
        _final_sample_kernel,
        grid=(t // tm,),
        in_specs=[rows(D_MODEL), rows(D_MODEL), rows(ATT_WIDTH), rows(MEM_WIDTH), rows(rest.shape[1]),
                  full(wssd), full(watt), full(wmem), full(wout), full(nf)],
        out_specs=rows(D_MODEL),
        out_shape=jax.ShapeDtypeStruct((t, D_MODEL), F32),
        compiler_params=_cparams(("parallel",)),
        name="final_sample",
    )(x2d, y_ssd, o_att, o_mem, rest, wssd, watt, wmem, wout, nf)


def _from_residues(a):
    b, d, length, c = a.shape
    return a.transpose(0, 2, 1, 3).reshape(b, length * d, c)


def kernel(x_prompt, x_sample, mem_prompt, cache_win128_kv, cache_win512_kv, cache_win2048_kv, cache_mem_kv,
           state_conv, state_ssm, norm_in_w, w_in, conv_w, conv_b, dt_bias, a_log, d_skip, ssd_norm_w,
           mem_norm_w, w_mem_kv, w_br_ssd, w_br_att, w_br_mem, w_out, norm_f_w):
    bp, seq, _ = x_prompt.shape
    bs, tdec, _ = x_sample.shape
    dils = ATT_DILATIONS

    row = lambda v: v.reshape(1, -1).astype(F32)
    dt_col0 = D_MODEL + SSD_CONV_DIM
    w_main = jnp.concatenate([w_in[:, :dt_col0], w_in[:, dt_col0 + SSD_N_HEADS:]], axis=1).astype(BF16)
    w_dt = jnp.pad(w_in[:, dt_col0:dt_col0 + SSD_N_HEADS], ((0, 0), (0, LANES - SSD_N_HEADS))).astype(BF16)
    pad_heads = lambda v: jnp.pad(v.astype(F32), (0, LANES - SSD_N_HEADS)).reshape(1, LANES)
    dtb = pad_heads(dt_bias)
    a_row = jnp.pad(-jnp.exp(a_log.astype(F32)), (0, LANES - SSD_N_HEADS)).reshape(1, LANES)
    dtb_col, a_col = dtb.reshape(LANES, 1), a_row.reshape(LANES, 1)
    dsk = jnp.repeat(d_skip.astype(F32), SSD_HEAD_DIM).reshape(1, D_MODEL)
    cw = conv_w.astype(F32)
    cb = row(conv_b)
    nin, nssd, nmem, nf = row(norm_in_w), row(ssd_norm_w), row(mem_norm_w), row(norm_f_w)
    wssd, watt, wmem, wout = (w.astype(BF16) for w in (w_br_ssd, w_br_att, w_br_mem, w_out))
    lane = jnp.arange(LANES)
    e_ssd = (lane[:, None] == (jnp.arange(D_MODEL)[None, :] // SSD_HEAD_DIM)).astype(BF16)
    e_att = (lane[:, None] == 16 * (jnp.arange(ATT_WIDTH)[None, :] // ATT_HEAD_DIM)).astype(BF16)

    xp2 = x_prompt.reshape(bp * seq, D_MODEL)
    mkv_f32, mkv_bf = _mem_kv(mem_prompt.reshape(bp * MEM_LEN, D_MODEL), nmem, w_mem_kv.astype(BF16))
    tm_p = 1024
    tabs_p = _rope_tables(jnp.arange(seq))
    zx, q0, q1, q2, rest, dtr = _in_proj(xp2, nin, w_main, w_dt, tabs_p, seq=seq, tm=tm_p, dils=dils, out_dtype=BF16)
    y_ssd, p_ssm = _ssd_prompt(zx, dtr, cw, cb, dtb, a_row, dtb_col, a_col, dsk, nssd, nbatch=bp, seq=seq)
    o_g, l_g = [], []
    for qkv, d in zip((q0, q1, q2), dils):
        o, l = _attn_prompt(qkv, d=d)
        o_g.append(o)
        l_g.append(l)
    y_prompt = _final_prompt(xp2, y_ssd, o_g, l_g, rest, mkv_bf.reshape(bp, MEM_LEN, 2 * MEM_WIDTH),
                             wssd, watt, wmem, wout, nf, e_att, seq=seq, tm=256, dils=dils)
    y_prompt = y_prompt.reshape(bp, seq, D_MODEL)

    p_win = []
    for qkv, d, w in zip((q0, q1, q2), dils, ATT_WINDOWS):
        wl = min(w, seq)
        kv = _from_residues(qkv[:, :, (seq - wl) // d:, COL_TILE:])
        p_win.append(kv.astype(F32).reshape(bp, wl, 2, ATT_HEADS, ATT_HEAD_DIM))
    p_mem_kv = mkv_f32.reshape(bp, MEM_LEN, 2, MEM_HEADS, MEM_HEAD_DIM)
    p_conv = zx.reshape(bp, seq, -1)[:, seq - (SSD_CONV - 1):, D_MODEL:].astype(F32)
    p_ssm = p_ssm.reshape(bp, SSD_N_HEADS, SSD_HEAD_DIM, SSD_D_STATE)

    ts = bs * tdec
    pos_s = PAST_LEN + jnp.tile(jnp.arange(tdec), bs)
    xs2 = x_sample.reshape(ts, D_MODEL)
    tabs_s = _rope_tables(pos_s)
    zx_s, q0_s, q1_s, q2_s, rest_s, dt_s = _in_proj(
        xs2, nin, w_main, w_dt, tabs_s, seq=ts, tm=min(ts, 512), dils=(1, 1, 1), out_dtype=F32)
    q_s = [a.reshape(ts, 3 * COL_TILE) for a in (q0_s, q1_s, q2_s)]
    hist = jnp.pad(state_conv.astype(F32), ((0, 0), (tdec - (SSD_CONV - 1), 0), (0, 0))).reshape(ts, SSD_CONV_DIM)
    ypart, ea, xdt, bmb, cmb, cd = _ssd_sample_rows(
        zx_s, dt_s, hist, cw, cb, dtb, a_row, dtb_col, a_col, dsk, e_ssd, tdec=tdec)
    cd_seq = cd.reshape(bs, tdec, LANES)[:, 0, :SSD_N_HEADS]
    y_ssd_s, s_ssm = _ssd_sample_state(cd_seq, zx_s, ypart, ea, xdt, bmb, cmb,
                                       state_ssm.reshape(bs, D_MODEL, SSD_D_STATE), nssd, tdec=tdec, gseq=8)
    c0 = cache_win128_kv.reshape(bs, ATT_WINDOWS[0], 2 * ATT_WIDTH)
    c1 = cache_win512_kv.reshape(bs, 128, dils[1] * 2 * ATT_WIDTH)
    c2 = cache_win2048_kv.reshape(bs, 128, dils[2] * 2 * ATT_WIDTH)
    cmem = cache_mem_kv.reshape(bs, MEM_LEN, 2 * MEM_WIDTH)
    o_att_s, o_mem_s = _attn_sample(*q_s, rest_s, c0, c1, c2, cmem, tdec=tdec)
    y_sample = _final_sample(xs2, y_ssd_s, o_att_s, o_mem_s, rest_s, wssd, watt, wmem, wout, nf, tm=min(ts, 256))
    y_sample = y_sample.reshape(bs, tdec, D_MODEL)

    s_win = [a[:, COL_TILE:].reshape(bs, tdec, 2, ATT_HEADS, ATT_HEAD_DIM) for a in q_s]
    s_conv = zx_s.reshape(bs, tdec, -1)[:, tdec - (SSD_CONV - 1):, D_MODEL:]
    s_ssm = s_ssm.reshape(bs, SSD_N_HEADS, SSD_HEAD_DIM, SSD_D_STATE)

    return (y_prompt, y_sample, p_win[0], p_win[1], p_win[2], p_mem_kv, p_conv, p_ssm,
            s_win[0], s_win[1], s_win[2], s_conv, s_ssm)
```

```python
import functools
import math

import jax
import jax.numpy as jnp
from jax import lax
from jax.experimental import pallas as pl
from jax.experimental.pallas import tpu as pltpu

F32 = jnp.float32
BF16 = jnp.bfloat16

D_MODEL = 1024
NORM_EPS = 1e-6
SSD_HEAD_DIM = 64
SSD_N_HEADS = 16
SSD_N_GROUPS = 4
SSD_D_STATE = 128
SSD_CONV = 4
SSD_CHUNK = 128
SSD_CONV_DIM = 2048
ATT_WINDOWS = (128, 512, 2048)
ATT_DILATIONS = (1, 4, 16)
ATT_HEADS = 8
ATT_HEAD_DIM = 64
ATT_WIDTH = 512
ROPE_DIM = 16
ROPE_THETA = 500000.0
MEM_LEN = 256
MEM_HEADS = 4
MEM_HEAD_DIM = 128
MEM_WIDTH = 512
NEG_INF = -1e30
PAST_LEN = 8192

LANES = 128
COL_TILE = 512
N_COL_TILES = 24
QKV_TILE0 = 6
REST_TILE0 = 15
VMEM_LIMIT = 56 * 1024 * 1024


def _cparams(sem):
    return pltpu.CompilerParams(dimension_semantics=sem, vmem_limit_bytes=VMEM_LIMIT)


def _dot(a, b):
    return jnp.dot(a, b, preferred_element_type=F32)


def _dot_nt(a, b):
    return lax.dot_general(a, b, (((1,), (1,)), ((), ())), preferred_element_type=F32)


def _rms(x, w):
    return x * lax.rsqrt(jnp.mean(x * x, axis=-1, keepdims=True) + NORM_EPS) * w


def _sigmoid(x):
    return 0.5 * jnp.tanh(0.5 * x) + 0.5


def _silu(x):
    return x * _sigmoid(x)


def _softplus(x):
    return jnp.maximum(x, 0.0) + jnp.log(1.0 + jnp.exp(-jnp.abs(x)))


def _split3(x):
    hi = x.astype(BF16)
    r1 = x - hi.astype(F32)
    mid = r1.astype(BF16)
    lo = (r1 - mid.astype(F32)).astype(BF16)
    return hi, mid, lo


def _dot_exact_lhs01(m01, x):
    hi, mid, lo = _split3(x)
    return _dot(m01, hi) + _dot(m01, mid) + _dot(m01, lo)


def _dot_exact_rhs01(x, m01):
    hi, mid, lo = _split3(x)
    return _dot(hi, m01) + _dot(mid, m01) + _dot(lo, m01)


def _mem_kv_kernel(x_ref, nw_ref, w_ref, o_ref, ob_ref):
    h = _rms(x_ref[...], nw_ref[...]).astype(BF16)
    acc = _dot(h, w_ref[...])
    o_ref[...] = acc
    ob_ref[...] = acc.astype(BF16)


def _mem_kv(mem2d, nw, w_bf):
    m = mem2d.shape[0]
    tm = 256
    return pl.pallas_call(
        _mem_kv_kernel,
        grid=(m // tm,),
        in_specs=[
            pl.BlockSpec((tm, D_MODEL), lambda i: (i, 0)),
            pl.BlockSpec((1, D_MODEL), lambda i: (0, 0)),
            pl.BlockSpec((D_MODEL, 2 * MEM_WIDTH), lambda i: (0, 0)),
        ],
        out_specs=[
            pl.BlockSpec((tm, 2 * MEM_WIDTH), lambda i: (i, 0)),
            pl.BlockSpec((tm, 2 * MEM_WIDTH), lambda i: (i, 0)),
        ],
        out_shape=[
            jax.ShapeDtypeStruct((m, 2 * MEM_WIDTH), F32),
            jax.ShapeDtypeStruct((m, 2 * MEM_WIDTH), BF16),
        ],
        compiler_params=_cparams(("parallel",)),
        name="mem_kv",
    )(mem2d, nw, w_bf)


def _rope_tables(pos):
    half = ROPE_DIM // 2
    inv = jnp.power(ROPE_THETA, -jnp.arange(half, dtype=F32) * 2.0 / ROPE_DIM)
    ang = pos.astype(F32)[:, None] * inv[None, :]
    cos, sin = jnp.cos(ang), jnp.sin(ang)
    m = jnp.arange(LANES) % ATT_HEAD_DIM
    idx = m % half
    c = jnp.where(m[None, :] < ROPE_DIM, cos[:, idx], 1.0)
    s1 = jnp.where(m[None, :] < half, -sin[:, idx], 0.0)
    s2 = jnp.where((m[None, :] >= half) & (m[None, :] < ROPE_DIM), sin[:, idx], 0.0)
    return c.astype(F32), s1.astype(F32), s2.astype(F32)


def _in_proj_kernel(x_ref, nw_ref, w_ref, wdt_ref, c_ref, s1_ref, s2_ref,
                    zx_ref, q0_ref, q1_ref, q2_ref, rest_ref, dt_ref,
                    h_ref, acc_ref, tmp_ref, *, tm, dils):
    nslab = COL_TILE // LANES
    h_ref[...] = _rms(x_ref[...], nw_ref[...]).astype(BF16)
    dt_ref[...] = _dot(h_ref[...], wdt_ref[...])
    q_refs = (q0_ref, q1_ref, q2_ref)
    strided = 0
    for j in range(N_COL_TILES):
        acc = _dot(h_ref[...], w_ref[:, j * COL_TILE:(j + 1) * COL_TILE])
        if j < QKV_TILE0:
            zx_ref[:, j * COL_TILE:(j + 1) * COL_TILE] = acc.astype(zx_ref.dtype)
            continue
        if j >= REST_TILE0:
            rest_ref[:, (j - REST_TILE0) * COL_TILE:(j - REST_TILE0 + 1) * COL_TILE] = acc.astype(rest_ref.dtype)
            continue
        g, kind = divmod(j - QKV_TILE0, 3)
        out_ref, d = q_refs[g], dils[g]
        for c in range(nslab):
            a = acc[:, c * LANES:(c + 1) * LANES]
            if kind < 2:
                a = (a * c_ref[...] + pltpu.roll(a, LANES - ROPE_DIM // 2, 1) * s1_ref[...]
                     + pltpu.roll(a, ROPE_DIM // 2, 1) * s2_ref[...])
            if kind == 0:
                a = a * (ATT_HEAD_DIM ** -0.5)
            lanes = slice(kind * COL_TILE + c * LANES, kind * COL_TILE + (c + 1) * LANES)
            if d == 1:
                out_ref[0, :, lanes] = a.astype(out_ref.dtype)
            elif d == 4:
                acc_ref[strided, c] = a
                for r in range(4):
                    out_ref[r, :, lanes] = acc_ref[strided, c, pl.ds(r, tm // 4, stride=4), :].astype(out_ref.dtype)
            else:
                acc_ref[strided, c] = a
                q4 = tm // 4
                for r1 in range(4):
                    tmp_ref[kind, c, r1 * q4:(r1 + 1) * q4, :] = acc_ref[strided, c, pl.ds(r1, q4, stride=4), :]
                for r1 in range(4):
                    for r2 in range(4):
                        out_ref[r1 + 4 * r2, :, lanes] = tmp_ref[
                            kind, c, pl.ds(r1 * q4 + r2, tm // 16, stride=4), :].astype(out_ref.dtype)
        if d > 1:
            strided += 1


def _in_proj(x2d, nw, w_main, w_dt, tabs, *, seq, tm, dils, out_dtype):
    t = x2d.shape[0]
    nb = t // seq
    tiles_per_seq = seq // tm
    tab_blocks = tabs[0].shape[0] // tm
    n_strided = 3 * sum(1 for d in dils if d > 1)
    nslab = COL_TILE // LANES

    def qkv_spec(d):
        return pl.BlockSpec((None, d, tm // d, 3 * COL_TILE),
                            lambda i: (i // tiles_per_seq, 0, i % tiles_per_seq, 0))

    tab_spec = pl.BlockSpec((tm, LANES), lambda i: (i % tab_blocks, 0))
    resident = lambda shape: pl.BlockSpec(shape, lambda i: (0, 0), pipeline_mode=pl.Buffered(1))
    kern = functools.partial(_in_proj_kernel, tm=tm, dils=dils)
    n_zx, n_rest = QKV_TILE0 * COL_TILE, (N_COL_TILES - REST_TILE0) * COL_TILE
    return pl.pallas_call(
        kern,
        grid=(t // tm,),
        in_specs=[
            pl.BlockSpec((tm, D_MODEL), lambda i: (i, 0)),
            resident((1, D_MODEL)),
            resident(w_main.shape),
            resident((D_MODEL, LANES)),
            tab_spec, tab_spec, tab_spec,
        ],
        out_specs=[
            pl.BlockSpec((tm, n_zx), lambda i: (i, 0)),
            qkv_spec(dils[0]), qkv_spec(dils[1]), qkv_spec(dils[2]),
            pl.BlockSpec((tm, n_rest), lambda i: (i, 0)),
            pl.BlockSpec((tm, LANES), lambda i: (i, 0)),
        ],
        out_shape=[
            jax.ShapeDtypeStruct((t, n_zx), out_dtype),
            jax.ShapeDtypeStruct((nb, dils[0], seq // dils[0], 3 * COL_TILE), out_dtype),
            jax.ShapeDtypeStruct((nb, dils[1], seq // dils[1], 3 * COL_TILE), out_dtype),
            jax.ShapeDtypeStruct((nb, dils[2], seq // dils[2], 3 * COL_TILE), out_dtype),
            jax.ShapeDtypeStruct((t, n_rest), out_dtype),
            jax.ShapeDtypeStruct((t, LANES), F32),
        ],
        scratch_shapes=[
            pltpu.VMEM((tm, D_MODEL), BF16),
            pltpu.VMEM((max(n_strided, 1), nslab, tm, LANES), F32),
            pltpu.VMEM((3, nslab, tm, LANES), F32),
        ],
        compiler_params=_cparams(("parallel",)),
        name="in_proj",
    )(x2d, nw, w_main, w_dt, *tabs)


def _ssd_small(dt_raw, dtb_row, a_row, dtb_col, a_col, same01, same01_t, tot01, tot01_t):
    dt = _softplus(dt_raw + dtb_row)
    dta = dt * a_row
    acs = _dot_exact_lhs01(same01, dta)
    tot = _dot_exact_lhs01(tot01, dta)
    dtd = dt * jnp.exp(tot - acs)
    dt_raw_t = dt_raw.T
    dt_t = _softplus(dt_raw_t + dtb_col)
    dta_t = dt_t * a_col
    acs_t = _dot_exact_rhs01(dta_t, same01_t)
    tot_t = _dot_exact_rhs01(dta_t, tot01_t)
    dtd_t = dt_t * jnp.exp(tot_t - acs_t)
    return dt, acs, tot, dtd, dt_t, acs_t, dtd_t


def _ssd_pairs(x, bm, cm, acs, acs_t, dt_t, dtd_t, mask, st_ref, y_ref_write, st_scale_row):
    lane = lax.broadcasted_iota(jnp.int32, (1, LANES), 1)
    half = [lane < ATT_HEAD_DIM, lane >= ATT_HEAD_DIM]
    bm_t = [bm[:, g * SSD_D_STATE:(g + 1) * SSD_D_STATE].T for g in range(SSD_N_GROUPS)]
    cms = [cm[:, g * SSD_D_STATE:(g + 1) * SSD_D_STATE] for g in range(SSD_N_GROUPS)]
    cb = [_dot(cms[g].astype(BF16), bm_t[g].astype(BF16)) for g in range(SSD_N_GROUPS)]
    for p in range(SSD_N_HEADS // 2):
        g = p // 2
        xp = x[:, p * LANES:(p + 1) * LANES]
        if st_ref is not None:
            stp = st_ref[:, p * LANES:(p + 1) * LANES]
        y_pair = jnp.zeros((SSD_CHUNK, LANES), F32)
        st_add = jnp.zeros((SSD_D_STATE, LANES), F32)
        for hh in range(2):
            h = 2 * p + hh
            col = acs[:, h:h + 1]
            row = acs_t[h:h + 1, :]
            lmat = jnp.exp(jnp.where(mask, col - row, NEG_INF))
            m = cb[g] * lmat * dt_t[h:h + 1, :]
            xm = jnp.where(half[hh], xp, 0.0).astype(BF16)
            if st_ref is not None:
                cx = cms[g] * jnp.exp(col)
                lhs = jnp.concatenate([m, cx], axis=1).astype(BF16)
                rhs = jnp.concatenate([xm, jnp.where(half[hh], stp, 0.0).astype(BF16)], axis=0)
                y_pair = y_pair + _dot(lhs, rhs)
                bt_h = (bm_t[g] * dtd_t[h:h + 1, :]).astype(BF16)
                st_add = st_add + _dot(bt_h, xm)
            else:
                y_pair = y_pair + _dot(m.astype(BF16), xm)
        y_ref_write(p, y_pair)
        if st_ref is not None:
            scale = jnp.where(half[0], st_scale_row[:, 2 * p:2 * p + 1], st_scale_row[:, 2 * p + 1:2 * p + 2])
            st_ref[:, p * LANES:(p + 1) * LANES] = stp * scale + st_add


def _ssd_prompt_kernel(z_ref, xs_ref, bc_ref, dt_ref, cw_ref, cb_ref, dtb_ref, a_ref, dtbc_ref, ac_ref,
                       dsk_ref, nw_ref, y_ref, ssm_ref, xbuf, st_ref, ybuf):
    c = pl.program_id(1)
    q = SSD_CHUNK
    hist = 8

    @pl.when(c == 0)
    def _():
        xbuf[0:hist, :] = jnp.zeros((hist, SSD_CONV_DIM), F32)
        st_ref[...] = jnp.zeros(st_ref.shape, F32)

    xbuf[hist:hist + q, 0:D_MODEL] = xs_ref[...].astype(F32)
    xbuf[hist:hist + q, D_MODEL:SSD_CONV_DIM] = bc_ref[...].astype(F32)
    acc = cb_ref[...] + cw_ref[SSD_CONV - 1:SSD_CONV, :] * xbuf[hist:hist + q, :]
    for s in range(1, SSD_CONV):
        acc = acc + cw_ref[SSD_CONV - 1 - s:SSD_CONV - s, :] * xbuf[hist - s:hist - s + q, :]
    xbuf[0:hist, :] = xbuf[q:q + hist, :]
    xc = _silu(acc)
    x = xc[:, :D_MODEL]
    bm = xc[:, D_MODEL:D_MODEL + 512]
    cm = xc[:, D_MODEL + 512:]

    ri = lax.broadcasted_iota(jnp.int32, (q, q), 0)
    ci = lax.broadcasted_iota(jnp.int32, (q, q), 1)
    tril = ri >= ci
    tril01 = jnp.where(tril, 1.0, 0.0).astype(BF16)
    triu01 = jnp.where(ri <= ci, 1.0, 0.0).astype(BF16)
    ones01 = jnp.ones((q, q), BF16)
    dt, acs, tot, dtd, dt_t, acs_t, dtd_t = _ssd_small(
        dt_ref[...], dtb_ref[...], a_ref[...], dtbc_ref[...], ac_ref[...], tril01, triu01, ones01, ones01)
    decay_row = jnp.exp(tot[0:1, :])

    def write_y(p, y_pair):
        ybuf[:, p * LANES:(p + 1) * LANES] = y_pair

    _ssd_pairs(x, bm, cm, acs, acs_t, dt_t, dtd_t, tril, st_ref, write_y, decay_row)

    y = (ybuf[...] + dsk_ref[...] * x) * _silu(z_ref[...].astype(F32))
    y_ref[...] = _rms(y, nw_ref[...]).astype(y_ref.dtype)

    @pl.when(c == pl.num_programs(1) - 1)
    def _():
        for p in range(D_MODEL // LANES):
            ssm_ref[p * LANES:(p + 1) * LANES, :] = st_ref[:, p * LANES:(p + 1) * LANES].T


def _ssd_prompt(zx, dt, cw, cb, dtb, a_row, dtb_col, a_col, dsk, nw, *, nbatch, seq):
    nc = seq // SSD_CHUNK
    q = SSD_CHUNK
    row = lambda b, c: b * nc + c
    vec = lambda n: pl.BlockSpec((1, n), lambda b, c: (0, 0))
    col = pl.BlockSpec((LANES, 1), lambda b, c: (0, 0))
    return pl.pallas_call(
        _ssd_prompt_kernel,
        grid=(nbatch, nc),
        in_specs=[
            pl.BlockSpec((q, D_MODEL), lambda b, c: (row(b, c), 0)),
            pl.BlockSpec((q, D_MODEL), lambda b, c: (row(b, c), 1)),
            pl.BlockSpec((q, D_MODEL), lambda b, c: (row(b, c), 2)),
            pl.BlockSpec((q, LANES), lambda b, c: (row(b, c), 0)),
            pl.BlockSpec((SSD_CONV, SSD_CONV_DIM), lambda b, c: (0, 0)),
            vec(SSD_CONV_DIM), vec(LANES), vec(LANES), col, col, vec(D_MODEL), vec(D_MODEL),
        ],
        out_specs=[
            pl.BlockSpec((q, D_MODEL), lambda b, c: (row(b, c), 0)),
            pl.BlockSpec((None, D_MODEL, SSD_D_STATE), lambda b, c: (b, 0, 0)),
        ],
        out_shape=[
            jax.ShapeDtypeStruct((nbatch * seq, D_MODEL), BF16),
            jax.ShapeDtypeStruct((nbatch, D_MODEL, SSD_D_STATE), F32),
        ],
        scratch_shapes=[
            pltpu.VMEM((q + 8, SSD_CONV_DIM), F32),
            pltpu.VMEM((SSD_D_STATE, D_MODEL), F32),
            pltpu.VMEM((q, D_MODEL), F32),
        ],
        compiler_params=_cparams(("parallel", "arbitrary")),
        name="ssd_prompt",
    )(zx, zx, zx, dt, cw, cb, dtb, a_row, dtb_col, a_col, dsk, nw)


def _expand_heads(v, e01):
    return _dot_exact_rhs01(v, e01)


def _ssd_sample_rows_kernel(xs_ref, bc_ref, dt_ref, hist_ref, cw_ref, cb_ref, dtb_ref, a_ref, dtbc_ref, ac_ref,
                            dsk_ref, e01_ref,
                            ypart_ref, ea_ref, xdt_ref, bm_ref, cm_ref, cd_ref, ybuf, *, tdec):
    q = SSD_CHUNK
    x_raw = jnp.concatenate([xs_ref[...], bc_ref[...]], axis=1)
    hist = hist_ref[...]
    rowi = lax.broadcasted_iota(jnp.int32, (q, 1), 0)
    tpos = rowi % tdec
    acc = cb_ref[...] + cw_ref[SSD_CONV - 1:SSD_CONV, :] * x_raw
    for s in range(1, SSD_CONV):
        prev = jnp.where(tpos >= s, pltpu.roll(x_raw, s, 0), pltpu.roll(hist, q - tdec + s, 0))
        acc = acc + cw_ref[SSD_CONV - 1 - s:SSD_CONV - s, :] * prev
    xc = _silu(acc)
    x = xc[:, :D_MODEL]
    bm = xc[:, D_MODEL:D_MODEL + 512]
    cm = xc[:, D_MODEL + 512:]

    ri = lax.broadcasted_iota(jnp.int32, (q, q), 0)
    ci = lax.broadcasted_iota(jnp.int32, (q, q), 1)
    same = (ri // tdec) == (ci // tdec)
    mask = same & (ri >= ci)
    low01 = jnp.where(mask, 1.0, 0.0).astype(BF16)
    up01 = jnp.where(same & (ri <= ci), 1.0, 0.0).astype(BF16)
    same01 = jnp.where(same, 1.0, 0.0).astype(BF16)
    dt, acs, tot, dtd, dt_t, acs_t, dtd_t = _ssd_small(
        dt_ref[...], dtb_ref[...], a_ref[...], dtbc_ref[...], ac_ref[...], low01, up01, same01, same01)

    def write_y(p, y_pair):
        ybuf[:, p * LANES:(p + 1) * LANES] = y_pair

    _ssd_pairs(x, bm, cm, acs, acs_t, dt_t, dtd_t, mask, None, write_y, None)

    e01 = e01_ref[...]
    ypart_ref[...] = ybuf[...] + dsk_ref[...] * x
    ea_ref[...] = _expand_heads(jnp.exp(acs), e01)
    xd = x * _expand_heads(dtd, e01)
    for p in range(D_MODEL // LANES):
        xdt_ref[p * LANES:(p + 1) * LANES, :] = xd[:, p * LANES:(p + 1) * LANES].T.astype(BF16)
    bm_ref[...] = bm.astype(BF16)
    cm_ref[...] = cm.astype(BF16)
    cd_ref[...] = jnp.exp(tot)


def _ssd_sample_rows(zx, dt, hist, cw, cb, dtb, a_row, dtb_col, a_col, dsk, e01, *, tdec):
    t = zx.shape[0]
    q = SSD_CHUNK
    vec = lambda n: pl.BlockSpec((1, n), lambda i: (0, 0))
    col = pl.BlockSpec((LANES, 1), lambda i: (0, 0))
    kern = functools.partial(_ssd_sample_rows_kernel, tdec=tdec)
    return pl.pallas_call(
        kern,
        grid=(t // q,),
        in_specs=[
            pl.BlockSpec((q, D_MODEL), lambda i: (i, 1)),
            pl.BlockSpec((q, D_MODEL), lambda i: (i, 2)),
            pl.BlockSpec((q, LANES), lambda i: (i, 0)),
            pl.BlockSpec((q, SSD_CONV_DIM), lambda i: (i, 0)),
            pl.BlockSpec((SSD_CONV, SSD_CONV_DIM), lambda i: (0, 0)),
            vec(SSD_CONV_DIM), vec(LANES), vec(LANES), col, col, vec(D_MODEL),
            pl.BlockSpec((LANES, D_MODEL), lambda i: (0, 0)),
        ],
        out_specs=[
            pl.BlockSpec((q, D_MODEL), lambda i: (i, 0)),
            pl.BlockSpec((q, D_MODEL), lambda i: (i, 0)),
            pl.BlockSpec((D_MODEL, q), lambda i: (0, i)),
            pl.BlockSpec((q, 512), lambda i: (i, 0)),
            pl.BlockSpec((q, 512), lambda i: (i, 0)),
            pl.BlockSpec((q, LANES), lambda i: (i, 0)),
        ],
        out_shape=[
            jax.ShapeDtypeStruct((t, D_MODEL), F32),
            jax.ShapeDtypeStruct((t, D_MODEL), F32),
            jax.ShapeDtypeStruct((D_MODEL, t), BF16),
            jax.ShapeDtypeStruct((t, 512), BF16),
            jax.ShapeDtypeStruct((t, 512), BF16),
            jax.ShapeDtypeStruct((t, LANES), F32),
        ],
        scratch_shapes=[pltpu.VMEM((q, D_MODEL), F32)],
        compiler_params=_cparams(("parallel",)),
        name="ssd_sample_rows",
    )(zx, zx, dt, hist, cw, cb, dtb, a_row, dtb_col, a_col, dsk, e01)


def _ssd_sample_state_kernel(cd_ref, z_ref, ypart_ref, ea_ref, xdt_ref, bm_ref, cm_ref, st_ref, nw_ref,
                             y_ref, sto_ref, *, tdec, gseq):
    i = pl.program_id(0)
    q = SSD_CHUNK
    rows = gseq * tdec
    steps_per_blk = q // rows
    lane = lax.broadcasted_iota(jnp.int32, (1, q), 1)
    rg = 16
    rowi = lax.broadcasted_iota(jnp.int32, (rg, 1), 0)
    lane_base = (i % steps_per_blk) * rows
    seq_per_rg = rg // tdec
    yoff_groups = []
    for q8 in range(rows // rg):
        yoff8 = jnp.zeros((rg, D_MODEL), F32)
        for sj in range(seq_per_rg):
            jj = q8 * seq_per_rg + sj
            b = i * gseq + jj
            lmask = (lane >= lane_base + jj * tdec) & (lane < lane_base + (jj + 1) * tdec)
            parts = []
            for g in range(SSD_N_GROUPS):
                s_g = st_ref[jj, g * 256:(g + 1) * 256, :]
                c8 = cm_ref[q8 * rg:(q8 + 1) * rg, g * SSD_D_STATE:(g + 1) * SSD_D_STATE]
                parts.append(_dot_nt(c8, s_g.astype(BF16)))
                lhs = jnp.where(lmask, xdt_ref[g * 256:(g + 1) * 256, :], jnp.zeros((), BF16))
                add = _dot(lhs, bm_ref[:, g * SSD_D_STATE:(g + 1) * SSD_D_STATE])
                for hh in range(4):
                    h = 4 * g + hh
                    sl = slice(hh * SSD_HEAD_DIM, (hh + 1) * SSD_HEAD_DIM)
                    sto_ref[jj, g * 256 + hh * 64:g * 256 + (hh + 1) * 64, :] = (
                        s_g[sl, :] * cd_ref[b, h] + add[sl, :])
            yo = jnp.concatenate(parts, axis=1)
            in_seq = (rowi >= sj * tdec) & (rowi < (sj + 1) * tdec)
            yoff8 = jnp.where(in_seq, yo, yoff8)
        yoff_groups.append(yoff8)
    yoff = jnp.concatenate(yoff_groups, axis=0) if len(yoff_groups) > 1 else yoff_groups[0]
    y = (ypart_ref[...] + yoff * ea_ref[...]) * _silu(z_ref[...])
    y_ref[...] = _rms(y, nw_ref[...])


def _ssd_sample_state(cd, zx, ypart, ea, xdt, bmb, cmb, state, nw, *, tdec, gseq):
    t = zx.shape[0]
    nseq = t // tdec
    q = SSD_CHUNK
    rows = gseq * tdec
    spb = q // rows
    kern = functools.partial(_ssd_sample_state_kernel, tdec=tdec, gseq=gseq)
    return pl.pallas_call(
        kern,
        grid=(nseq // gseq,),
        in_specs=[
            pl.BlockSpec(memory_space=pltpu.SMEM),
            pl.BlockSpec((rows, D_MODEL), lambda i: (i, 0)),
            pl.BlockSpec((rows, D_MODEL), lambda i: (i, 0)),
            pl.BlockSpec((rows, D_MODEL), lambda i: (i, 0)),
            pl.BlockSpec((D_MODEL, q), lambda i: (0, i // spb)),
            pl.BlockSpec((q, 512), lambda i: (i // spb, 0)),
            pl.BlockSpec((rows, 512), lambda i: (i, 0)),
            pl.BlockSpec((gseq, D_MODEL, SSD_D_STATE), lambda i: (i, 0, 0)),
            pl.BlockSpec((1, D_MODEL), lambda i: (0, 0)),
        ],
        out_specs=[
            pl.BlockSpec((rows, D_MODEL), lambda i: (i, 0)),
            pl.BlockSpec((gseq, D_MODEL, SSD_D_STATE), lambda i: (i, 0, 0)),
        ],
        out_shape=[
            jax.ShapeDtypeStruct((t, D_MODEL), F32),
            jax.ShapeDtypeStruct((nseq, D_MODEL, SSD_D_STATE), F32),
        ],
        compiler_params=_cparams(("parallel",)),
        name="ssd_sample_state",
    )(cd, zx, ypart, ea, xdt, bmb, cmb, state, nw)


def _attn_prompt_kernel(q_ref, kc_ref, kp_ref, vc_ref, vp_ref, o_ref, lse_ref):
    n = pl.program_id(2)
    nk = 128
    q = q_ref[...]
    k = jnp.concatenate([kp_ref[...], kc_ref[...]], axis=0)
    v = jnp.concatenate([vp_ref[...], vc_ref[...]], axis=0)
    qi = lax.broadcasted_iota(jnp.int32, (nk, 2 * nk), 0)
    kj = lax.broadcasted_iota(jnp.int32, (nk, 2 * nk), 1)
    valid = (kj >= qi) & (kj <= qi + nk) & ((kj >= nk) | (n > 0))
    lane = lax.broadcasted_iota(jnp.int32, (1, LANES), 1)
    lse_full = jnp.zeros((nk, LANES), F32)
    for p in range(ATT_HEADS // 2):
        lanes = slice(p * LANES, (p + 1) * LANES)
        qp, kp, vp = q[:, lanes], k[:, lanes], v[:, lanes]
        o_pair = jnp.zeros((nk, LANES), F32)
        for hh in range(2):
            hm = (lane // ATT_HEAD_DIM) == hh
            qm = jnp.where(hm, qp, jnp.zeros((), qp.dtype))
            s = jnp.where(valid, _dot_nt(qm, kp), NEG_INF)
            m = jnp.max(s, axis=-1, keepdims=True)
            e = jnp.exp(s - m)
            den = jnp.sum(e, axis=-1, keepdims=True)
            o = _dot(e.astype(BF16), vp) / den
            o_pair = jnp.where(hm, o, o_pair)
            lse_full = jnp.where((lane // 16) == 2 * p + hh, m + jnp.log(den), lse_full)
        o_ref[:, lanes] = o_pair.astype(o_ref.dtype)
    lse_ref[...] = lse_full


def _attn_prompt(qkv, *, d):
    nb, _, length, _ = qkv.shape
    nk = 128
    nblk = length // nk
    cur = lambda which: pl.BlockSpec((None, None, nk, COL_TILE), lambda b, r, n: (b, r, n, which))
    prev = lambda which: pl.BlockSpec((None, None, nk, COL_TILE), lambda b, r, n: (b, r, jnp.maximum(n - 1, 0), which))
    return pl.pallas_call(
        _attn_prompt_kernel,
        grid=(nb, d, nblk),
        in_specs=[cur(0), cur(1), prev(1), cur(2), prev(2)],
        out_specs=[
            pl.BlockSpec((None, None, nk, ATT_WIDTH), lambda b, r, n: (b, r, n, 0)),
            pl.BlockSpec((None, None, nk, LANES), lambda b, r, n: (b, r, n, 0)),
        ],
        out_shape=[
            jax.ShapeDtypeStruct((nb, d, length, ATT_WIDTH), BF16),
            jax.ShapeDtypeStruct((nb, d, length, LANES), F32),
        ],
        compiler_params=_cparams(("parallel", "parallel", "arbitrary")),
        name=f"attn_prompt_d{d}",
    )(qkv, qkv, qkv, qkv, qkv)


def _attn_sample_kernel(q0_ref, q1_ref, q2_ref, qm_ref, c0_ref, c1_ref, c2_ref, cm_ref, oa_ref, om_ref, *, tdec):
    nh = ATT_HEADS
    rows8 = 8
    j = pl.program_id(1)
    r64 = nh * rows8
    rowi = lax.broadcasted_iota(jnp.int32, (r64, 1), 0)
    head_of_row = rowi // rows8
    r8 = rowi % rows8
    seq_q = r8 // tdec
    t_q = r8 % tdec
    lane512 = lax.broadcasted_iota(jnp.int32, (1, ATT_WIDTH), 1)
    headmask = (lane512 // ATT_HEAD_DIM) == head_of_row
    col8 = lax.broadcasted_iota(jnp.int32, (1, rows8), 1)
    seq_k = col8 // tdec
    t_k = col8 % tdec

    outs, lses = [], []
    for g, (q_ref, c_ref, d) in enumerate(zip((q0_ref, q1_ref, q2_ref), (c0_ref, c1_ref, c2_ref), ATT_DILATIONS)):
        w = c_ref.shape[-1]
        qkv = q_ref[...]
        qv = qkv[:, :ATT_WIDTH]
        kn = qkv[:, ATT_WIDTH:2 * ATT_WIDTH]
        vn = qkv[:, 2 * ATT_WIDTH:]
        q64 = jnp.where(headmask, jnp.concatenate([qv] * nh, axis=0), 0.0)
        s_new = _dot_nt(q64, kn)
        if d == 1:
            valid_new = (seq_k == seq_q) & (t_k <= t_q)
        else:
            valid_new = (seq_k == seq_q) & (t_k == t_q)
        s_new = jnp.where(valid_new, s_new, NEG_INF)
        wi = lax.broadcasted_iota(jnp.int32, (1, w), 1)
        valid_c = (wi >= t_q) if d == 1 else ((wi % d) == t_q)
        s_c = jnp.where(valid_c, _dot(q64.astype(BF16), c_ref[0].astype(BF16)), NEG_INF)
        m = jnp.maximum(jnp.max(s_c, axis=-1, keepdims=True), jnp.max(s_new, axis=-1, keepdims=True))
        e_c = jnp.exp(s_c - m)
        e_n = jnp.exp(s_new - m)
        den = jnp.sum(e_c, axis=-1, keepdims=True) + jnp.sum(e_n, axis=-1, keepdims=True)
        o = _dot(e_n, vn) + _dot_nt(e_c.astype(BF16), c_ref[1].astype(BF16))
        outs.append(o / den)
        lses.append(m + jnp.log(den))

    mx = jnp.maximum(jnp.maximum(lses[0], lses[1]), lses[2])
    ws = [jnp.exp(l - mx) for l in lses]
    om = (ws[0] * outs[0] + ws[1] * outs[1] + ws[2] * outs[2]) / (ws[0] + ws[1] + ws[2])
    om = jnp.where(headmask, om, 0.0)
    o8 = om[0:rows8]
    for h in range(1, nh):
        o8 = o8 + om[h * rows8:(h + 1) * rows8]

    qm = jnp.concatenate([qm_ref[...] * (MEM_HEAD_DIM ** -0.5), jnp.zeros((rows8, MEM_WIDTH), F32)], axis=0)
    mparts = []
    for h in range(MEM_HEADS):
        kh = cm_ref[pl.ds(h, MEM_LEN, stride=2 * MEM_HEADS), :].astype(BF16)
        vh = cm_ref[pl.ds(MEM_HEADS + h, MEM_LEN, stride=2 * MEM_HEADS), :].astype(BF16)
        s = _dot_nt(qm[:, h * MEM_HEAD_DIM:(h + 1) * MEM_HEAD_DIM].astype(BF16), kh)
        mm = jnp.max(s, axis=-1, keepdims=True)
        e = jnp.exp(s - mm)
        den = jnp.sum(e, axis=-1, keepdims=True)
        mparts.append((_dot(e.astype(BF16), vh) / den)[0:rows8])
    m8 = jnp.concatenate(mparts, axis=1)

    row8 = lax.broadcasted_iota(jnp.int32, (rows8, 1), 0)
    mine = (row8 // tdec) == j

    @pl.when(j == 0)
    def _():
        oa_ref[...] = jnp.where(mine, o8, 0.0)
        om_ref[...] = jnp.where(mine, m8, 0.0)

    @pl.when(j > 0)
    def _():
        oa_ref[...] = jnp.where(mine, o8, oa_ref[...])
        om_ref[...] = jnp.where(mine, m8, om_ref[...])


def _attn_sample(q0, q1, q2, rest, c0, c1, c2, cmem, *, tdec):
    t = q0.shape[0]
    spg = 8 // tdec
    row_spec = pl.BlockSpec((8, 3 * COL_TILE), lambda i, j: (i, 0))
    cache_spec = lambda c: pl.BlockSpec((None,) + c.shape[1:], lambda i, j: (i * spg + j,) + (0,) * (c.ndim - 1))
    kern = functools.partial(_attn_sample_kernel, tdec=tdec)
    return pl.pallas_call(
        kern,
        grid=(t // 8, spg),
        in_specs=[
            row_spec, row_spec, row_spec,
            pl.BlockSpec((8, COL_TILE), lambda i, j: (i, 1)),
            cache_spec(c0), cache_spec(c1), cache_spec(c2), cache_spec(cmem),
        ],
        out_specs=[
            pl.BlockSpec((8, ATT_WIDTH), lambda i, j: (i, 0)),
            pl.BlockSpec((8, MEM_WIDTH), lambda i, j: (i, 0)),
        ],
        out_shape=[
            jax.ShapeDtypeStruct((t, ATT_WIDTH), F32),
            jax.ShapeDtypeStruct((t, MEM_WIDTH), F32),
        ],
        compiler_params=_cparams(("parallel", "arbitrary")),
        name="attn_sample",
    )(q0, q1, q2, rest, c0, c1, c2, cmem)


def _final_tail(x, y_ssd, y_att, y_mem, gate_raw, wssd_ref, watt_ref, wmem_ref, wout_ref, nf_ref, y_ref):
    gates = _sigmoid(gate_raw)
    merged = (gates[:, 0:D_MODEL] * _dot(y_ssd.astype(BF16), wssd_ref[...])
              + gates[:, D_MODEL:2 * D_MODEL] * _dot(y_att.astype(BF16), watt_ref[...])
              + gates[:, 2 * D_MODEL:] * _dot(y_mem.astype(BF16), wmem_ref[...]))
    y_ref[...] = _rms(x + _dot(merged.astype(BF16), wout_ref[...]), nf_ref[...])


def _final_prompt_kernel(x_ref, ys_ref, o0_ref, o1_ref, o2_ref, l0_ref, l1_ref, l2_ref, rest_ref, mkv_ref,
                         wssd_ref, watt_ref, wmem_ref, wout_ref, nf_ref, e16_ref,
                         y_ref, obuf, lbuf, *, tm, dils):
    nslab = ATT_WIDTH // LANES
    o_nat, l_nat = [], []
    for g, (o_ref, l_ref, d) in enumerate(zip((o0_ref, o1_ref, o2_ref), (l0_ref, l1_ref, l2_ref), dils)):
        if d == 1:
            o_nat.append(o_ref[0].astype(F32))
            l_nat.append(l_ref[0])
            continue
        for r in range(d):
            blk = o_ref[r].astype(F32)
            for c in range(nslab):
                obuf[g, c, pl.ds(r, tm // d, stride=d), :] = blk[:, c * LANES:(c + 1) * LANES]
            lbuf[g, pl.ds(r, tm // d, stride=d), :] = l_ref[r]
        o_nat.append(jnp.concatenate([obuf[g, c] for c in range(nslab)], axis=1))
        l_nat.append(lbuf[g])
    mx = jnp.maximum(jnp.maximum(l_nat[0], l_nat[1]), l_nat[2])
    ws = [jnp.exp(l - mx) for l in l_nat]
    wsum = ws[0] + ws[1] + ws[2]
    e16 = e16_ref[...]
    o_att = jnp.zeros((tm, ATT_WIDTH), F32)
    for g in range(3):
        hi, mid, _ = _split3(ws[g] / wsum)
        o_att = o_att + (_dot(hi, e16) + _dot(mid, e16)) * o_nat[g]

    rest = rest_ref[...]
    z_att = rest[:, 0:512].astype(F32)
    q_mem = rest[:, 512:1024]
    z_mem = rest[:, 1024:1536].astype(F32)
    gate_raw = rest[:, 1536:].astype(F32)
    y_att = o_att * _silu(z_att)

    outs = []
    for h in range(MEM_HEADS):
        lanes = slice(h * MEM_HEAD_DIM, (h + 1) * MEM_HEAD_DIM)
        kh = mkv_ref[:, h * MEM_HEAD_DIM:(h + 1) * MEM_HEAD_DIM]
        vh = mkv_ref[:, MEM_WIDTH + h * MEM_HEAD_DIM:MEM_WIDTH + (h + 1) * MEM_HEAD_DIM]
        s = _dot_nt(q_mem[:, lanes], kh) * (MEM_HEAD_DIM ** -0.5)
        m = jnp.max(s, axis=-1, keepdims=True)
        e = jnp.exp(s - m)
        den = jnp.sum(e, axis=-1, keepdims=True)
        outs.append(_dot(e.astype(BF16), vh) / den)
    y_mem = jnp.concatenate(outs, axis=1) * _silu(z_mem)

    _final_tail(x_ref[...], ys_ref[...], y_att, y_mem, gate_raw,
                wssd_ref, watt_ref, wmem_ref, wout_ref, nf_ref, y_ref)


def _final_prompt(x2d, y_ssd, o_g, l_g, rest, mkv_bf, wssd, watt, wmem, wout, nf, e16, *, seq, tm, dils):
    t = x2d.shape[0]
    tps = seq // tm

    def res_spec(d, width):
        return pl.BlockSpec((None, d, tm // d, width), lambda i: (i // tps, 0, i % tps, 0))

    full = lambda a: pl.BlockSpec(a.shape, lambda i: (0,) * a.ndim)
    kern = functools.partial(_final_prompt_kernel, tm=tm, dils=dils)
    return pl.pallas_call(
        kern,
        grid=(t // tm,),
        in_specs=[
            pl.BlockSpec((tm, D_MODEL), lambda i: (i, 0)),
            pl.BlockSpec((tm, D_MODEL), lambda i: (i, 0)),
            res_spec(dils[0], ATT_WIDTH), res_spec(dils[1], ATT_WIDTH), res_spec(dils[2], ATT_WIDTH),
            res_spec(dils[0], LANES), res_spec(dils[1], LANES), res_spec(dils[2], LANES),
            pl.BlockSpec((tm, rest.shape[1]), lambda i: (i, 0)),
            pl.BlockSpec((None, MEM_LEN, 2 * MEM_WIDTH), lambda i: (i // tps, 0, 0)),
            full(wssd), full(watt), full(wmem), full(wout), full(nf), full(e16),
        ],
        out_specs=pl.BlockSpec((tm, D_MODEL), lambda i: (i, 0)),
        out_shape=jax.ShapeDtypeStruct((t, D_MODEL), F32),
        scratch_shapes=[
            pltpu.VMEM((3, ATT_WIDTH // LANES, tm, LANES), F32),
            pltpu.VMEM((3, tm, LANES), F32),
        ],
        compiler_params=_cparams(("parallel",)),
        name="final_prompt",
    )(x2d, y_ssd, *o_g, *l_g, rest, mkv_bf, wssd, watt, wmem, wout, nf, e16)


def _final_sample_kernel(x_ref, ys_ref, oa_ref, om_ref, rest_ref,
                         wssd_ref, watt_ref, wmem_ref, wout_ref, nf_ref, y_ref):
    rest = rest_ref[...]
    y_att = oa_ref[...] * _silu(rest[:, 0:512])
    y_mem = om_ref[...] * _silu(rest[:, 1024:1536])
    _final_tail(x_ref[...], ys_ref[...], y_att, y_mem, rest[:, 1536:],
                wssd_ref, watt_ref, wmem_ref, wout_ref, nf_ref, y_ref)


def _final_sample(x2d, y_ssd, o_att, o_mem, rest, wssd, watt, wmem, wout, nf, *, tm):
    t = x2d.shape[0]
    full = lambda a: pl.BlockSpec(a.shape, lambda i: (0,) * a.ndim)
    rows = lambda w: pl.BlockSpec((tm, w), lambda i: (i, 0))
    return pl.pallas_call(
        _final_sample_kernel,
        grid=(t // tm,),
        in_specs=[rows(D_MODEL), rows(D_MODEL), rows(ATT_WIDTH), rows(MEM_WIDTH), rows(rest.shape[1]),
                  full(wssd), full(watt), full(wmem), full(wout), full(nf)],
        out_specs=rows(D_MODEL),
        out_shape=jax.ShapeDtypeStruct((t, D_MODEL), F32),
        compiler_params=_cparams(("parallel",)),
        name="final_sample",
    )(x2d, y_ssd, o_att, o_mem, rest, wssd, watt, wmem, wout, nf)


def _from_residues(a):
    b, d, length, c = a.shape
    return a.transpose(0, 2, 1, 3).reshape(b, length * d, c)


def kernel(x_prompt, x_sample, mem_prompt, cache_win128_kv, cache_win512_kv, cache_win2048_kv, cache_mem_kv,
           state_conv, state_ssm, norm_in_w, w_in, conv_w, conv_b, dt_bias, a_log, d_skip, ssd_norm_w,
           mem_norm_w, w_mem_kv, w_br_ssd, w_br_att, w_br_mem, w_out, norm_f_w):
    bp, seq, _ = x_prompt.shape
    bs, tdec, _ = x_sample.shape
    dils = ATT_DILATIONS

    row = lambda v: v.reshape(1, -1).astype(F32)
    dt_col0 = D_MODEL + SSD_CONV_DIM
    w_main = jnp.concatenate([w_in[:, :dt_col0], w_in[:, dt_col0 + SSD_N_HEADS:]], axis=1).astype(BF16)
    w_dt = jnp.pad(w_in[:, dt_col0:dt_col0 + SSD_N_HEADS], ((0, 0), (0, LANES - SSD_N_HEADS))).astype(BF16)
    pad_heads = lambda v: jnp.pad(v.astype(F32), (0, LANES - SSD_N_HEADS)).reshape(1, LANES)
    dtb = pad_heads(dt_bias)
    a_row = jnp.pad(-jnp.exp(a_log.astype(F32)), (0, LANES - SSD_N_HEADS)).reshape(1, LANES)
    dtb_col, a_col = dtb.reshape(LANES, 1), a_row.reshape(LANES, 1)
    dsk = jnp.repeat(d_skip.astype(F32), SSD_HEAD_DIM).reshape(1, D_MODEL)
    cw = conv_w.astype(F32)
    cb = row(conv_b)
    nin, nssd, nmem, nf = row(norm_in_w), row(ssd_norm_w), row(mem_norm_w), row(norm_f_w)
    wssd, watt, wmem, wout = (w.astype(BF16) for w in (w_br_ssd, w_br_att, w_br_mem, w_out))
    lane = jnp.arange(LANES)
    e_ssd = (lane[:, None] == (jnp.arange(D_MODEL)[None, :] // SSD_HEAD_DIM)).astype(BF16)
    e_att = (lane[:, None] == 16 * (jnp.arange(ATT_WIDTH)[None, :] // ATT_HEAD_DIM)).astype(BF16)

    xp2 = x_prompt.reshape(bp * seq, D_MODEL)
    mkv_f32, mkv_bf = _mem_kv(mem_prompt.reshape(bp * MEM_LEN, D_MODEL), nmem, w_mem_kv.astype(BF16))
    tm_p = 256
    tabs_p = _rope_tables(jnp.arange(seq))
    zx, q0, q1, q2, rest, dtr = _in_proj(xp2, nin, w_main, w_dt, tabs_p, seq=seq, tm=tm_p, dils=dils, out_dtype=BF16)
    y_ssd, p_ssm = _ssd_prompt(zx, dtr, cw, cb, dtb, a_row, dtb_col, a_col, dsk, nssd, nbatch=bp, seq=seq)
    o_g, l_g = [], []
    for qkv, d in zip((q0, q1, q2), dils):
        o, l = _attn_prompt(qkv, d=d)
        o_g.append(o)
        l_g.append(l)
    y_prompt = _final_prompt(xp2, y_ssd, o_g, l_g, rest, mkv_bf.reshape(bp, MEM_LEN, 2 * MEM_WIDTH),
                             wssd, watt, wmem, wout, nf, e_att, seq=seq, tm=256, dils=dils)
    y_prompt = y_prompt.reshape(bp, seq, D_MODEL)

    p_win = []
    for qkv, d, w in zip((q0, q1, q2), dils, ATT_WINDOWS):
        wl = min(w, seq)
        kv = _from_residues(qkv[:, :, (seq - wl) // d:, COL_TILE:])
        p_win.append(kv.astype(F32).reshape(bp, wl, 2, ATT_HEADS, ATT_HEAD_DIM))
    p_mem_kv = mkv_f32.reshape(bp, MEM_LEN, 2, MEM_HEADS, MEM_HEAD_DIM)
    p_conv = zx.reshape(bp, seq, -1)[:, seq - (SSD_CONV - 1):, D_MODEL:].astype(F32)
    p_ssm = p_ssm.reshape(bp, SSD_N_HEADS, SSD_HEAD_DIM, SSD_D_STATE)

    ts = bs * tdec
    pos_s = PAST_LEN + jnp.tile(jnp.arange(tdec), bs)
    xs2 = x_sample.reshape(ts, D_MODEL)
    tabs_s = _rope_tables(pos_s)
    zx_s, q0_s, q1_s, q2_s, rest_s, dt_s = _in_proj(
        xs2, nin, w_main, w_dt, tabs_s, seq=ts, tm=min(ts, 128), dils=(1, 1, 1), out_dtype=F32)
    q_s = [a.reshape(ts, 3 * COL_TILE) for a in (q0_s, q1_s, q2_s)]
    hist = jnp.pad(state_conv.astype(F32), ((0, 0), (tdec - (SSD_CONV - 1), 0), (0, 0))).reshape(ts, SSD_CONV_DIM)
    ypart, ea, xdt, bmb, cmb, cd = _ssd_sample_rows(
        zx_s, dt_s, hist, cw, cb, dtb, a_row, dtb_col, a_col, dsk, e_ssd, tdec=tdec)
    cd_seq = cd.reshape(bs, tdec, LANES)[:, 0, :SSD_N_HEADS]
    y_ssd_s, s_ssm = _ssd_sample_state(cd_seq, zx_s, ypart, ea, xdt, bmb, cmb,
                                       state_ssm.reshape(bs, D_MODEL, SSD_D_STATE), nssd, tdec=tdec, gseq=8)
    to_kt = lambda c: c.transpose(0, 2, 3, 4, 1).reshape(bs, 2, ATT_WIDTH, c.shape[1])
    c0, c1, c2 = to_kt(cache_win128_kv), to_kt(cache_win512_kv), to_kt(cache_win2048_kv)
    cmem = cache_mem_kv.reshape(bs, MEM_LEN * 2 * MEM_HEADS, MEM_HEAD_DIM)
    o_att_s, o_mem_s = _attn_sample(*q_s, rest_s, c0, c1, c2, cmem, tdec=tdec)
    y_sample = _final_sample(xs2, y_ssd_s, o_att_s, o_mem_s, rest_s, wssd, watt, wmem, wout, nf, tm=min(ts, 256))
    y_sample = y_sample.reshape(bs, tdec, D_MODEL)

    s_win = [a[:, COL_TILE:].reshape(bs, tdec, 2, ATT_HEADS, ATT_HEAD_DIM) for a in q_s]
    s_conv = zx_s.reshape(bs, tdec, -1)[:, tdec - (SSD_CONV - 1):, D_MODEL:]
    s_ssm = s_ssm.reshape(bs, SSD_N_HEADS, SSD_HEAD_DIM, SSD_D_STATE)

    return (y_prompt, y_sample, p_win[0], p_win[1], p_win[2], p_mem_kv, p_conv, p_ssm,
            s_win[0], s_win[1], s_win[2], s_conv, s_ssm)
```

```python
import functools
import math

import jax
import jax.numpy as jnp
from jax import lax
from jax.experimental import pallas as pl
from jax.experimental.pallas import tpu as pltpu

F32 = jnp.float32
BF16 = jnp.bfloat16

D_MODEL = 1024
NORM_EPS = 1e-6
SSD_HEAD_DIM = 64
SSD_N_HEADS = 16
SSD_N_GROUPS = 4
SSD_D_STATE = 128
SSD_CONV = 4
SSD_CHUNK = 128
SSD_CONV_DIM = 2048
ATT_WINDOWS = (128, 512, 2048)
ATT_DILATIONS = (1, 4, 16)
ATT_HEADS = 8
ATT_HEAD_DIM = 64
ATT_WIDTH = 512
ROPE_DIM = 16
ROPE_THETA = 500000.0
MEM_LEN = 256
MEM_HEADS = 4
MEM_HEAD_DIM = 128
MEM_WIDTH = 512
NEG_INF = -1e30
PAST_LEN = 8192

LANES = 128
COL_TILE = 512
N_COL_TILES = 24
QKV_TILE0 = 6
REST_TILE0 = 15
VMEM_LIMIT = 56 * 1024 * 1024


def _cparams(sem):
    return pltpu.CompilerParams(dimension_semantics=sem, vmem_limit_bytes=VMEM_LIMIT)


def _dot(a, b):
    return jnp.dot(a, b, preferred_element_type=F32)


def _dot_nt(a, b):
    return lax.dot_general(a, b, (((1,), (1,)), ((), ())), preferred_element_type=F32)


def _rms(x, w):
    return x * lax.rsqrt(jnp.mean(x * x, axis=-1, keepdims=True) + NORM_EPS) * w


def _sigmoid(x):
    return 0.5 * jnp.tanh(0.5 * x) + 0.5


def _silu(x):
    return x * _sigmoid(x)


def _softplus(x):
    return jnp.maximum(x, 0.0) + jnp.log(1.0 + jnp.exp(-jnp.abs(x)))


def _split3(x):
    hi = x.astype(BF16)
    r1 = x - hi.astype(F32)
    mid = r1.astype(BF16)
    lo = (r1 - mid.astype(F32)).astype(BF16)
    return hi, mid, lo


def _dot_exact_lhs01(m01, x):
    hi, mid, lo = _split3(x)
    return _dot(m01, hi) + _dot(m01, mid) + _dot(m01, lo)


def _dot_exact_rhs01(x, m01):
    hi, mid, lo = _split3(x)
    return _dot(hi, m01) + _dot(mid, m01) + _dot(lo, m01)


def _mem_kv_kernel(x_ref, nw_ref, w_ref, o_ref, ob_ref):
    h = _rms(x_ref[...], nw_ref[...]).astype(BF16)
    acc = _dot(h, w_ref[...])
    o_ref[...] = acc
    ob_ref[...] = acc.astype(BF16)


def _mem_kv(mem2d, nw, w_bf):
    m = mem2d.shape[0]
    tm = 256
    return pl.pallas_call(
        _mem_kv_kernel,
        grid=(m // tm,),
        in_specs=[
            pl.BlockSpec((tm, D_MODEL), lambda i: (i, 0)),
            pl.BlockSpec((1, D_MODEL), lambda i: (0, 0)),
            pl.BlockSpec((D_MODEL, 2 * MEM_WIDTH), lambda i: (0, 0)),
        ],
        out_specs=[
            pl.BlockSpec((tm, 2 * MEM_WIDTH), lambda i: (i, 0)),
            pl.BlockSpec((tm, 2 * MEM_WIDTH), lambda i: (i, 0)),
        ],
        out_shape=[
            jax.ShapeDtypeStruct((m, 2 * MEM_WIDTH), F32),
            jax.ShapeDtypeStruct((m, 2 * MEM_WIDTH), BF16),
        ],
        compiler_params=_cparams(("parallel",)),
        name="mem_kv",
    )(mem2d, nw, w_bf)


def _rope_tables(pos):
    half = ROPE_DIM // 2
    inv = jnp.power(ROPE_THETA, -jnp.arange(half, dtype=F32) * 2.0 / ROPE_DIM)
    ang = pos.astype(F32)[:, None] * inv[None, :]
    cos, sin = jnp.cos(ang), jnp.sin(ang)
    m = jnp.arange(LANES) % ATT_HEAD_DIM
    idx = m % half
    c = jnp.where(m[None, :] < ROPE_DIM, cos[:, idx], 1.0)
    s1 = jnp.where(m[None, :] < half, -sin[:, idx], 0.0)
    s2 = jnp.where((m[None, :] >= half) & (m[None, :] < ROPE_DIM), sin[:, idx], 0.0)
    return c.astype(F32), s1.astype(F32), s2.astype(F32)


def _in_proj_kernel(x_ref, nw_ref, w_ref, wdt_ref, c_ref, s1_ref, s2_ref,
                    zx_ref, q0_ref, q1_ref, q2_ref, rest_ref, dt_ref, *more_refs, tm, dils, win_rows):
    pw_refs = more_refs[:3] if win_rows else ()
    h_ref, acc_ref, tmp_ref = more_refs[-3:]
    nslab = COL_TILE // LANES
    h_ref[...] = _rms(x_ref[...], nw_ref[...]).astype(BF16)
    dt_ref[...] = _dot(h_ref[...], wdt_ref[...])
    q_refs = (q0_ref, q1_ref, q2_ref)
    strided = 0
    for j in range(N_COL_TILES):
        acc = _dot(h_ref[...], w_ref[:, j * COL_TILE:(j + 1) * COL_TILE])
        if j < QKV_TILE0:
            zx_ref[:, j * COL_TILE:(j + 1) * COL_TILE] = acc.astype(zx_ref.dtype)
            continue
        if j >= REST_TILE0:
            rest_ref[:, (j - REST_TILE0) * COL_TILE:(j - REST_TILE0 + 1) * COL_TILE] = acc.astype(rest_ref.dtype)
            continue
        g, kind = divmod(j - QKV_TILE0, 3)
        out_ref, d = q_refs[g], dils[g]
        for c in range(nslab):
            a = acc[:, c * LANES:(c + 1) * LANES]
            if kind < 2:
                a = (a * c_ref[...] + pltpu.roll(a, LANES - ROPE_DIM // 2, 1) * s1_ref[...]
                     + pltpu.roll(a, ROPE_DIM // 2, 1) * s2_ref[...])
            if kind == 0:
                a = a * (ATT_HEAD_DIM ** -0.5)
            lanes = slice(kind * COL_TILE + c * LANES, kind * COL_TILE + (c + 1) * LANES)
            if win_rows and kind > 0:
                ch0 = (kind - 1) * COL_TILE + c * LANES
                pw_refs[g][ch0:ch0 + LANES, :] = a[tm - win_rows[g]:, :].T
            if d == 1:
                out_ref[0, :, lanes] = a.astype(out_ref.dtype)
            elif d == 4:
                acc_ref[strided, c] = a
                for r in range(4):
                    out_ref[r, :, lanes] = acc_ref[strided, c, pl.ds(r, tm // 4, stride=4), :].astype(out_ref.dtype)
            else:
                acc_ref[strided, c] = a
                q4 = tm // 4
                for r1 in range(4):
                    tmp_ref[kind, c, r1 * q4:(r1 + 1) * q4, :] = acc_ref[strided, c, pl.ds(r1, q4, stride=4), :]
                for r1 in range(4):
                    for r2 in range(4):
                        out_ref[r1 + 4 * r2, :, lanes] = tmp_ref[
                            kind, c, pl.ds(r1 * q4 + r2, tm // 16, stride=4), :].astype(out_ref.dtype)
        if d > 1:
            strided += 1


def _in_proj(x2d, nw, w_main, w_dt, tabs, *, seq, tm, dils, out_dtype, windows=None):
    t = x2d.shape[0]
    nb = t // seq
    tiles_per_seq = seq // tm
    tab_blocks = tabs[0].shape[0] // tm
    n_strided = 3 * sum(1 for d in dils if d > 1)
    nslab = COL_TILE // LANES

    def qkv_spec(d):
        return pl.BlockSpec((None, d, tm // d, 3 * COL_TILE),
                            lambda i: (i // tiles_per_seq, 0, i % tiles_per_seq, 0))

    win_rows, win_specs, win_shapes = None, [], []
    if windows:
        win_rows = tuple(min(tm, w) for w in windows)
        for w, wr in zip(windows, win_rows):
            first = tiles_per_seq - w // wr
            win_specs.append(pl.BlockSpec(
                (None, 2 * COL_TILE, wr),
                lambda i, first=first: (i // tiles_per_seq, 0, jnp.maximum(i % tiles_per_seq - first, 0))))
            win_shapes.append(jax.ShapeDtypeStruct((nb, 2 * COL_TILE, w), F32))

    tab_spec = pl.BlockSpec((tm, LANES), lambda i: (i % tab_blocks, 0))
    resident = lambda shape: pl.BlockSpec(shape, lambda i: (0, 0), pipeline_mode=pl.Buffered(1))
    kern = functools.partial(_in_proj_kernel, tm=tm, dils=dils, win_rows=win_rows)
    n_zx, n_rest = QKV_TILE0 * COL_TILE, (N_COL_TILES - REST_TILE0) * COL_TILE
    return pl.pallas_call(
        kern,
        grid=(t // tm,),
        in_specs=[
            pl.BlockSpec((tm, D_MODEL), lambda i: (i, 0)),
            resident((1, D_MODEL)),
            resident(w_main.shape),
            resident((D_MODEL, LANES)),
            tab_spec, tab_spec, tab_spec,
        ],
        out_specs=[
            pl.BlockSpec((tm, n_zx), lambda i: (i, 0)),
            qkv_spec(dils[0]), qkv_spec(dils[1]), qkv_spec(dils[2]),
            pl.BlockSpec((tm, n_rest), lambda i: (i, 0)),
            pl.BlockSpec((tm, LANES), lambda i: (i, 0)),
        ] + win_specs,
        out_shape=[
            jax.ShapeDtypeStruct((t, n_zx), out_dtype),
            jax.ShapeDtypeStruct((nb, dils[0], seq // dils[0], 3 * COL_TILE), out_dtype),
            jax.ShapeDtypeStruct((nb, dils[1], seq // dils[1], 3 * COL_TILE), out_dtype),
            jax.ShapeDtypeStruct((nb, dils[2], seq // dils[2], 3 * COL_TILE), out_dtype),
            jax.ShapeDtypeStruct((t, n_rest), out_dtype),
            jax.ShapeDtypeStruct((t, LANES), F32),
        ] + win_shapes,
        scratch_shapes=[
            pltpu.VMEM((tm, D_MODEL), BF16),
            pltpu.VMEM((max(n_strided, 1), nslab, tm, LANES), F32),
            pltpu.VMEM((3, nslab, tm, LANES), F32),
        ],
        compiler_params=_cparams(("arbitrary",)),
        name="in_proj",
    )(x2d, nw, w_main, w_dt, *tabs)


def _ssd_small(dt_raw, dtb_row, a_row, dtb_col, a_col, same01, same01_t, tot01, tot01_t):
    dt = _softplus(dt_raw + dtb_row)
    dta = dt * a_row
    acs = _dot_exact_lhs01(same01, dta)
    tot = _dot_exact_lhs01(tot01, dta)
    dtd = dt * jnp.exp(tot - acs)
    dt_raw_t = dt_raw.T
    dt_t = _softplus(dt_raw_t + dtb_col)
    dta_t = dt_t * a_col
    acs_t = _dot_exact_rhs01(dta_t, same01_t)
    tot_t = _dot_exact_rhs01(dta_t, tot01_t)
    dtd_t = dt_t * jnp.exp(tot_t - acs_t)
    return dt, acs, tot, dtd, dt_t, acs_t, dtd_t


def _ssd_pairs(x, bm, cm, acs, acs_t, dt_t, dtd_t, mask, st_ref, y_ref_write, st_scale_row):
    lane = lax.broadcasted_iota(jnp.int32, (1, LANES), 1)
    half = [lane < ATT_HEAD_DIM, lane >= ATT_HEAD_DIM]
    bm_t = [bm[:, g * SSD_D_STATE:(g + 1) * SSD_D_STATE].T for g in range(SSD_N_GROUPS)]
    cms = [cm[:, g * SSD_D_STATE:(g + 1) * SSD_D_STATE] for g in range(SSD_N_GROUPS)]
    cb = [_dot(cms[g].astype(BF16), bm_t[g].astype(BF16)) for g in range(SSD_N_GROUPS)]
    for p in range(SSD_N_HEADS // 2):
        g = p // 2
        xp = x[:, p * LANES:(p + 1) * LANES]
        if st_ref is not None:
            stp = st_ref[:, p * LANES:(p + 1) * LANES]
        y_pair = jnp.zeros((SSD_CHUNK, LANES), F32)
        st_add = jnp.zeros((SSD_D_STATE, LANES), F32)
        for hh in range(2):
            h = 2 * p + hh
            col = acs[:, h:h + 1]
            row = acs_t[h:h + 1, :]
            lmat = jnp.exp(jnp.where(mask, col - row, NEG_INF))
            m = cb[g] * lmat * dt_t[h:h + 1, :]
            xm = jnp.where(half[hh], xp, 0.0).astype(BF16)
            if st_ref is not None:
                cx = cms[g] * jnp.exp(col)
                lhs = jnp.concatenate([m, cx], axis=1).astype(BF16)
                rhs = jnp.concatenate([xm, jnp.where(half[hh], stp, 0.0).astype(BF16)], axis=0)
                y_pair = y_pair + _dot(lhs, rhs)
                bt_h = (bm_t[g] * dtd_t[h:h + 1, :]).astype(BF16)
                st_add = st_add + _dot(bt_h, xm)
            else:
                y_pair = y_pair + _dot(m.astype(BF16), xm)
        y_ref_write(p, y_pair)
        if st_ref is not None:
            scale = jnp.where(half[0], st_scale_row[:, 2 * p:2 * p + 1], st_scale_row[:, 2 * p + 1:2 * p + 2])
            st_ref[:, p * LANES:(p + 1) * LANES] = stp * scale + st_add


def _ssd_prompt_kernel(z_ref, xs_ref, bc_ref, dt_ref, cw_ref, cb_ref, dtb_ref, a_ref, dtbc_ref, ac_ref,
                       dsk_ref, nw_ref, sh_ref, y_ref, ssm_ref, xbuf, st_ref, ybuf):
    c = pl.program_id(1)
    q = SSD_CHUNK

    @pl.when(c == 0)
    def _():
        xbuf[0:q, :] = jnp.zeros((q, SSD_CONV_DIM), xbuf.dtype)
        st_ref[...] = jnp.zeros(st_ref.shape, F32)

    xbuf[q:2 * q, 0:D_MODEL] = xs_ref[...]
    xbuf[q:2 * q, D_MODEL:SSD_CONV_DIM] = bc_ref[...]
    shifted = _dot(sh_ref[...], xbuf[...])
    acc = cb_ref[...] + cw_ref[SSD_CONV - 1:SSD_CONV, :] * xbuf[q:2 * q, :].astype(F32)
    for s in range(1, SSD_CONV):
        acc = acc + cw_ref[SSD_CONV - 1 - s:SSD_CONV - s, :] * shifted[(s - 1) * q:s * q]
    xbuf[0:q, :] = xbuf[q:2 * q, :]
    xc = _silu(acc)
    x = xc[:, :D_MODEL]
    bm = xc[:, D_MODEL:D_MODEL + 512]
    cm = xc[:, D_MODEL + 512:]

    ri = lax.broadcasted_iota(jnp.int32, (q, q), 0)
    ci = lax.broadcasted_iota(jnp.int32, (q, q), 1)
    tril = ri >= ci
    tril01 = jnp.where(tril, 1.0, 0.0).astype(BF16)
    triu01 = jnp.where(ri <= ci, 1.0, 0.0).astype(BF16)
    ones01 = jnp.ones((q, q), BF16)
    dt, acs, tot, dtd, dt_t, acs_t, dtd_t = _ssd_small(
        dt_ref[...], dtb_ref[...], a_ref[...], dtbc_ref[...], ac_ref[...], tril01, triu01, ones01, ones01)
    decay_row = jnp.exp(tot[0:1, :])

    def write_y(p, y_pair):
        ybuf[:, p * LANES:(p + 1) * LANES] = y_pair

    _ssd_pairs(x, bm, cm, acs, acs_t, dt_t, dtd_t, tril, st_ref, write_y, decay_row)

    y = (ybuf[...] + dsk_ref[...] * x) * _silu(z_ref[...].astype(F32))
    y_ref[...] = _rms(y, nw_ref[...]).astype(y_ref.dtype)

    @pl.when(c == pl.num_programs(1) - 1)
    def _():
        for p in range(D_MODEL // LANES):
            ssm_ref[p * LANES:(p + 1) * LANES, :] = st_ref[:, p * LANES:(p + 1) * LANES].T


def _ssd_prompt(zx, dt, cw, cb, dtb, a_row, dtb_col, a_col, dsk, nw, *, nbatch, seq):
    nc = seq // SSD_CHUNK
    q = SSD_CHUNK
    row = lambda b, c: b * nc + c
    vec = lambda n: pl.BlockSpec((1, n), lambda b, c: (0, 0))
    col = pl.BlockSpec((LANES, 1), lambda b, c: (0, 0))
    nsh = SSD_CONV - 1
    out_t = jnp.arange(nsh * q) % q
    shift = 1 + jnp.arange(nsh * q) // q
    shift01 = (jnp.arange(2 * q)[None, :] == (q + out_t - shift)[:, None]).astype(BF16)
    return pl.pallas_call(
        _ssd_prompt_kernel,
        grid=(nbatch, nc),
        in_specs=[
            pl.BlockSpec((q, D_MODEL), lambda b, c: (row(b, c), 0)),
            pl.BlockSpec((q, D_MODEL), lambda b, c: (row(b, c), 1)),
            pl.BlockSpec((q, D_MODEL), lambda b, c: (row(b, c), 2)),
            pl.BlockSpec((q, LANES), lambda b, c: (row(b, c), 0)),
            pl.BlockSpec((SSD_CONV, SSD_CONV_DIM), lambda b, c: (0, 0)),
            vec(SSD_CONV_DIM), vec(LANES), vec(LANES), col, col, vec(D_MODEL), vec(D_MODEL),
            pl.BlockSpec(((SSD_CONV - 1) * q, 2 * q), lambda b, c: (0, 0)),
        ],
        out_specs=[
            pl.BlockSpec((q, D_MODEL), lambda b, c: (row(b, c), 0)),
            pl.BlockSpec((None, D_MODEL, SSD_D_STATE), lambda b, c: (b, 0, 0)),
        ],
        out_shape=[
            jax.ShapeDtypeStruct((nbatch * seq, D_MODEL), BF16),
            jax.ShapeDtypeStruct((nbatch, D_MODEL, SSD_D_STATE), F32),
        ],
        scratch_shapes=[
            pltpu.VMEM((2 * q, SSD_CONV_DIM), BF16),
            pltpu.VMEM((SSD_D_STATE, D_MODEL), F32),
            pltpu.VMEM((q, D_MODEL), F32),
        ],
        compiler_params=_cparams(("parallel", "arbitrary")),
        name="ssd_prompt",
    )(zx, zx, zx, dt, cw, cb, dtb, a_row, dtb_col, a_col, dsk, nw, shift01)


def _expand_heads(v, e01):
    return _dot_exact_rhs01(v, e01)


def _ssd_sample_rows_kernel(xs_ref, bc_ref, dt_ref, hist_ref, cw_ref, cb_ref, dtb_ref, a_ref, dtbc_ref, ac_ref,
                            dsk_ref, e01_ref,
                            ypart_ref, ea_ref, xdt_ref, bm_ref, cm_ref, cd_ref, ybuf, *, tdec):
    q = SSD_CHUNK
    x_raw = jnp.concatenate([xs_ref[...], bc_ref[...]], axis=1)
    hist = hist_ref[...]
    rowi = lax.broadcasted_iota(jnp.int32, (q, 1), 0)
    tpos = rowi % tdec
    acc = cb_ref[...] + cw_ref[SSD_CONV - 1:SSD_CONV, :] * x_raw
    for s in range(1, SSD_CONV):
        prev = jnp.where(tpos >= s, pltpu.roll(x_raw, s, 0), pltpu.roll(hist, q - tdec + s, 0))
        acc = acc + cw_ref[SSD_CONV - 1 - s:SSD_CONV - s, :] * prev
    xc = _silu(acc)
    x = xc[:, :D_MODEL]
    bm = xc[:, D_MODEL:D_MODEL + 512]
    cm = xc[:, D_MODEL + 512:]

    ri = lax.broadcasted_iota(jnp.int32, (q, q), 0)
    ci = lax.broadcasted_iota(jnp.int32, (q, q), 1)
    same = (ri // tdec) == (ci // tdec)
    mask = same & (ri >= ci)
    low01 = jnp.where(mask, 1.0, 0.0).astype(BF16)
    up01 = jnp.where(same & (ri <= ci), 1.0, 0.0).astype(BF16)
    same01 = jnp.where(same, 1.0, 0.0).astype(BF16)
    dt, acs, tot, dtd, dt_t, acs_t, dtd_t = _ssd_small(
        dt_ref[...], dtb_ref[...], a_ref[...], dtbc_ref[...], ac_ref[...], low01, up01, same01, same01)

    def write_y(p, y_pair):
        ybuf[:, p * LANES:(p + 1) * LANES] = y_pair

    _ssd_pairs(x, bm, cm, acs, acs_t, dt_t, dtd_t, mask, None, write_y, None)

    e01 = e01_ref[...]
    ypart_ref[...] = ybuf[...] + dsk_ref[...] * x
    ea_ref[...] = _expand_heads(jnp.exp(acs), e01)
    xd = x * _expand_heads(dtd, e01)
    for p in range(D_MODEL // LANES):
        xdt_ref[p * LANES:(p + 1) * LANES, :] = xd[:, p * LANES:(p + 1) * LANES].T.astype(BF16)
    bm_ref[...] = bm.astype(BF16)
    cm_ref[...] = cm.astype(BF16)
    cd_ref[...] = jnp.exp(tot)


def _ssd_sample_rows(zx, dt, hist, cw, cb, dtb, a_row, dtb_col, a_col, dsk, e01, *, tdec):
    t = zx.shape[0]
    q = SSD_CHUNK
    vec = lambda n: pl.BlockSpec((1, n), lambda i: (0, 0))
    col = pl.BlockSpec((LANES, 1), lambda i: (0, 0))
    kern = functools.partial(_ssd_sample_rows_kernel, tdec=tdec)
    return pl.pallas_call(
        kern,
        grid=(t // q,),
        in_specs=[
            pl.BlockSpec((q, D_MODEL), lambda i: (i, 1)),
            pl.BlockSpec((q, D_MODEL), lambda i: (i, 2)),
            pl.BlockSpec((q, LANES), lambda i: (i, 0)),
            pl.BlockSpec((q, SSD_CONV_DIM), lambda i: (i, 0)),
            pl.BlockSpec((SSD_CONV, SSD_CONV_DIM), lambda i: (0, 0)),
            vec(SSD_CONV_DIM), vec(LANES), vec(LANES), col, col, vec(D_MODEL),
            pl.BlockSpec((LANES, D_MODEL), lambda i: (0, 0)),
        ],
        out_specs=[
            pl.BlockSpec((q, D_MODEL), lambda i: (i, 0)),
            pl.BlockSpec((q, D_MODEL), lambda i: (i, 0)),
            pl.BlockSpec((D_MODEL, q), lambda i: (0, i)),
            pl.BlockSpec((q, 512), lambda i: (i, 0)),
            pl.BlockSpec((q, 512), lambda i: (i, 0)),
            pl.BlockSpec((q, LANES), lambda i: (i, 0)),
        ],
        out_shape=[
            jax.ShapeDtypeStruct((t, D_MODEL), F32),
            jax.ShapeDtypeStruct((t, D_MODEL), F32),
            jax.ShapeDtypeStruct((D_MODEL, t), BF16),
            jax.ShapeDtypeStruct((t, 512), BF16),
            jax.ShapeDtypeStruct((t, 512), BF16),
            jax.ShapeDtypeStruct((t, LANES), F32),
        ],
        scratch_shapes=[pltpu.VMEM((q, D_MODEL), F32)],
        compiler_params=_cparams(("parallel",)),
        name="ssd_sample_rows",
    )(zx, zx, dt, hist, cw, cb, dtb, a_row, dtb_col, a_col, dsk, e01)


def _ssd_sample_state_kernel(cd_ref, z_ref, ypart_ref, ea_ref, xdt_ref, bm_ref, cm_ref, st_ref, nw_ref,
                             y_ref, sto_ref, *, tdec, gseq):
    i = pl.program_id(0)
    q = SSD_CHUNK
    rows = gseq * tdec
    steps_per_blk = q // rows
    lane = lax.broadcasted_iota(jnp.int32, (1, q), 1)
    rg = 16
    rowi = lax.broadcasted_iota(jnp.int32, (rg, 1), 0)
    lane_base = (i % steps_per_blk) * rows
    seq_per_rg = rg // tdec
    yoff_groups = []
    for q8 in range(rows // rg):
        yoff8 = jnp.zeros((rg, D_MODEL), F32)
        for sj in range(seq_per_rg):
            jj = q8 * seq_per_rg + sj
            b = i * gseq + jj
            lmask = (lane >= lane_base + jj * tdec) & (lane < lane_base + (jj + 1) * tdec)
            parts = []
            for g in range(SSD_N_GROUPS):
                s_g = st_ref[jj, g * 256:(g + 1) * 256, :]
                c8 = cm_ref[q8 * rg:(q8 + 1) * rg, g * SSD_D_STATE:(g + 1) * SSD_D_STATE]
                parts.append(_dot_nt(c8, s_g.astype(BF16)))
                lhs = jnp.where(lmask, xdt_ref[g * 256:(g + 1) * 256, :], jnp.zeros((), BF16))
                add = _dot(lhs, bm_ref[:, g * SSD_D_STATE:(g + 1) * SSD_D_STATE])
                for hh in range(4):
                    h = 4 * g + hh
                    sl = slice(hh * SSD_HEAD_DIM, (hh + 1) * SSD_HEAD_DIM)
                    sto_ref[jj, g * 256 + hh * 64:g * 256 + (hh + 1) * 64, :] = (
                        s_g[sl, :] * cd_ref[b, h] + add[sl, :])
            yo = jnp.concatenate(parts, axis=1)
            in_seq = (rowi >= sj * tdec) & (rowi < (sj + 1) * tdec)
            yoff8 = jnp.where(in_seq, yo, yoff8)
        yoff_groups.append(yoff8)
    yoff = jnp.concatenate(yoff_groups, axis=0) if len(yoff_groups) > 1 else yoff_groups[0]
    y = (ypart_ref[...] + yoff * ea_ref[...]) * _silu(z_ref[...])
    y_ref[...] = _rms(y, nw_ref[...])


def _ssd_sample_state(cd, zx, ypart, ea, xdt, bmb, cmb, state, nw, *, tdec, gseq):
    t = zx.shape[0]
    nseq = t // tdec
    q = SSD_CHUNK
    rows = gseq * tdec
    spb = q // rows
    kern = functools.partial(_ssd_sample_state_kernel, tdec=tdec, gseq=gseq)
    return pl.pallas_call(
        kern,
        grid=(nseq // gseq,),
        in_specs=[
            pl.BlockSpec(memory_space=pltpu.SMEM),
            pl.BlockSpec((rows, D_MODEL), lambda i: (i, 0)),
            pl.BlockSpec((rows, D_MODEL), lambda i: (i, 0)),
            pl.BlockSpec((rows, D_MODEL), lambda i: (i, 0)),
            pl.BlockSpec((D_MODEL, q), lambda i: (0, i // spb)),
            pl.BlockSpec((q, 512), lambda i: (i // spb, 0)),
            pl.BlockSpec((rows, 512), lambda i: (i, 0)),
            pl.BlockSpec((gseq, D_MODEL, SSD_D_STATE), lambda i: (i, 0, 0)),
            pl.BlockSpec((1, D_MODEL), lambda i: (0, 0)),
        ],
        out_specs=[
            pl.BlockSpec((rows, D_MODEL), lambda i: (i, 0)),
            pl.BlockSpec((gseq, D_MODEL, SSD_D_STATE), lambda i: (i, 0, 0)),
        ],
        out_shape=[
            jax.ShapeDtypeStruct((t, D_MODEL), F32),
            jax.ShapeDtypeStruct((nseq, D_MODEL, SSD_D_STATE), F32),
        ],
        compiler_params=_cparams(("parallel",)),
        name="ssd_sample_state",
    )(cd, zx, ypart, ea, xdt, bmb, cmb, state, nw)


def _lse_lane(h):
    return 16 * h


def _attn_prompt_kernel(q_ref, kc_ref, kp_ref, vc_ref, vp_ref, o_ref, lse_ref, *, nq):
    n = pl.program_id(2)
    nk = 128
    k = jnp.concatenate([kp_ref[...], kc_ref[...]], axis=0)
    v = jnp.concatenate([vp_ref[...], vc_ref[...]], axis=0)
    lane = lax.broadcasted_iota(jnp.int32, (1, LANES), 1)
    lo = lane < ATT_HEAD_DIM
    qi = lax.broadcasted_iota(jnp.int32, (2 * nk, 2 * nk), 0) % nk
    kj = lax.broadcasted_iota(jnp.int32, (2 * nk, 2 * nk), 1)
    band = (kj >= qi) & (kj <= qi + nk)
    band_first = band & ((kj >= nk) | (n > 0))
    q = q_ref[...]
    zero = jnp.zeros((), q.dtype)
    npairs = ATT_HEADS // 2
    pair_lanes = [slice(p * LANES, (p + 1) * LANES) for p in range(npairs)]
    units = [(j, p) for j in range(nq) for p in range(npairs)]

    def scores(u):
        j, p = u
        qp = q[j * nk:(j + 1) * nk, pair_lanes[p]]
        q2 = jnp.concatenate([jnp.where(lo, qp, zero), jnp.where(lo, zero, qp)], axis=0)
        s = _dot_nt(q2, k[j * nk:(j + 2) * nk, pair_lanes[p]])
        return jnp.where(band_first if j == 0 else band, s, NEG_INF)

    def softmax(s):
        m = jnp.max(s, axis=-1, keepdims=True)
        e = jnp.exp(s - m)
        den = jnp.sum(e, axis=-1, keepdims=True)
        return e.astype(BF16), den, m + jnp.log(den)

    ahead = 3
    s_next = [scores(u) for u in units[:ahead]]
    lse_full, o_parts = None, []
    for i, (j, p) in enumerate(units):
        e, den, lse = softmax(s_next.pop(0))
        if i + ahead < len(units):
            s_next.append(scores(units[i + ahead]))
        r = _dot(e, v[j * nk:(j + 2) * nk, pair_lanes[p]]) / den
        o_parts.append(jnp.where(lo, r[0:nk], r[nk:2 * nk]).astype(o_ref.dtype))
        if p == 0:
            lse_full = jnp.zeros((nk, LANES), F32)
        for hh in range(2):
            base = _lse_lane(2 * p + hh)
            lse_full = jnp.where((lane >= base) & (lane < base + 16), lse[hh * nk:(hh + 1) * nk], lse_full)
        if p == npairs - 1:
            o_ref[j * nk:(j + 1) * nk, :] = jnp.concatenate(o_parts, axis=1)
            lse_ref[j * nk:(j + 1) * nk, :] = lse_full
            o_parts = []


def _attn_prompt(qkv, *, d):
    nb, _, length, _ = qkv.shape
    nk = 128
    nq = 2 if length % (2 * nk) == 0 else 1
    nblk = length // (nq * nk)
    cur = lambda which: pl.BlockSpec((None, None, nq * nk, COL_TILE), lambda b, r, n: (b, r, n, which))
    prev = lambda which: pl.BlockSpec((None, None, nk, COL_TILE),
                                      lambda b, r, n: (b, r, jnp.maximum(nq * n - 1, 0), which))
    return pl.pallas_call(
        functools.partial(_attn_prompt_kernel, nq=nq),
        grid=(nb, d, nblk),
        in_specs=[cur(0), cur(1), prev(1), cur(2), prev(2)],
        out_specs=[
            pl.BlockSpec((None, None, nq * nk, ATT_WIDTH), lambda b, r, n: (b, r, n, 0)),
            pl.BlockSpec((None, None, nq * nk, LANES), lambda b, r, n: (b, r, n, 0)),
        ],
        out_shape=[
            jax.ShapeDtypeStruct((nb, d, length, ATT_WIDTH), BF16),
            jax.ShapeDtypeStruct((nb, d, length, LANES), F32),
        ],
        compiler_params=_cparams(("parallel", "parallel", "arbitrary")),
        name=f"attn_prompt_d{d}",
    )(qkv, qkv, qkv, qkv, qkv)


def _attn_sample_kernel(q0_ref, q1_ref, q2_ref, qm_ref, c0_ref, c1_ref, c2_ref, cm_ref, oa_ref, om_ref, *, tdec):
    nh = ATT_HEADS
    rows8 = 8
    j = pl.program_id(1)
    r64 = nh * rows8
    rowi = lax.broadcasted_iota(jnp.int32, (r64, 1), 0)
    head_of_row = rowi // rows8
    r8 = rowi % rows8
    seq_q = r8 // tdec
    t_q = r8 % tdec
    lane512 = lax.broadcasted_iota(jnp.int32, (1, ATT_WIDTH), 1)
    headmask = (lane512 // ATT_HEAD_DIM) == head_of_row
    col8 = lax.broadcasted_iota(jnp.int32, (1, rows8), 1)
    seq_k = col8 // tdec
    t_k = col8 % tdec

    outs, lses = [], []
    for g, (q_ref, c_ref, d) in enumerate(zip((q0_ref, q1_ref, q2_ref), (c0_ref, c1_ref, c2_ref), ATT_DILATIONS)):
        w = c_ref.shape[-1]
        qkv = q_ref[...]
        qv = qkv[:, :ATT_WIDTH]
        kn = qkv[:, ATT_WIDTH:2 * ATT_WIDTH]
        vn = qkv[:, 2 * ATT_WIDTH:]
        q64 = jnp.where(headmask, jnp.concatenate([qv] * nh, axis=0), 0.0)
        s_new = _dot_nt(q64, kn)
        if d == 1:
            valid_new = (seq_k == seq_q) & (t_k <= t_q)
        else:
            valid_new = (seq_k == seq_q) & (t_k == t_q)
        s_new = jnp.where(valid_new, s_new, NEG_INF)
        wi = lax.broadcasted_iota(jnp.int32, (1, w), 1)
        valid_c = (wi >= t_q) if d == 1 else ((wi % d) == t_q)
        s_c = jnp.where(valid_c, _dot(q64.astype(BF16), c_ref[0].astype(BF16)), NEG_INF)
        m = jnp.maximum(jnp.max(s_c, axis=-1, keepdims=True), jnp.max(s_new, axis=-1, keepdims=True))
        e_c = jnp.exp(s_c - m)
        e_n = jnp.exp(s_new - m)
        den = jnp.sum(e_c, axis=-1, keepdims=True) + jnp.sum(e_n, axis=-1, keepdims=True)
        o = _dot(e_n, vn) + _dot_nt(e_c.astype(BF16), c_ref[1].astype(BF16))
        outs.append(o / den)
        lses.append(m + jnp.log(den))

    mx = jnp.maximum(jnp.maximum(lses[0], lses[1]), lses[2])
    ws = [jnp.exp(l - mx) for l in lses]
    om = (ws[0] * outs[0] + ws[1] * outs[1] + ws[2] * outs[2]) / (ws[0] + ws[1] + ws[2])
    om = jnp.where(headmask, om, 0.0)
    o8 = om[0:rows8]
    for h in range(1, nh):
        o8 = o8 + om[h * rows8:(h + 1) * rows8]

    qm = jnp.concatenate([qm_ref[...] * (MEM_HEAD_DIM ** -0.5), jnp.zeros((rows8, MEM_WIDTH), F32)], axis=0)
    mparts = []
    for h in range(MEM_HEADS):
        kh = cm_ref[pl.ds(h, MEM_LEN, stride=2 * MEM_HEADS), :].astype(BF16)
        vh = cm_ref[pl.ds(MEM_HEADS + h, MEM_LEN, stride=2 * MEM_HEADS), :].astype(BF16)
        s = _dot_nt(qm[:, h * MEM_HEAD_DIM:(h + 1) * MEM_HEAD_DIM].astype(BF16), kh)
        mm = jnp.max(s, axis=-1, keepdims=True)
        e = jnp.exp(s - mm)
        den = jnp.sum(e, axis=-1, keepdims=True)
        mparts.append((_dot(e.astype(BF16), vh) / den)[0:rows8])
    m8 = jnp.concatenate(mparts, axis=1)

    row8 = lax.broadcasted_iota(jnp.int32, (rows8, 1), 0)
    mine = (row8 // tdec) == j

    @pl.when(j == 0)
    def _():
        oa_ref[...] = jnp.where(mine, o8, 0.0)
        om_ref[...] = jnp.where(mine, m8, 0.0)

    @pl.when(j > 0)
    def _():
        oa_ref[...] = jnp.where(mine, o8, oa_ref[...])
        om_ref[...] = jnp.where(mine, m8, om_ref[...])


def _attn_sample(q0, q1, q2, rest, c0, c1, c2, cmem, *, tdec):
    t = q0.shape[0]
    spg = 8 // tdec
    row_spec = pl.BlockSpec((8, 3 * COL_TILE), lambda i, j: (i, 0))
    cache_spec = lambda c: pl.BlockSpec((None,) + c.shape[1:], lambda i, j: (i * spg + j,) + (0,) * (c.ndim - 1))
    kern = functools.partial(_attn_sample_kernel, tdec=tdec)
    return pl.pallas_call(
        kern,
        grid=(t // 8, spg),
        in_specs=[
            row_spec, row_spec, row_spec,
            pl.BlockSpec((8, COL_TILE), lambda i, j: (i, 1)),
            cache_spec(c0), cache_spec(c1), cache_spec(c2), cache_spec(cmem),
        ],
        out_specs=[
            pl.BlockSpec((8, ATT_WIDTH), lambda i, j: (i, 0)),
            pl.BlockSpec((8, MEM_WIDTH), lambda i, j: (i, 0)),
        ],
        out_shape=[
            jax.ShapeDtypeStruct((t, ATT_WIDTH), F32),
            jax.ShapeDtypeStruct((t, MEM_WIDTH), F32),
        ],
        compiler_params=_cparams(("parallel", "arbitrary")),
        name="attn_sample",
    )(q0, q1, q2, rest, c0, c1, c2, cmem)


def _final_tail(x, y_ssd, y_att, y_mem, gate_raw, wssd_ref, watt_ref, wmem_ref, wout_ref, nf_ref, y_ref):
    gates = _sigmoid(gate_raw)
    merged = (gates[:, 0:D_MODEL] * _dot(y_ssd.astype(BF16), wssd_ref[...])
              + gates[:, D_MODEL:2 * D_MODEL] * _dot(y_att.astype(BF16), watt_ref[...])
              + gates[:, 2 * D_MODEL:] * _dot(y_mem.astype(BF16), wmem_ref[...]))
    y_ref[...] = _rms(x + _dot(merged.astype(BF16), wout_ref[...]), nf_ref[...])


def _final_prompt_kernel(x_ref, ys_ref, o0_ref, o1_ref, o2_ref, l0_ref, l1_ref, l2_ref, rest_ref, mkv_ref,
                         wssd_ref, watt_ref, wmem_ref, wout_ref, nf_ref, e16_ref,
                         y_ref, obuf, lbuf, *, tm, dils):
    nslab = ATT_WIDTH // LANES
    o_nat, l_nat = [], []
    for g, (o_ref, l_ref, d) in enumerate(zip((o0_ref, o1_ref, o2_ref), (l0_ref, l1_ref, l2_ref), dils)):
        if d == 1:
            o_nat.append(o_ref[0].astype(F32))
            l_nat.append(l_ref[0])
            continue
        for r in range(d):
            blk = o_ref[r].astype(F32)
            for c in range(nslab):
                obuf[g, c, pl.ds(r, tm // d, stride=d), :] = blk[:, c * LANES:(c + 1) * LANES]
            lbuf[g, pl.ds(r, tm // d, stride=d), :] = l_ref[r]
        o_nat.append(jnp.concatenate([obuf[g, c] for c in range(nslab)], axis=1))
        l_nat.append(lbuf[g])
    mx = jnp.maximum(jnp.maximum(l_nat[0], l_nat[1]), l_nat[2])
    ws = [jnp.exp(l - mx) for l in l_nat]
    wsum = ws[0] + ws[1] + ws[2]
    e16 = e16_ref[...]
    o_att = jnp.zeros((tm, ATT_WIDTH), F32)
    for g in range(3):
        hi, mid, _ = _split3(ws[g] / wsum)
        o_att = o_att + _dot(jnp.concatenate([hi, mid], axis=1), e16) * o_nat[g]

    rest = rest_ref[...]
    z_att = rest[:, 0:512].astype(F32)
    q_mem = rest[:, 512:1024]
    z_mem = rest[:, 1024:1536].astype(F32)
    gate_raw = rest[:, 1536:].astype(F32)
    y_att = o_att * _silu(z_att)

    outs = []
    for h in range(MEM_HEADS):
        lanes = slice(h * MEM_HEAD_DIM, (h + 1) * MEM_HEAD_DIM)
        kh = mkv_ref[:, h * MEM_HEAD_DIM:(h + 1) * MEM_HEAD_DIM]
        vh = mkv_ref[:, MEM_WIDTH + h * MEM_HEAD_DIM:MEM_WIDTH + (h + 1) * MEM_HEAD_DIM]
        s = _dot_nt(q_mem[:, lanes], kh) * (MEM_HEAD_DIM ** -0.5)
        m = jnp.max(s, axis=-1, keepdims=True)
        e = jnp.exp(s - m)
        den = jnp.sum(e, axis=-1, keepdims=True)
        outs.append(_dot(e.astype(BF16), vh) / den)
    y_mem = jnp.concatenate(outs, axis=1) * _silu(z_mem)

    _final_tail(x_ref[...], ys_ref[...], y_att, y_mem, gate_raw,
                wssd_ref, watt_ref, wmem_ref, wout_ref, nf_ref, y_ref)


def _final_prompt(x2d, y_ssd, o_g, l_g, rest, mkv_bf, wssd, watt, wmem, wout, nf, e16, *, seq, tm, dils):
    t = x2d.shape[0]
    tps = seq // tm

    def res_spec(d, width):
        return pl.BlockSpec((None, d, tm // d, width), lambda i: (i // tps, 0, i % tps, 0))

    full = lambda a: pl.BlockSpec(a.shape, lambda i: (0,) * a.ndim)
    kern = functools.partial(_final_prompt_kernel, tm=tm, dils=dils)
    return pl.pallas_call(
        kern,
        grid=(t // tm,),
        in_specs=[
            pl.BlockSpec((tm, D_MODEL), lambda i: (i, 0)),
            pl.BlockSpec((tm, D_MODEL), lambda i: (i, 0)),
            res_spec(dils[0], ATT_WIDTH), res_spec(dils[1], ATT_WIDTH), res_spec(dils[2], ATT_WIDTH),
            res_spec(dils[0], LANES), res_spec(dils[1], LANES), res_spec(dils[2], LANES),
            pl.BlockSpec((tm, rest.shape[1]), lambda i: (i, 0)),
            pl.BlockSpec((None, MEM_LEN, 2 * MEM_WIDTH), lambda i: (i // tps, 0, 0)),
            full(wssd), full(watt), full(wmem), full(wout), full(nf), full(e16),
        ],
        out_specs=pl.BlockSpec((tm, D_MODEL), lambda i: (i, 0)),
        out_shape=jax.ShapeDtypeStruct((t, D_MODEL), F32),
        scratch_shapes=[
            pltpu.VMEM((3, ATT_WIDTH // LANES, tm, LANES), F32),
            pltpu.VMEM((3, tm, LANES), F32),
        ],
        compiler_params=_cparams(("parallel",)),
        name="final_prompt",
    )(x2d, y_ssd, *o_g, *l_g, rest, mkv_bf, wssd, watt, wmem, wout, nf, e16)


def _final_sample_kernel(x_ref, ys_ref, oa_ref, om_ref, rest_ref,
                         wssd_ref, watt_ref, wmem_ref, wout_ref, nf_ref, y_ref):
    rest = rest_ref[...]
    y_att = oa_ref[...] * _silu(rest[:, 0:512])
    y_mem = om_ref[...] * _silu(rest[:, 1024:1536])
    _final_tail(x_ref[...], ys_ref[...], y_att, y_mem, rest[:, 1536:],
                wssd_ref, watt_ref, wmem_ref, wout_ref, nf_ref, y_ref)


def _final_sample(x2d, y_ssd, o_att, o_mem, rest, wssd, watt, wmem, wout, nf, *, tm):
    t = x2d.shape[0]
    full = lambda a: pl.BlockSpec(a.shape, lambda i: (0,) * a.ndim)
    rows = lambda w: pl.BlockSpec((tm, w), lambda i: (i, 0))
    return pl.pallas_call(
        _final_sample_kernel,
        grid=(t // tm,),
        in_specs=[rows(D_MODEL), rows(D_MODEL), rows(ATT_WIDTH), rows(MEM_WIDTH), rows(rest.shape[1]),
                  full(wssd), full(watt), full(wmem), full(wout), full(nf)],
        out_specs=rows(D_MODEL),
        out_shape=jax.ShapeDtypeStruct((t, D_MODEL), F32),
        compiler_params=_cparams(("parallel",)),
        name="final_sample",
    )(x2d, y_ssd, o_att, o_mem, rest, wssd, watt, wmem, wout, nf)


def kernel(x_prompt, x_sample, mem_prompt, cache_win128_kv, cache_win512_kv, cache_win2048_kv, cache_mem_kv,
           state_conv, state_ssm, norm_in_w, w_in, conv_w, conv_b, dt_bias, a_log, d_skip, ssd_norm_w,
           mem_norm_w, w_mem_kv, w_br_ssd, w_br_att, w_br_mem, w_out, norm_f_w):
    bp, seq, _ = x_prompt.shape
    bs, tdec, _ = x_sample.shape
    dils = ATT_DILATIONS

    row = lambda v: v.reshape(1, -1).astype(F32)
    dt_col0 = D_MODEL + SSD_CONV_DIM
    w_main = jnp.concatenate([w_in[:, :dt_col0], w_in[:, dt_col0 + SSD_N_HEADS:]], axis=1).astype(BF16)
    w_dt = jnp.pad(w_in[:, dt_col0:dt_col0 + SSD_N_HEADS], ((0, 0), (0, LANES - SSD_N_HEADS))).astype(BF16)
    pad_heads = lambda v: jnp.pad(v.astype(F32), (0, LANES - SSD_N_HEADS)).reshape(1, LANES)
    dtb = pad_heads(dt_bias)
    a_row = jnp.pad(-jnp.exp(a_log.astype(F32)), (0, LANES - SSD_N_HEADS)).reshape(1, LANES)
    dtb_col, a_col = dtb.reshape(LANES, 1), a_row.reshape(LANES, 1)
    dsk = jnp.repeat(d_skip.astype(F32), SSD_HEAD_DIM).reshape(1, D_MODEL)
    cw = conv_w.astype(F32)
    cb = row(conv_b)
    nin, nssd, nmem, nf = row(norm_in_w), row(ssd_norm_w), row(mem_norm_w), row(norm_f_w)
    wssd, watt, wmem, wout = (w.astype(BF16) for w in (w_br_ssd, w_br_att, w_br_mem, w_out))
    lane = jnp.arange(LANES)
    e_ssd = (lane[:, None] == (jnp.arange(D_MODEL)[None, :] // SSD_HEAD_DIM)).astype(BF16)
    lse_lane_of_ch = jnp.repeat(jnp.array([_lse_lane(h) for h in range(ATT_HEADS)]), ATT_HEAD_DIM)
    e_att = (lane[:, None] == lse_lane_of_ch[None, :]).astype(BF16)
    e_att = jnp.concatenate([e_att, e_att], axis=0)

    xp2 = x_prompt.reshape(bp * seq, D_MODEL)
    mkv_f32, mkv_bf = _mem_kv(mem_prompt.reshape(bp * MEM_LEN, D_MODEL), nmem, w_mem_kv.astype(BF16))
    tm_p = 256
    tabs_p = _rope_tables(jnp.arange(seq))
    windows = tuple(min(w, seq) for w in ATT_WINDOWS)
    zx, q0, q1, q2, rest, dtr, kv0, kv1, kv2 = _in_proj(
        xp2, nin, w_main, w_dt, tabs_p, seq=seq, tm=tm_p, dils=dils, out_dtype=BF16, windows=windows)
    y_ssd, p_ssm = _ssd_prompt(zx, dtr, cw, cb, dtb, a_row, dtb_col, a_col, dsk, nssd, nbatch=bp, seq=seq)
    o_g, l_g = [], []
    for qkv, d in zip((q0, q1, q2), dils):
        o, l = _attn_prompt(qkv, d=d)
        o_g.append(o)
        l_g.append(l)
    y_prompt = _final_prompt(xp2, y_ssd, o_g, l_g, rest, mkv_bf.reshape(bp, MEM_LEN, 2 * MEM_WIDTH),
                             wssd, watt, wmem, wout, nf, e_att, seq=seq, tm=256, dils=dils)
    y_prompt = y_prompt.reshape(bp, seq, D_MODEL)

    p_win = [kv.reshape(bp, 2, ATT_HEADS, ATT_HEAD_DIM, kv.shape[-1]).transpose(0, 4, 1, 2, 3)
             for kv in (kv0, kv1, kv2)]
    p_mem_kv = mkv_f32.reshape(bp, MEM_LEN, 2, MEM_HEADS, MEM_HEAD_DIM)
    p_conv = zx.reshape(bp, seq, -1)[:, seq - (SSD_CONV - 1):, D_MODEL:].astype(F32)
    p_ssm = p_ssm.reshape(bp, SSD_N_HEADS, SSD_HEAD_DIM, SSD_D_STATE)

    ts = bs * tdec
    pos_s = PAST_LEN + jnp.tile(jnp.arange(tdec), bs)
    xs2 = x_sample.reshape(ts, D_MODEL)
    tabs_s = _rope_tables(pos_s)
    zx_s, q0_s, q1_s, q2_s, rest_s, dt_s = _in_proj(
        xs2, nin, w_main, w_dt, tabs_s, seq=ts, tm=min(ts, 128), dils=(1, 1, 1), out_dtype=F32)
    q_s = [a.reshape(ts, 3 * COL_TILE) for a in (q0_s, q1_s, q2_s)]
    hist = jnp.pad(state_conv.astype(F32), ((0, 0), (tdec - (SSD_CONV - 1), 0), (0, 0))).reshape(ts, SSD_CONV_DIM)
    ypart, ea, xdt, bmb, cmb, cd = _ssd_sample_rows(
        zx_s, dt_s, hist, cw, cb, dtb, a_row, dtb_col, a_col, dsk, e_ssd, tdec=tdec)
    cd_seq = cd.reshape(bs, tdec, LANES)[:, 0, :SSD_N_HEADS]
    y_ssd_s, s_ssm = _ssd_sample_state(cd_seq, zx_s, ypart, ea, xdt, bmb, cmb,
                                       state_ssm.reshape(bs, D_MODEL, SSD_D_STATE), nssd, tdec=tdec, gseq=8)
    to_kt = lambda c: c.transpose(0, 2, 3, 4, 1).reshape(bs, 2, ATT_WIDTH, c.shape[1])
    c0, c1, c2 = to_kt(cache_win128_kv), to_kt(cache_win512_kv), to_kt(cache_win2048_kv)
    cmem = cache_mem_kv.reshape(bs, MEM_LEN * 2 * MEM_HEADS, MEM_HEAD_DIM)
    o_att_s, o_mem_s = _attn_sample(*q_s, rest_s, c0, c1, c2, cmem, tdec=tdec)
    y_sample = _final_sample(xs2, y_ssd_s, o_att_s, o_mem_s, rest_s, wssd, watt, wmem, wout, nf, tm=min(ts, 256))
    y_sample = y_sample.reshape(bs, tdec, D_MODEL)

    s_win = [a[:, COL_TILE:].reshape(bs, tdec, 2, ATT_HEADS, ATT_HEAD_DIM) for a in q_s]
    s_conv = zx_s.reshape(bs, tdec, -1)[:, tdec - (SSD_CONV - 1):, D_MODEL:]
    s_ssm = s_ssm.reshape(bs, SSD_N_HEADS, SSD_HEAD_DIM, SSD_D_STATE)

    return (y_prompt, y_sample, p_win[0], p_win[1], p_win[2], p_mem_kv, p_conv, p_ssm,
            s_win[0], s_win[1], s_win[2], s_conv, s_ssm)
```

```python
import functools
import math

import jax
import jax.numpy as jnp
from jax import lax
from jax.experimental import pallas as pl
from jax.experimental.pallas import tpu as pltpu

F32 = jnp.float32
BF16 = jnp.bfloat16

D_MODEL = 1024
NORM_EPS = 1e-6
SSD_HEAD_DIM = 64
SSD_N_HEADS = 16
SSD_N_GROUPS = 4
SSD_D_STATE = 128
SSD_CONV = 4
SSD_CHUNK = 128
SSD_CONV_DIM = 2048
ATT_WINDOWS = (128, 512, 2048)
ATT_DILATIONS = (1, 4, 16)
ATT_HEADS = 8
ATT_HEAD_DIM = 64
ATT_WIDTH = 512
ROPE_DIM = 16
ROPE_THETA = 500000.0
MEM_LEN = 256
MEM_HEADS = 4
MEM_HEAD_DIM = 128
MEM_WIDTH = 512
NEG_INF = -1e30
PAST_LEN = 8192

LANES = 128
COL_TILE = 512
N_COL_TILES = 24
DT_COL0 = 3072
QKV_TILE0 = 6
REST_TILE0 = 15
VMEM_LIMIT = 56 * 1024 * 1024


def _cparams(sem):
    return pltpu.CompilerParams(dimension_semantics=sem, vmem_limit_bytes=VMEM_LIMIT)


def _dot(a, b):
    return jnp.dot(a, b, preferred_element_type=F32)


def _dot_nt(a, b):
    return lax.dot_general(a, b, (((1,), (1,)), ((), ())), preferred_element_type=F32)


def _rms(x, w):
    return x * lax.rsqrt(jnp.mean(x * x, axis=-1, keepdims=True) + NORM_EPS) * w


def _sigmoid(x):
    return 0.5 * jnp.tanh(0.5 * x) + 0.5


def _silu(x):
    u = 0.5 * x
    return u * jnp.tanh(u) + u


LOG2E = 1.4426950408889634
LN2 = 0.6931471805599453


def _softplus(x):
    return jnp.maximum(x, 0.0) + jnp.log(1.0 + jnp.exp(-jnp.abs(x)))


def _split3(x):
    hi = x.astype(BF16)
    r1 = x - hi.astype(F32)
    mid = r1.astype(BF16)
    lo = (r1 - mid.astype(F32)).astype(BF16)
    return hi, mid, lo


def _dot_exact_lhs01(m01, x):
    hi, mid, lo = _split3(x)
    return _dot(m01, hi) + _dot(m01, mid) + _dot(m01, lo)


def _dot_exact_rhs01(x, m01):
    hi, mid, lo = _split3(x)
    return _dot(hi, m01) + _dot(mid, m01) + _dot(lo, m01)


def _mem_kv_kernel(x_ref, nw_ref, w_ref, o_ref, ob_ref):
    h = _rms(x_ref[...], nw_ref[...]).astype(BF16)
    acc = _dot(h, w_ref[...])
    o_ref[...] = acc
    ob_ref[...] = acc.astype(BF16)


def _mem_kv(mem2d, nw, w_bf):
    m = mem2d.shape[0]
    tm = 256
    return pl.pallas_call(
        _mem_kv_kernel,
        grid=(m // tm,),
        in_specs=[
            pl.BlockSpec((tm, D_MODEL), lambda i: (i, 0)),
            pl.BlockSpec((1, D_MODEL), lambda i: (0, 0)),
            pl.BlockSpec((D_MODEL, 2 * MEM_WIDTH), lambda i: (0, 0)),
        ],
        out_specs=[
            pl.BlockSpec((tm, 2 * MEM_WIDTH), lambda i: (i, 0)),
            pl.BlockSpec((tm, 2 * MEM_WIDTH), lambda i: (i, 0)),
        ],
        out_shape=[
            jax.ShapeDtypeStruct((m, 2 * MEM_WIDTH), F32),
            jax.ShapeDtypeStruct((m, 2 * MEM_WIDTH), BF16),
        ],
        compiler_params=_cparams(("parallel",)),
        name="mem_kv",
    )(mem2d, nw, w_bf)


def _rope_tables(pos):
    half = ROPE_DIM // 2
    inv = jnp.power(ROPE_THETA, -jnp.arange(half, dtype=F32) * 2.0 / ROPE_DIM)
    ang = pos.astype(F32)[:, None] * inv[None, :]
    cos, sin = jnp.cos(ang), jnp.sin(ang)
    m = jnp.arange(LANES) % ATT_HEAD_DIM
    idx = m % half
    c = jnp.where(m[None, :] < ROPE_DIM, cos[:, idx], 1.0)
    s1 = jnp.where(m[None, :] < half, -sin[:, idx], 0.0)
    s2 = jnp.where((m[None, :] >= half) & (m[None, :] < ROPE_DIM), sin[:, idx], 0.0)
    return c.astype(F32), s1.astype(F32), s2.astype(F32)


def _in_proj_kernel(x_ref, nw_ref, wt_ref, c_ref, s1_ref, s2_ref,
                    zx_ref, q0_ref, q1_ref, q2_ref, rest_ref, dt_ref, *more_refs, tm, dils, win_rows):
    pw_refs = more_refs[:3] if win_rows else ()
    h_ref, acc_ref, tmp_ref = more_refs[-3:]
    nslab = COL_TILE // LANES
    h_ref[...] = _rms(x_ref[...], nw_ref[...]).astype(BF16)
    dt_ref[...] = _dot_nt(h_ref[...], wt_ref[DT_COL0:DT_COL0 + LANES, :])
    q_refs = (q0_ref, q1_ref, q2_ref)
    strided = 0
    for j in range(N_COL_TILES):
        col0 = j * COL_TILE + (SSD_N_HEADS if j >= QKV_TILE0 else 0)
        acc = _dot_nt(h_ref[...], wt_ref[col0:col0 + COL_TILE, :])
        if j < QKV_TILE0:
            zx_ref[:, j * COL_TILE:(j + 1) * COL_TILE] = acc.astype(zx_ref.dtype)
            continue
        if j >= REST_TILE0:
            rest_ref[:, (j - REST_TILE0) * COL_TILE:(j - REST_TILE0 + 1) * COL_TILE] = acc.astype(rest_ref.dtype)
            continue
        g, kind = divmod(j - QKV_TILE0, 3)
        out_ref, d = q_refs[g], dils[g]
        for c in range(nslab):
            a = acc[:, c * LANES:(c + 1) * LANES]
            if kind < 2:
                a = (a * c_ref[...] + pltpu.roll(a, LANES - ROPE_DIM // 2, 1) * s1_ref[...]
                     + pltpu.roll(a, ROPE_DIM // 2, 1) * s2_ref[...])
            if kind == 0:
                a = a * (ATT_HEAD_DIM ** -0.5 * LOG2E)
            lanes = slice(kind * COL_TILE + c * LANES, kind * COL_TILE + (c + 1) * LANES)
            if win_rows and kind > 0:
                ch0 = (kind - 1) * COL_TILE + c * LANES
                pw_refs[g][ch0:ch0 + LANES, :] = a[tm - win_rows[g]:, :].T
            if d == 1:
                out_ref[0, :, lanes] = a.astype(out_ref.dtype)
            elif d == 4:
                acc_ref[strided, c] = a
                for r in range(4):
                    out_ref[r, :, lanes] = acc_ref[strided, c, pl.ds(r, tm // 4, stride=4), :].astype(out_ref.dtype)
            else:
                acc_ref[strided, c] = a
                q4 = tm // 4
                for r1 in range(4):
                    tmp_ref[kind, c, r1 * q4:(r1 + 1) * q4, :] = acc_ref[strided, c, pl.ds(r1, q4, stride=4), :]
                for r1 in range(4):
                    for r2 in range(4):
                        out_ref[r1 + 4 * r2, :, lanes] = tmp_ref[
                            kind, c, pl.ds(r1 * q4 + r2, tm // 16, stride=4), :].astype(out_ref.dtype)
        if d > 1:
            strided += 1


def _in_proj(x2d, nw, wt, tabs, *, seq, tm, dils, out_dtype, windows=None):
    t = x2d.shape[0]
    nb = t // seq
    tiles_per_seq = seq // tm
    tab_blocks = tabs[0].shape[0] // tm
    n_strided = 3 * sum(1 for d in dils if d > 1)
    nslab = COL_TILE // LANES

    def qkv_spec(d):
        return pl.BlockSpec((None, d, tm // d, 3 * COL_TILE),
                            lambda i: (i // tiles_per_seq, 0, i % tiles_per_seq, 0))

    win_rows, win_specs, win_shapes = None, [], []
    if windows:
        win_rows = tuple(min(tm, w) for w in windows)
        for w, wr in zip(windows, win_rows):
            first = tiles_per_seq - w // wr
            win_specs.append(pl.BlockSpec(
                (None, 2 * COL_TILE, wr),
                lambda i, first=first: (i // tiles_per_seq, 0, jnp.maximum(i % tiles_per_seq - first, 0))))
            win_shapes.append(jax.ShapeDtypeStruct((nb, 2 * COL_TILE, w), F32))

    tab_spec = pl.BlockSpec((tm, LANES), lambda i: (i % tab_blocks, 0))
    resident = lambda shape: pl.BlockSpec(shape, lambda i: (0, 0), pipeline_mode=pl.Buffered(1))
    kern = functools.partial(_in_proj_kernel, tm=tm, dils=dils, win_rows=win_rows)
    n_zx, n_rest = QKV_TILE0 * COL_TILE, (N_COL_TILES - REST_TILE0) * COL_TILE
    return pl.pallas_call(
        kern,
        grid=(t // tm,),
        in_specs=[
            pl.BlockSpec((tm, D_MODEL), lambda i: (i, 0)),
            resident((1, D_MODEL)),
            resident(wt.shape),
            tab_spec, tab_spec, tab_spec,
        ],
        out_specs=[
            pl.BlockSpec((tm, n_zx), lambda i: (i, 0)),
            qkv_spec(dils[0]), qkv_spec(dils[1]), qkv_spec(dils[2]),
            pl.BlockSpec((tm, n_rest), lambda i: (i, 0)),
            pl.BlockSpec((tm, LANES), lambda i: (i, 0)),
        ] + win_specs,
        out_shape=[
            jax.ShapeDtypeStruct((t, n_zx), out_dtype),
            jax.ShapeDtypeStruct((nb, dils[0], seq // dils[0], 3 * COL_TILE), out_dtype),
            jax.ShapeDtypeStruct((nb, dils[1], seq // dils[1], 3 * COL_TILE), out_dtype),
            jax.ShapeDtypeStruct((nb, dils[2], seq // dils[2], 3 * COL_TILE), out_dtype),
            jax.ShapeDtypeStruct((t, n_rest), out_dtype),
            jax.ShapeDtypeStruct((t, LANES), F32),
        ] + win_shapes,
        scratch_shapes=[
            pltpu.VMEM((tm, D_MODEL), BF16),
            pltpu.VMEM((max(n_strided, 1), nslab, tm, LANES), F32),
            pltpu.VMEM((3, nslab, tm, LANES), F32),
        ],
        compiler_params=_cparams(("arbitrary",)),
        name="in_proj",
    )(x2d, nw, wt, *tabs)


def _ssd_small(dt_raw, dtb_row, a_row, dtb_col, a_col, same01, same01_t, tot01, tot01_t):
    dt = _softplus(dt_raw + dtb_row)
    dta = dt * a_row
    acs = _dot_exact_lhs01(same01, dta)
    tot = _dot_exact_lhs01(tot01, dta)
    dtd = dt * jnp.exp2(tot - acs)
    dt_raw_t = dt_raw.T
    dt_t = _softplus(dt_raw_t + dtb_col)
    dta_t = dt_t * a_col
    acs_t = _dot_exact_rhs01(dta_t, same01_t)
    tot_t = _dot_exact_rhs01(dta_t, tot01_t)
    dtd_t = dt_t * jnp.exp2(tot_t - acs_t)
    return dt, acs, tot, dtd, dt_t, acs_t, dtd_t


def _ssd_pairs(x, bm, cm, acs, acs_t, dt_t, dtd_t, mask, st_ref, y_ref_write, st_scale_row):
    lane = lax.broadcasted_iota(jnp.int32, (1, LANES), 1)
    half = [lane < ATT_HEAD_DIM, lane >= ATT_HEAD_DIM]
    bm_t = [bm[:, g * SSD_D_STATE:(g + 1) * SSD_D_STATE].T for g in range(SSD_N_GROUPS)]
    cms = [cm[:, g * SSD_D_STATE:(g + 1) * SSD_D_STATE] for g in range(SSD_N_GROUPS)]
    cb = [_dot(cms[g].astype(BF16), bm_t[g].astype(BF16)) for g in range(SSD_N_GROUPS)]
    for p in range(SSD_N_HEADS // 2):
        g = p // 2
        xp = x[:, p * LANES:(p + 1) * LANES]
        if st_ref is not None:
            stp = st_ref[:, p * LANES:(p + 1) * LANES]
        y_pair = jnp.zeros((SSD_CHUNK, LANES), F32)
        st_add = jnp.zeros((SSD_D_STATE, LANES), F32)
        for hh in range(2):
            h = 2 * p + hh
            col = acs[:, h:h + 1]
            row = acs_t[h:h + 1, :]
            lmat = jnp.exp2(jnp.where(mask, col - row, NEG_INF))
            m = cb[g] * lmat * dt_t[h:h + 1, :]
            xm = jnp.where(half[hh], xp, 0.0).astype(BF16)
            if st_ref is not None:
                cx = cms[g] * jnp.exp2(col)
                lhs = jnp.concatenate([m, cx], axis=1).astype(BF16)
                rhs = jnp.concatenate([xm, jnp.where(half[hh], stp, 0.0).astype(BF16)], axis=0)
                y_pair = y_pair + _dot(lhs, rhs)
                bt_h = (bm_t[g] * dtd_t[h:h + 1, :]).astype(BF16)
                st_add = st_add + _dot(bt_h, xm)
            else:
                y_pair = y_pair + _dot(m.astype(BF16), xm)
        y_ref_write(p, y_pair)
        if st_ref is not None:
            scale = jnp.where(half[0], st_scale_row[:, 2 * p:2 * p + 1], st_scale_row[:, 2 * p + 1:2 * p + 2])
            st_ref[:, p * LANES:(p + 1) * LANES] = stp * scale + st_add


def _ssd_prompt_kernel(z_ref, xs_ref, bc_ref, dt_ref, cw_ref, cb_ref, dtb_ref, a_ref, dtbc_ref, ac_ref,
                       dsk_ref, nw_ref, sh_ref, y_ref, ssm_ref, xbuf, st_ref, ybuf):
    c = pl.program_id(1)
    q = SSD_CHUNK

    @pl.when(c == 0)
    def _():
        xbuf[0:q, :] = jnp.zeros((q, SSD_CONV_DIM), xbuf.dtype)
        st_ref[...] = jnp.zeros(st_ref.shape, F32)

    xbuf[q:2 * q, 0:D_MODEL] = xs_ref[...]
    xbuf[q:2 * q, D_MODEL:SSD_CONV_DIM] = bc_ref[...]
    shifted = _dot(sh_ref[...], xbuf[...])
    acc = cb_ref[...] + cw_ref[SSD_CONV - 1:SSD_CONV, :] * xbuf[q:2 * q, :].astype(F32)
    for s in range(1, SSD_CONV):
        acc = acc + cw_ref[SSD_CONV - 1 - s:SSD_CONV - s, :] * shifted[(s - 1) * q:s * q]
    xbuf[0:q, :] = xbuf[q:2 * q, :]
    xc = _silu(acc)
    x = xc[:, :D_MODEL]
    bm = xc[:, D_MODEL:D_MODEL + 512]
    cm = xc[:, D_MODEL + 512:]

    ri = lax.broadcasted_iota(jnp.int32, (q, q), 0)
    ci = lax.broadcasted_iota(jnp.int32, (q, q), 1)
    tril = ri >= ci
    tril01 = jnp.where(tril, 1.0, 0.0).astype(BF16)
    triu01 = jnp.where(ri <= ci, 1.0, 0.0).astype(BF16)
    ones01 = jnp.ones((q, q), BF16)
    dt, acs, tot, dtd, dt_t, acs_t, dtd_t = _ssd_small(
        dt_ref[...], dtb_ref[...], a_ref[...], dtbc_ref[...], ac_ref[...], tril01, triu01, ones01, ones01)
    decay_row = jnp.exp2(tot[0:1, :])

    def write_y(p, y_pair):
        ybuf[:, p * LANES:(p + 1) * LANES] = y_pair

    _ssd_pairs(x, bm, cm, acs, acs_t, dt_t, dtd_t, tril, st_ref, write_y, decay_row)

    y = (ybuf[...] + dsk_ref[...] * x) * _silu(z_ref[...].astype(F32))
    y_ref[...] = _rms(y, nw_ref[...]).astype(y_ref.dtype)

    @pl.when(c == pl.num_programs(1) - 1)
    def _():
        for p in range(D_MODEL // LANES):
            ssm_ref[p * LANES:(p + 1) * LANES, :] = st_ref[:, p * LANES:(p + 1) * LANES].T


def _ssd_prompt(zx, dt, cw, cb, dtb, a_row, dtb_col, a_col, dsk, nw, *, nbatch, seq):
    nc = seq // SSD_CHUNK
    q = SSD_CHUNK
    row = lambda b, c: b * nc + c
    vec = lambda n: pl.BlockSpec((1, n), lambda b, c: (0, 0))
    col = pl.BlockSpec((LANES, 1), lambda b, c: (0, 0))
    nsh = SSD_CONV - 1
    out_t = jnp.arange(nsh * q) % q
    shift = 1 + jnp.arange(nsh * q) // q
    shift01 = (jnp.arange(2 * q)[None, :] == (q + out_t - shift)[:, None]).astype(BF16)
    return pl.pallas_call(
        _ssd_prompt_kernel,
        grid=(nbatch, nc),
        in_specs=[
            pl.BlockSpec((q, D_MODEL), lambda b, c: (row(b, c), 0)),
            pl.BlockSpec((q, D_MODEL), lambda b, c: (row(b, c), 1)),
            pl.BlockSpec((q, D_MODEL), lambda b, c: (row(b, c), 2)),
            pl.BlockSpec((q, LANES), lambda b, c: (row(b, c), 0)),
            pl.BlockSpec((SSD_CONV, SSD_CONV_DIM), lambda b, c: (0, 0)),
            vec(SSD_CONV_DIM), vec(LANES), vec(LANES), col, col, vec(D_MODEL), vec(D_MODEL),
            pl.BlockSpec(((SSD_CONV - 1) * q, 2 * q), lambda b, c: (0, 0)),
        ],
        out_specs=[
            pl.BlockSpec((q, D_MODEL), lambda b, c: (row(b, c), 0)),
            pl.BlockSpec((None, D_MODEL, SSD_D_STATE), lambda b, c: (b, 0, 0)),
        ],
        out_shape=[
            jax.ShapeDtypeStruct((nbatch * seq, D_MODEL), BF16),
            jax.ShapeDtypeStruct((nbatch, D_MODEL, SSD_D_STATE), F32),
        ],
        scratch_shapes=[
            pltpu.VMEM((2 * q, SSD_CONV_DIM), BF16),
            pltpu.VMEM((SSD_D_STATE, D_MODEL), F32),
            pltpu.VMEM((q, D_MODEL), F32),
        ],
        compiler_params=_cparams(("parallel", "arbitrary")),
        name="ssd_prompt",
    )(zx, zx, zx, dt, cw, cb, dtb, a_row, dtb_col, a_col, dsk, nw, shift01)


def _expand_heads(v, e01):
    return _dot_exact_rhs01(v, e01)


def _ssd_sample_rows_kernel(xs_ref, bc_ref, dt_ref, hist_ref, cw_ref, cb_ref, dtb_ref, a_ref, dtbc_ref, ac_ref,
                            dsk_ref, e01_ref,
                            ypart_ref, ea_ref, xdt_ref, bm_ref, cm_ref, cd_ref, ybuf, *, tdec):
    q = SSD_CHUNK
    x_raw = jnp.concatenate([xs_ref[...], bc_ref[...]], axis=1)
    hist = hist_ref[...]
    rowi = lax.broadcasted_iota(jnp.int32, (q, 1), 0)
    tpos = rowi % tdec
    acc = cb_ref[...] + cw_ref[SSD_CONV - 1:SSD_CONV, :] * x_raw
    for s in range(1, SSD_CONV):
        prev = jnp.where(tpos >= s, pltpu.roll(x_raw, s, 0), pltpu.roll(hist, q - tdec + s, 0))
        acc = acc + cw_ref[SSD_CONV - 1 - s:SSD_CONV - s, :] * prev
    xc = _silu(acc)
    x = xc[:, :D_MODEL]
    bm = xc[:, D_MODEL:D_MODEL + 512]
    cm = xc[:, D_MODEL + 512:]

    ri = lax.broadcasted_iota(jnp.int32, (q, q), 0)
    ci = lax.broadcasted_iota(jnp.int32, (q, q), 1)
    same = (ri // tdec) == (ci // tdec)
    mask = same & (ri >= ci)
    low01 = jnp.where(mask, 1.0, 0.0).astype(BF16)
    up01 = jnp.where(same & (ri <= ci), 1.0, 0.0).astype(BF16)
    same01 = jnp.where(same, 1.0, 0.0).astype(BF16)
    dt, acs, tot, dtd, dt_t, acs_t, dtd_t = _ssd_small(
        dt_ref[...], dtb_ref[...], a_ref[...], dtbc_ref[...], ac_ref[...], low01, up01, same01, same01)

    def write_y(p, y_pair):
        ybuf[:, p * LANES:(p + 1) * LANES] = y_pair

    _ssd_pairs(x, bm, cm, acs, acs_t, dt_t, dtd_t, mask, None, write_y, None)

    e01 = e01_ref[...]
    ypart_ref[...] = ybuf[...] + dsk_ref[...] * x
    ea_ref[...] = _expand_heads(jnp.exp2(acs), e01)
    xd = x * _expand_heads(dtd, e01)
    for p in range(D_MODEL // LANES):
        xdt_ref[p * LANES:(p + 1) * LANES, :] = xd[:, p * LANES:(p + 1) * LANES].T.astype(BF16)
    bm_ref[...] = bm.astype(BF16)
    cm_ref[...] = cm.astype(BF16)
    cd_ref[...] = jnp.exp2(tot)


def _ssd_sample_rows(zx, dt, hist, cw, cb, dtb, a_row, dtb_col, a_col, dsk, e01, *, tdec):
    t = zx.shape[0]
    q = SSD_CHUNK
    vec = lambda n: pl.BlockSpec((1, n), lambda i: (0, 0))
    col = pl.BlockSpec((LANES, 1), lambda i: (0, 0))
    kern = functools.partial(_ssd_sample_rows_kernel, tdec=tdec)
    return pl.pallas_call(
        kern,
        grid=(t // q,),
        in_specs=[
            pl.BlockSpec((q, D_MODEL), lambda i: (i, 1)),
            pl.BlockSpec((q, D_MODEL), lambda i: (i, 2)),
            pl.BlockSpec((q, LANES), lambda i: (i, 0)),
            pl.BlockSpec((q, SSD_CONV_DIM), lambda i: (i, 0)),
            pl.BlockSpec((SSD_CONV, SSD_CONV_DIM), lambda i: (0, 0)),
            vec(SSD_CONV_DIM), vec(LANES), vec(LANES), col, col, vec(D_MODEL),
            pl.BlockSpec((LANES, D_MODEL), lambda i: (0, 0)),
        ],
        out_specs=[
            pl.BlockSpec((q, D_MODEL), lambda i: (i, 0)),
            pl.BlockSpec((q, D_MODEL), lambda i: (i, 0)),
            pl.BlockSpec((D_MODEL, q), lambda i: (0, i)),
            pl.BlockSpec((q, 512), lambda i: (i, 0)),
            pl.BlockSpec((q, 512), lambda i: (i, 0)),
            pl.BlockSpec((q, LANES), lambda i: (i, 0)),
        ],
        out_shape=[
            jax.ShapeDtypeStruct((t, D_MODEL), F32),
            jax.ShapeDtypeStruct((t, D_MODEL), F32),
            jax.ShapeDtypeStruct((D_MODEL, t), BF16),
            jax.ShapeDtypeStruct((t, 512), BF16),
            jax.ShapeDtypeStruct((t, 512), BF16),
            jax.ShapeDtypeStruct((t, LANES), F32),
        ],
        scratch_shapes=[pltpu.VMEM((q, D_MODEL), F32)],
        compiler_params=_cparams(("parallel",)),
        name="ssd_sample_rows",
    )(zx, zx, dt, hist, cw, cb, dtb, a_row, dtb_col, a_col, dsk, e01)


def _ssd_sample_state_kernel(cd_ref, z_ref, ypart_ref, ea_ref, xdt_ref, bm_ref, cm_ref, st_ref, nw_ref,
                             y_ref, sto_ref, *, tdec, gseq):
    i = pl.program_id(0)
    q = SSD_CHUNK
    rows = gseq * tdec
    steps_per_blk = q // rows
    lane = lax.broadcasted_iota(jnp.int32, (1, q), 1)
    rg = 16
    rowi = lax.broadcasted_iota(jnp.int32, (rg, 1), 0)
    lane_base = (i % steps_per_blk) * rows
    seq_per_rg = rg // tdec
    yoff_groups = []
    for q8 in range(rows // rg):
        yoff8 = jnp.zeros((rg, D_MODEL), F32)
        for sj in range(seq_per_rg):
            jj = q8 * seq_per_rg + sj
            b = i * gseq + jj
            lmask = (lane >= lane_base + jj * tdec) & (lane < lane_base + (jj + 1) * tdec)
            parts = []
            for g in range(SSD_N_GROUPS):
                s_g = st_ref[jj, g * 256:(g + 1) * 256, :]
                c8 = cm_ref[q8 * rg:(q8 + 1) * rg, g * SSD_D_STATE:(g + 1) * SSD_D_STATE]
                parts.append(_dot_nt(c8, s_g.astype(BF16)))
                lhs = jnp.where(lmask, xdt_ref[g * 256:(g + 1) * 256, :], jnp.zeros((), BF16))
                add = _dot(lhs, bm_ref[:, g * SSD_D_STATE:(g + 1) * SSD_D_STATE])
                for hh in range(4):
                    h = 4 * g + hh
                    sl = slice(hh * SSD_HEAD_DIM, (hh + 1) * SSD_HEAD_DIM)
                    sto_ref[jj, g * 256 + hh * 64:g * 256 + (hh + 1) * 64, :] = (
                        s_g[sl, :] * cd_ref[b, h] + add[sl, :])
            yo = jnp.concatenate(parts, axis=1)
            in_seq = (rowi >= sj * tdec) & (rowi < (sj + 1) * tdec)
            yoff8 = jnp.where(in_seq, yo, yoff8)
        yoff_groups.append(yoff8)
    yoff = jnp.concatenate(yoff_groups, axis=0) if len(yoff_groups) > 1 else yoff_groups[0]
    y = (ypart_ref[...] + yoff * ea_ref[...]) * _silu(z_ref[...])
    y_ref[...] = _rms(y, nw_ref[...])


def _ssd_sample_state(cd, zx, ypart, ea, xdt, bmb, cmb, state, nw, *, tdec, gseq):
    t = zx.shape[0]
    nseq = t // tdec
    q = SSD_CHUNK
    rows = gseq * tdec
    spb = q // rows
    kern = functools.partial(_ssd_sample_state_kernel, tdec=tdec, gseq=gseq)
    return pl.pallas_call(
        kern,
        grid=(nseq // gseq,),
        in_specs=[
            pl.BlockSpec(memory_space=pltpu.SMEM),
            pl.BlockSpec((rows, D_MODEL), lambda i: (i, 0)),
            pl.BlockSpec((rows, D_MODEL), lambda i: (i, 0)),
            pl.BlockSpec((rows, D_MODEL), lambda i: (i, 0)),
            pl.BlockSpec((D_MODEL, q), lambda i: (0, i // spb)),
            pl.BlockSpec((q, 512), lambda i: (i // spb, 0)),
            pl.BlockSpec((rows, 512), lambda i: (i, 0)),
            pl.BlockSpec((gseq, D_MODEL, SSD_D_STATE), lambda i: (i, 0, 0)),
            pl.BlockSpec((1, D_MODEL), lambda i: (0, 0)),
        ],
        out_specs=[
            pl.BlockSpec((rows, D_MODEL), lambda i: (i, 0)),
            pl.BlockSpec((gseq, D_MODEL, SSD_D_STATE), lambda i: (i, 0, 0)),
        ],
        out_shape=[
            jax.ShapeDtypeStruct((t, D_MODEL), F32),
            jax.ShapeDtypeStruct((nseq, D_MODEL, SSD_D_STATE), F32),
        ],
        compiler_params=_cparams(("parallel",)),
        name="ssd_sample_state",
    )(cd, zx, ypart, ea, xdt, bmb, cmb, state, nw)


def _lse_lane(h):
    return 16 * h


def _attn_prompt_kernel(q_ref, kc_ref, kp_ref, vc_ref, vp_ref, o_ref, lse_ref, *, nq):
    n = pl.program_id(2)
    nk = 128
    k = jnp.concatenate([kp_ref[...], kc_ref[...]], axis=0)
    v = jnp.concatenate([vp_ref[...], vc_ref[...]], axis=0)
    lane = lax.broadcasted_iota(jnp.int32, (1, LANES), 1)
    lo = lane < ATT_HEAD_DIM
    qi = lax.broadcasted_iota(jnp.int32, (2 * nk, 2 * nk), 0) % nk
    kj = lax.broadcasted_iota(jnp.int32, (2 * nk, 2 * nk), 1)
    band = (kj >= qi) & (kj <= qi + nk)
    band_first = band & ((kj >= nk) | (n > 0))
    q = q_ref[...]
    zero = jnp.zeros((), q.dtype)
    npairs = ATT_HEADS // 2
    pair_lanes = [slice(p * LANES, (p + 1) * LANES) for p in range(npairs)]
    units = [(j, p) for j in range(nq) for p in range(npairs)]

    def scores(u):
        j, p = u
        qp = q[j * nk:(j + 1) * nk, pair_lanes[p]]
        q2 = jnp.concatenate([jnp.where(lo, qp, zero), jnp.where(lo, zero, qp)], axis=0)
        s = _dot_nt(q2, k[j * nk:(j + 2) * nk, pair_lanes[p]])
        return jnp.where(band_first if j == 0 else band, s, NEG_INF)

    def softmax(s):
        m = jnp.max(s, axis=-1, keepdims=True)
        e = jnp.exp2(s - m)
        den = jnp.sum(e, axis=-1, keepdims=True)
        return e.astype(BF16), den, (m + jnp.log2(den)) * LN2

    ahead = 3
    s_next = [scores(u) for u in units[:ahead]]
    lse_full, o_parts = None, []
    for i, (j, p) in enumerate(units):
        e, den, lse = softmax(s_next.pop(0))
        if i + ahead < len(units):
            s_next.append(scores(units[i + ahead]))
        r = _dot(e, v[j * nk:(j + 2) * nk, pair_lanes[p]]) / den
        o_parts.append(jnp.where(lo, r[0:nk], r[nk:2 * nk]).astype(o_ref.dtype))
        if p == 0:
            lse_full = jnp.zeros((nk, LANES), F32)
        for hh in range(2):
            base = _lse_lane(2 * p + hh)
            lse_full = jnp.where((lane >= base) & (lane < base + 16), lse[hh * nk:(hh + 1) * nk], lse_full)
        if p == npairs - 1:
            o_ref[j * nk:(j + 1) * nk, :] = jnp.concatenate(o_parts, axis=1)
            lse_ref[j * nk:(j + 1) * nk, :] = lse_full
            o_parts = []


def _attn_prompt(qkv, *, d):
    nb, _, length, _ = qkv.shape
    nk = 128
    nq = 2 if length % (2 * nk) == 0 else 1
    nblk = length // (nq * nk)
    cur = lambda which: pl.BlockSpec((None, None, nq * nk, COL_TILE), lambda b, r, n: (b, r, n, which))
    prev = lambda which: pl.BlockSpec((None, None, nk, COL_TILE),
                                      lambda b, r, n: (b, r, jnp.maximum(nq * n - 1, 0), which))
    return pl.pallas_call(
        functools.partial(_attn_prompt_kernel, nq=nq),
        grid=(nb, d, nblk),
        in_specs=[cur(0), cur(1), prev(1), cur(2), prev(2)],
        out_specs=[
            pl.BlockSpec((None, None, nq * nk, ATT_WIDTH), lambda b, r, n: (b, r, n, 0)),
            pl.BlockSpec((None, None, nq * nk, LANES), lambda b, r, n: (b, r, n, 0)),
        ],
        out_shape=[
            jax.ShapeDtypeStruct((nb, d, length, ATT_WIDTH), BF16),
            jax.ShapeDtypeStruct((nb, d, length, LANES), F32),
        ],
        compiler_params=_cparams(("parallel", "parallel", "arbitrary")),
        name=f"attn_prompt_d{d}",
    )(qkv, qkv, qkv, qkv, qkv)


def _attn_sample_kernel(q0_ref, q1_ref, q2_ref, qm_ref, c0_ref, c1_ref, c2_ref, cm_ref, oa_ref, om_ref, *, tdec):
    nh = ATT_HEADS
    rows8 = 8
    j = pl.program_id(1)
    r64 = nh * rows8
    rowi = lax.broadcasted_iota(jnp.int32, (r64, 1), 0)
    head_of_row = rowi // rows8
    r8 = rowi % rows8
    seq_q = r8 // tdec
    t_q = r8 % tdec
    lane512 = lax.broadcasted_iota(jnp.int32, (1, ATT_WIDTH), 1)
    headmask = (lane512 // ATT_HEAD_DIM) == head_of_row
    col8 = lax.broadcasted_iota(jnp.int32, (1, rows8), 1)
    seq_k = col8 // tdec
    t_k = col8 % tdec

    outs, lses = [], []
    for g, (q_ref, c_ref, d) in enumerate(zip((q0_ref, q1_ref, q2_ref), (c0_ref, c1_ref, c2_ref), ATT_DILATIONS)):
        w = c_ref.shape[-1]
        qkv = q_ref[...]
        qv = qkv[:, :ATT_WIDTH]
        kn = qkv[:, ATT_WIDTH:2 * ATT_WIDTH]
        vn = qkv[:, 2 * ATT_WIDTH:]
        q64 = jnp.where(headmask, jnp.concatenate([qv] * nh, axis=0), 0.0)
        s_new = _dot_nt(q64, kn)
        if d == 1:
            valid_new = (seq_k == seq_q) & (t_k <= t_q)
        else:
            valid_new = (seq_k == seq_q) & (t_k == t_q)
        s_new = jnp.where(valid_new, s_new, NEG_INF)
        wi = lax.broadcasted_iota(jnp.int32, (1, w), 1)
        valid_c = (wi >= t_q) if d == 1 else ((wi % d) == t_q)
        s_c = jnp.where(valid_c, _dot(q64.astype(BF16), c_ref[0].astype(BF16)), NEG_INF)
        m = jnp.maximum(jnp.max(s_c, axis=-1, keepdims=True), jnp.max(s_new, axis=-1, keepdims=True))
        e_c = jnp.exp2(s_c - m)
        e_n = jnp.exp2(s_new - m)
        den = jnp.sum(e_c, axis=-1, keepdims=True) + jnp.sum(e_n, axis=-1, keepdims=True)
        o = _dot(e_n, vn) + _dot_nt(e_c.astype(BF16), c_ref[1].astype(BF16))
        outs.append(o / den)
        lses.append((m + jnp.log2(den)) * LN2)

    mx = jnp.maximum(jnp.maximum(lses[0], lses[1]), lses[2])
    ws = [jnp.exp(l - mx) for l in lses]
    om = (ws[0] * outs[0] + ws[1] * outs[1] + ws[2] * outs[2]) / (ws[0] + ws[1] + ws[2])
    om = jnp.where(headmask, om, 0.0)
    o8 = om[0:rows8]
    for h in range(1, nh):
        o8 = o8 + om[h * rows8:(h + 1) * rows8]

    qm = jnp.concatenate([qm_ref[...] * (MEM_HEAD_DIM ** -0.5), jnp.zeros((rows8, MEM_WIDTH), F32)], axis=0)
    mparts = []
    for h in range(MEM_HEADS):
        kh = cm_ref[pl.ds(h, MEM_LEN, stride=2 * MEM_HEADS), :].astype(BF16)
        vh = cm_ref[pl.ds(MEM_HEADS + h, MEM_LEN, stride=2 * MEM_HEADS), :].astype(BF16)
        s = _dot_nt(qm[:, h * MEM_HEAD_DIM:(h + 1) * MEM_HEAD_DIM].astype(BF16), kh)
        mm = jnp.max(s, axis=-1, keepdims=True)
        e = jnp.exp(s - mm)
        den = jnp.sum(e, axis=-1, keepdims=True)
        mparts.append((_dot(e.astype(BF16), vh) / den)[0:rows8])
    m8 = jnp.concatenate(mparts, axis=1)

    row8 = lax.broadcasted_iota(jnp.int32, (rows8, 1), 0)
    mine = (row8 // tdec) == j

    @pl.when(j == 0)
    def _():
        oa_ref[...] = jnp.where(mine, o8, 0.0)
        om_ref[...] = jnp.where(mine, m8, 0.0)

    @pl.when(j > 0)
    def _():
        oa_ref[...] = jnp.where(mine, o8, oa_ref[...])
        om_ref[...] = jnp.where(mine, m8, om_ref[...])


def _attn_sample(q0, q1, q2, rest, c0, c1, c2, cmem, *, tdec):
    t = q0.shape[0]
    spg = 8 // tdec
    row_spec = pl.BlockSpec((8, 3 * COL_TILE), lambda i, j: (i, 0))
    cache_spec = lambda c: pl.BlockSpec((None,) + c.shape[1:], lambda i, j: (i * spg + j,) + (0,) * (c.ndim - 1))
    kern = functools.partial(_attn_sample_kernel, tdec=tdec)
    return pl.pallas_call(
        kern,
        grid=(t // 8, spg),
        in_specs=[
            row_spec, row_spec, row_spec,
            pl.BlockSpec((8, COL_TILE), lambda i, j: (i, 1)),
            cache_spec(c0), cache_spec(c1), cache_spec(c2), cache_spec(cmem),
        ],
        out_specs=[
            pl.BlockSpec((8, ATT_WIDTH), lambda i, j: (i, 0)),
            pl.BlockSpec((8, MEM_WIDTH), lambda i, j: (i, 0)),
        ],
        out_shape=[
            jax.ShapeDtypeStruct((t, ATT_WIDTH), F32),
            jax.ShapeDtypeStruct((t, MEM_WIDTH), F32),
        ],
        compiler_params=_cparams(("parallel", "arbitrary")),
        name="attn_sample",
    )(q0, q1, q2, rest, c0, c1, c2, cmem)


def _final_tail(x, p_ssd, y_att, y_mem, gate_raw, watt_ref, wmem_ref, wout_ref, nf_ref):
    gates = _sigmoid(gate_raw)
    merged = (gates[:, 0:D_MODEL] * p_ssd
              + gates[:, D_MODEL:2 * D_MODEL] * _dot(y_att.astype(BF16), watt_ref[...])
              + gates[:, 2 * D_MODEL:] * _dot(y_mem.astype(BF16), wmem_ref[...]))
    return _rms(x + _dot(merged.astype(BF16), wout_ref[...]), nf_ref[...])


def _final_prompt_kernel(x_ref, ys_ref, o0_ref, o1_ref, o2_ref, l0_ref, l1_ref, l2_ref, rest_ref, mkv_ref,
                         wssd_ref, watt_ref, wmem_ref, wout_ref, nf_ref, e16_ref,
                         y_ref, obuf, lbuf, *, tm, dils, nsub):
    nslab = ATT_WIDTH // LANES
    o_refs, l_refs = (o0_ref, o1_ref, o2_ref), (l0_ref, l1_ref, l2_ref)
    for g, (o_ref, l_ref, d) in enumerate(zip(o_refs, l_refs, dils)):
        if d == 1:
            continue
        for r in range(d):
            blk = o_ref[r].astype(F32)
            for c in range(nslab):
                obuf[g, c, pl.ds(r, tm // d, stride=d), :] = blk[:, c * LANES:(c + 1) * LANES]
            lbuf[g, pl.ds(r, tm // d, stride=d), :] = l_ref[r]
    e16 = e16_ref[...]

    for rs in range(nsub):
        rows = slice(rs * (tm // nsub), (rs + 1) * (tm // nsub))
        on, ln = [], []
        for g in range(3):
            if dils[g] == 1:
                on.append(o_refs[g][0, rows, :].astype(F32))
                ln.append(l_refs[g][0, rows, :])
            else:
                on.append(jnp.concatenate([obuf[g, c, rows, :] for c in range(nslab)], axis=1))
                ln.append(lbuf[g, rows, :])
        mx = jnp.maximum(jnp.maximum(ln[0], ln[1]), ln[2])
        ws = [jnp.exp(l - mx) for l in ln]
        wsum = ws[0] + ws[1] + ws[2]
        o_att = jnp.zeros((tm // nsub, ATT_WIDTH), F32)
        for g in range(3):
            hi, mid, _ = _split3(ws[g] / wsum)
            o_att = o_att + _dot(jnp.concatenate([hi, mid], axis=1), e16) * on[g]

        rest = rest_ref[rows, :]
        z_att = rest[:, 0:512].astype(F32)
        q_mem = rest[:, 512:1024]
        z_mem = rest[:, 1024:1536].astype(F32)
        gate_raw = rest[:, 1536:].astype(F32)
        y_att = o_att * _silu(z_att)

        outs = []
        for h in range(MEM_HEADS):
            lanes = slice(h * MEM_HEAD_DIM, (h + 1) * MEM_HEAD_DIM)
            vh = mkv_ref[:, MEM_WIDTH + h * MEM_HEAD_DIM:MEM_WIDTH + (h + 1) * MEM_HEAD_DIM]
            s = _dot_nt(q_mem[:, lanes], mkv_ref[:, lanes]) * (MEM_HEAD_DIM ** -0.5)
            m = jnp.max(s, axis=-1, keepdims=True)
            e = jnp.exp(s - m)
            den = jnp.sum(e, axis=-1, keepdims=True)
            outs.append(_dot(e.astype(BF16), vh) / den)
        y_mem = jnp.concatenate(outs, axis=1) * _silu(z_mem)

        p_ssd = _dot(ys_ref[rows, :], wssd_ref[...])
        y_ref[rows, :] = _final_tail(x_ref[rows, :], p_ssd, y_att, y_mem, gate_raw,
                                     watt_ref, wmem_ref, wout_ref, nf_ref)


def _final_prompt(x2d, y_ssd, o_g, l_g, rest, mkv_bf, wssd, watt, wmem, wout, nf, e16, *, seq, tm, dils, nsub):
    t = x2d.shape[0]
    tps = seq // tm

    def res_spec(d, width):
        return pl.BlockSpec((None, d, tm // d, width), lambda i: (i // tps, 0, i % tps, 0))

    full = lambda a: pl.BlockSpec(a.shape, lambda i: (0,) * a.ndim)
    kern = functools.partial(_final_prompt_kernel, tm=tm, dils=dils, nsub=nsub)
    return pl.pallas_call(
        kern,
        grid=(t // tm,),
        in_specs=[
            pl.BlockSpec((tm, D_MODEL), lambda i: (i, 0)),
            pl.BlockSpec((tm, D_MODEL), lambda i: (i, 0)),
            res_spec(dils[0], ATT_WIDTH), res_spec(dils[1], ATT_WIDTH), res_spec(dils[2], ATT_WIDTH),
            res_spec(dils[0], LANES), res_spec(dils[1], LANES), res_spec(dils[2], LANES),
            pl.BlockSpec((tm, rest.shape[1]), lambda i: (i, 0)),
            pl.BlockSpec((None, MEM_LEN, 2 * MEM_WIDTH), lambda i: (i // tps, 0, 0)),
            full(wssd), full(watt), full(wmem), full(wout), full(nf), full(e16),
        ],
        out_specs=pl.BlockSpec((tm, D_MODEL), lambda i: (i, 0)),
        out_shape=jax.ShapeDtypeStruct((t, D_MODEL), F32),
        scratch_shapes=[
            pltpu.VMEM((3, ATT_WIDTH // LANES, tm, LANES), F32),
            pltpu.VMEM((3, tm, LANES), F32),
        ],
        compiler_params=_cparams(("parallel",)),
        name="final_prompt",
    )(x2d, y_ssd, *o_g, *l_g, rest, mkv_bf, wssd, watt, wmem, wout, nf, e16)


def _final_sample_kernel(x_ref, ys_ref, oa_ref, om_ref, rest_ref,
                         wssd_ref, watt_ref, wmem_ref, wout_ref, nf_ref, y_ref):
    p_ssd = _dot(ys_ref[...].astype(BF16), wssd_ref[...])
    rest = rest_ref[...]
    y_att = oa_ref[...] * _silu(rest[:, 0:512])
    y_mem = om_ref[...] * _silu(rest[:, 1024:1536])
    y_ref[...] = _final_tail(x_ref[...], p_ssd, y_att, y_mem, rest[:, 1536:], watt_ref, wmem_ref, wout_ref, nf_ref)


def _final_sample(x2d, y_ssd, o_att, o_mem, rest, wssd, watt, wmem, wout, nf, *, tm):
    t = x2d.shape[0]
    full = lambda a: pl.BlockSpec(a.shape, lambda i: (0,) * a.ndim)
    rows = lambda w: pl.BlockSpec((tm, w), lambda i: (i, 0))
    return pl.pallas_call(
        _final_sample_kernel,
        grid=(t // tm,),
        in_specs=[rows(D_MODEL), rows(D_MODEL), rows(ATT_WIDTH), rows(MEM_WIDTH), rows(rest.shape[1]),
                  full(wssd), full(watt), full(wmem), full(wout), full(nf)],
        out_specs=rows(D_MODEL),
        out_shape=jax.ShapeDtypeStruct((t, D_MODEL), F32),
        compiler_params=_cparams(("parallel",)),
        name="final_sample",
    )(x2d, y_ssd, o_att, o_mem, rest, wssd, watt, wmem, wout, nf)


def kernel(x_prompt, x_sample, mem_prompt, cache_win128_kv, cache_win512_kv, cache_win2048_kv, cache_mem_kv,
           state_conv, state_ssm, norm_in_w, w_in, conv_w, conv_b, dt_bias, a_log, d_skip, ssd_norm_w,
           mem_norm_w, w_mem_kv, w_br_ssd, w_br_att, w_br_mem, w_out, norm_f_w):
    bp, seq, _ = x_prompt.shape
    bs, tdec, _ = x_sample.shape
    dils = ATT_DILATIONS

    row = lambda v: v.reshape(1, -1).astype(F32)
    wt = w_in.T.astype(BF16)
    pad_heads = lambda v: jnp.pad(v.astype(F32), (0, LANES - SSD_N_HEADS)).reshape(1, LANES)
    dtb = pad_heads(dt_bias)
    a_row = jnp.pad(-jnp.exp(a_log.astype(F32)) * LOG2E, (0, LANES - SSD_N_HEADS)).reshape(1, LANES)
    dtb_col, a_col = dtb.reshape(LANES, 1), a_row.reshape(LANES, 1)
    dsk = jnp.repeat(d_skip.astype(F32), SSD_HEAD_DIM).reshape(1, D_MODEL)
    cw = conv_w.astype(F32)
    cb = row(conv_b)
    nin, nssd, nmem, nf = row(norm_in_w), row(ssd_norm_w), row(mem_norm_w), row(norm_f_w)
    wssd, watt, wmem, wout = (w.astype(BF16) for w in (w_br_ssd, w_br_att, w_br_mem, w_out))
    lane = jnp.arange(LANES)
    e_ssd = (lane[:, None] == (jnp.arange(D_MODEL)[None, :] // SSD_HEAD_DIM)).astype(BF16)
    lse_lane_of_ch = jnp.repeat(jnp.array([_lse_lane(h) for h in range(ATT_HEADS)]), ATT_HEAD_DIM)
    e_att = (lane[:, None] == lse_lane_of_ch[None, :]).astype(BF16)
    e_att = jnp.concatenate([e_att, e_att], axis=0)

    xp2 = x_prompt.reshape(bp * seq, D_MODEL)
    mkv_f32, mkv_bf = _mem_kv(mem_prompt.reshape(bp * MEM_LEN, D_MODEL), nmem, w_mem_kv.astype(BF16))
    tm_p = 256
    tabs_p = _rope_tables(jnp.arange(seq))
    windows = tuple(min(w, seq) for w in ATT_WINDOWS)
    zx, q0, q1, q2, rest, dtr, kv0, kv1, kv2 = _in_proj(
        xp2, nin, wt, tabs_p, seq=seq, tm=tm_p, dils=dils, out_dtype=BF16, windows=windows)
    y_ssd, p_ssm = _ssd_prompt(zx, dtr, cw, cb, dtb, a_row, dtb_col, a_col, dsk, nssd, nbatch=bp, seq=seq)
    o_g, l_g = [], []
    for qkv, d in zip((q0, q1, q2), dils):
        o, l = _attn_prompt(qkv, d=d)
        o_g.append(o)
        l_g.append(l)
    y_prompt = _final_prompt(xp2, y_ssd, o_g, l_g, rest, mkv_bf.reshape(bp, MEM_LEN, 2 * MEM_WIDTH),
                             wssd, watt, wmem, wout, nf, e_att, seq=seq, tm=512, dils=dils, nsub=1)
    y_prompt = y_prompt.reshape(bp, seq, D_MODEL)

    p_win = [kv.reshape(bp, 2, ATT_HEADS, ATT_HEAD_DIM, kv.shape[-1]).transpose(0, 4, 1, 2, 3)
             for kv in (kv0, kv1, kv2)]
    p_mem_kv = mkv_f32.reshape(bp, MEM_LEN, 2, MEM_HEADS, MEM_HEAD_DIM)
    p_conv = zx.reshape(bp, seq, -1)[:, seq - (SSD_CONV - 1):, D_MODEL:].astype(F32)
    p_ssm = p_ssm.reshape(bp, SSD_N_HEADS, SSD_HEAD_DIM, SSD_D_STATE)

    ts = bs * tdec
    pos_s = PAST_LEN + jnp.tile(jnp.arange(tdec), bs)
    xs2 = x_sample.reshape(ts, D_MODEL)
    tabs_s = _rope_tables(pos_s)
    zx_s, q0_s, q1_s, q2_s, rest_s, dt_s = _in_proj(
        xs2, nin, wt, tabs_s, seq=ts, tm=min(ts, 128), dils=(1, 1, 1), out_dtype=F32)
    q_s = [a.reshape(ts, 3 * COL_TILE) for a in (q0_s, q1_s, q2_s)]
    hist = jnp.pad(state_conv.astype(F32), ((0, 0), (tdec - (SSD_CONV - 1), 0), (0, 0))).reshape(ts, SSD_CONV_DIM)
    ypart, ea, xdt, bmb, cmb, cd = _ssd_sample_rows(
        zx_s, dt_s, hist, cw, cb, dtb, a_row, dtb_col, a_col, dsk, e_ssd, tdec=tdec)
    cd_seq = cd.reshape(bs, tdec, LANES)[:, 0, :SSD_N_HEADS]
    y_ssd_s, s_ssm = _ssd_sample_state(cd_seq, zx_s, ypart, ea, xdt, bmb, cmb,
                                       state_ssm.reshape(bs, D_MODEL, SSD_D_STATE), nssd, tdec=tdec, gseq=8)
    to_kt = lambda c: c.transpose(0, 2, 3, 4, 1).reshape(bs, 2, ATT_WIDTH, c.shape[1])
    c0, c1, c2 = to_kt(cache_win128_kv), to_kt(cache_win512_kv), to_kt(cache_win2048_kv)
    cmem = cache_mem_kv.reshape(bs, MEM_LEN * 2 * MEM_HEADS, MEM_HEAD_DIM)
    o_att_s, o_mem_s = _attn_sample(*q_s, rest_s, c0, c1, c2, cmem, tdec=tdec)
    y_sample = _final_sample(xs2, y_ssd_s, o_att_s, o_mem_s, rest_s, wssd, watt, wmem, wout, nf, tm=min(ts, 256))
    y_sample = y_sample.reshape(bs, tdec, D_MODEL)

    s_win = [a[:, COL_TILE:].reshape(bs, tdec, 2, ATT_HEADS, ATT_HEAD_DIM) for a in q_s]
    s_conv = zx_s.reshape(bs, tdec, -1)[:, tdec - (SSD_CONV - 1):, D_MODEL:]
    s_ssm = s_ssm.reshape(bs, SSD_N_HEADS, SSD_HEAD_DIM, SSD_D_STATE)

    return (y_prompt, y_sample, p_win[0], p_win[1], p_win[2], p_mem_kv, p_conv, p_ssm,
            s_win[0], s_win[1], s_win[2], s_conv, s_ssm)
```

```python
import functools
import math

import jax
import jax.numpy as jnp
from jax import lax
from jax.experimental import pallas as pl
from jax.experimental.pallas import tpu as pltpu

F32 = jnp.float32
BF16 = jnp.bfloat16

D_MODEL = 1024
NORM_EPS = 1e-6
SSD_HEAD_DIM = 64
SSD_N_HEADS = 16
SSD_N_GROUPS = 4
SSD_D_STATE = 128
SSD_CONV = 4
SSD_CHUNK = 128
SSD_CONV_DIM = 2048
ATT_WINDOWS = (128, 512, 2048)
ATT_DILATIONS = (1, 4, 16)
ATT_HEADS = 8
ATT_HEAD_DIM = 64
ATT_WIDTH = 512
ROPE_DIM = 16
ROPE_THETA = 500000.0
MEM_LEN = 256
MEM_HEADS = 4
MEM_HEAD_DIM = 128
MEM_WIDTH = 512
NEG_INF = -1e30
PAST_LEN = 8192

LANES = 128
COL_TILE = 512
N_COL_TILES = 24
DT_COL0 = 3072
QKV_TILE0 = 6
REST_TILE0 = 15
VMEM_LIMIT = 56 * 1024 * 1024


def _cparams(sem):
    return pltpu.CompilerParams(dimension_semantics=sem, vmem_limit_bytes=VMEM_LIMIT)


def _dot(a, b):
    return jnp.dot(a, b, preferred_element_type=F32)


def _dot_nt(a, b):
    return lax.dot_general(a, b, (((1,), (1,)), ((), ())), preferred_element_type=F32)


def _rms(x, w):
    return x * lax.rsqrt(jnp.mean(x * x, axis=-1, keepdims=True) + NORM_EPS) * w


def _sigmoid(x):
    return 0.5 * jnp.tanh(0.5 * x) + 0.5


def _silu(x):
    u = 0.5 * x
    return u * jnp.tanh(u) + u


LOG2E = 1.4426950408889634
LN2 = 0.6931471805599453


def _softplus(x):
    return jnp.maximum(x, 0.0) + jnp.log(1.0 + jnp.exp(-jnp.abs(x)))


def _split3(x):
    hi = x.astype(BF16)
    r1 = x - hi.astype(F32)
    mid = r1.astype(BF16)
    lo = (r1 - mid.astype(F32)).astype(BF16)
    return hi, mid, lo


def _dot_exact_lhs01(m01, x):
    hi, mid, lo = _split3(x)
    return _dot(m01, hi) + _dot(m01, mid) + _dot(m01, lo)


def _dot_exact_rhs01(x, m01):
    hi, mid, lo = _split3(x)
    return _dot(hi, m01) + _dot(mid, m01) + _dot(lo, m01)


def _mem_kv_kernel(x_ref, nw_ref, w_ref, o_ref, ob_ref):
    h = _rms(x_ref[...], nw_ref[...]).astype(BF16)
    acc = _dot(h, w_ref[...])
    o_ref[...] = acc
    ob_ref[...] = acc.astype(BF16)


def _mem_kv(mem2d, nw, w_bf):
    m = mem2d.shape[0]
    tm = 256
    return pl.pallas_call(
        _mem_kv_kernel,
        grid=(m // tm,),
        in_specs=[
            pl.BlockSpec((tm, D_MODEL), lambda i: (i, 0)),
            pl.BlockSpec((1, D_MODEL), lambda i: (0, 0)),
            pl.BlockSpec((D_MODEL, 2 * MEM_WIDTH), lambda i: (0, 0)),
        ],
        out_specs=[
            pl.BlockSpec((tm, 2 * MEM_WIDTH), lambda i: (i, 0)),
            pl.BlockSpec((tm, 2 * MEM_WIDTH), lambda i: (i, 0)),
        ],
        out_shape=[
            jax.ShapeDtypeStruct((m, 2 * MEM_WIDTH), F32),
            jax.ShapeDtypeStruct((m, 2 * MEM_WIDTH), BF16),
        ],
        compiler_params=_cparams(("parallel",)),
        name="mem_kv",
    )(mem2d, nw, w_bf)


def _rope_tables(pos):
    half = ROPE_DIM // 2
    inv = jnp.power(ROPE_THETA, -jnp.arange(half, dtype=F32) * 2.0 / ROPE_DIM)
    ang = pos.astype(F32)[:, None] * inv[None, :]
    cos, sin = jnp.cos(ang), jnp.sin(ang)
    m = jnp.arange(LANES) % ATT_HEAD_DIM
    idx = m % half
    c = jnp.where(m[None, :] < ROPE_DIM, cos[:, idx], 1.0)
    s1 = jnp.where(m[None, :] < half, -sin[:, idx], 0.0)
    s2 = jnp.where((m[None, :] >= half) & (m[None, :] < ROPE_DIM), sin[:, idx], 0.0)
    return c.astype(F32), s1.astype(F32), s2.astype(F32)


def _in_proj_kernel(x_ref, nw_ref, wt_ref, c_ref, s1_ref, s2_ref,
                    zx_ref, q0_ref, q1_ref, q2_ref, rest_ref, dt_ref, *more_refs, tm, dils, win_rows):
    pw_refs = more_refs[:3] if win_rows else ()
    h_ref, acc_ref, tmp_ref = more_refs[-3:]
    nslab = COL_TILE // LANES
    h_ref[...] = _rms(x_ref[...], nw_ref[...]).astype(BF16)
    dt_ref[...] = _dot_nt(h_ref[...], wt_ref[DT_COL0:DT_COL0 + LANES, :])
    q_refs = (q0_ref, q1_ref, q2_ref)
    strided = 0
    for j in range(N_COL_TILES):
        col0 = j * COL_TILE + (SSD_N_HEADS if j >= QKV_TILE0 else 0)
        acc = _dot_nt(h_ref[...], wt_ref[col0:col0 + COL_TILE, :])
        if j < QKV_TILE0:
            zx_ref[:, j * COL_TILE:(j + 1) * COL_TILE] = acc.astype(zx_ref.dtype)
            continue
        if j >= REST_TILE0:
            rest_ref[:, (j - REST_TILE0) * COL_TILE:(j - REST_TILE0 + 1) * COL_TILE] = acc.astype(rest_ref.dtype)
            continue
        g, kind = divmod(j - QKV_TILE0, 3)
        out_ref, d = q_refs[g], dils[g]
        for c in range(nslab):
            a = acc[:, c * LANES:(c + 1) * LANES]
            if kind < 2:
                a = (a * c_ref[...] + pltpu.roll(a, LANES - ROPE_DIM // 2, 1) * s1_ref[...]
                     + pltpu.roll(a, ROPE_DIM // 2, 1) * s2_ref[...])
            if kind == 0:
                a = a * (ATT_HEAD_DIM ** -0.5 * LOG2E)
            lanes = slice(kind * COL_TILE + c * LANES, kind * COL_TILE + (c + 1) * LANES)
            if win_rows and kind > 0:
                ch0 = (kind - 1) * COL_TILE + c * LANES
                pw_refs[g][ch0:ch0 + LANES, :] = a[tm - win_rows[g]:, :].T
            if d == 1:
                out_ref[0, :, lanes] = a.astype(out_ref.dtype)
            elif d == 4:
                acc_ref[strided, c] = a
                for r in range(4):
                    out_ref[r, :, lanes] = acc_ref[strided, c, pl.ds(r, tm // 4, stride=4), :].astype(out_ref.dtype)
            else:
                acc_ref[strided, c] = a
                q4 = tm // 4
                for r1 in range(4):
                    tmp_ref[kind, c, r1 * q4:(r1 + 1) * q4, :] = acc_ref[strided, c, pl.ds(r1, q4, stride=4), :]
                for r1 in range(4):
                    for r2 in range(4):
                        out_ref[r1 + 4 * r2, :, lanes] = tmp_ref[
                            kind, c, pl.ds(r1 * q4 + r2, tm // 16, stride=4), :].astype(out_ref.dtype)
        if d > 1:
            strided += 1


def _in_proj(x2d, nw, wt, tabs, *, seq, tm, dils, out_dtype, windows=None):
    t = x2d.shape[0]
    nb = t // seq
    tiles_per_seq = seq // tm
    tab_blocks = tabs[0].shape[0] // tm
    n_strided = 3 * sum(1 for d in dils if d > 1)
    nslab = COL_TILE // LANES

    def qkv_spec(d):
        return pl.BlockSpec((None, d, tm // d, 3 * COL_TILE),
                            lambda i: (i // tiles_per_seq, 0, i % tiles_per_seq, 0))

    win_rows, win_specs, win_shapes = None, [], []
    if windows:
        win_rows = tuple(min(tm, w) for w in windows)
        for w, wr in zip(windows, win_rows):
            first = tiles_per_seq - w // wr
            win_specs.append(pl.BlockSpec(
                (None, 2 * COL_TILE, wr),
                lambda i, first=first: (i // tiles_per_seq, 0, jnp.maximum(i % tiles_per_seq - first, 0))))
            win_shapes.append(jax.ShapeDtypeStruct((nb, 2 * COL_TILE, w), F32))

    tab_spec = pl.BlockSpec((tm, LANES), lambda i: (i % tab_blocks, 0))
    resident = lambda shape: pl.BlockSpec(shape, lambda i: (0, 0), pipeline_mode=pl.Buffered(1))
    kern = functools.partial(_in_proj_kernel, tm=tm, dils=dils, win_rows=win_rows)
    n_zx, n_rest = QKV_TILE0 * COL_TILE, (N_COL_TILES - REST_TILE0) * COL_TILE
    return pl.pallas_call(
        kern,
        grid=(t // tm,),
        in_specs=[
            pl.BlockSpec((tm, D_MODEL), lambda i: (i, 0)),
            resident((1, D_MODEL)),
            resident(wt.shape),
            tab_spec, tab_spec, tab_spec,
        ],
        out_specs=[
            pl.BlockSpec((tm, n_zx), lambda i: (i, 0)),
            qkv_spec(dils[0]), qkv_spec(dils[1]), qkv_spec(dils[2]),
            pl.BlockSpec((tm, n_rest), lambda i: (i, 0)),
            pl.BlockSpec((tm, LANES), lambda i: (i, 0)),
        ] + win_specs,
        out_shape=[
            jax.ShapeDtypeStruct((t, n_zx), out_dtype),
            jax.ShapeDtypeStruct((nb, dils[0], seq // dils[0], 3 * COL_TILE), out_dtype),
            jax.ShapeDtypeStruct((nb, dils[1], seq // dils[1], 3 * COL_TILE), out_dtype),
            jax.ShapeDtypeStruct((nb, dils[2], seq // dils[2], 3 * COL_TILE), out_dtype),
            jax.ShapeDtypeStruct((t, n_rest), out_dtype),
            jax.ShapeDtypeStruct((t, LANES), F32),
        ] + win_shapes,
        scratch_shapes=[
            pltpu.VMEM((tm, D_MODEL), BF16),
            pltpu.VMEM((max(n_strided, 1), nslab, tm, LANES), F32),
            pltpu.VMEM((3, nslab, tm, LANES), F32),
        ],
        compiler_params=_cparams(("arbitrary",)),
        name="in_proj",
    )(x2d, nw, wt, *tabs)


def _ssd_small(dt_raw, dtb_row, a_row, dtb_col, a_col, same01, same01_t, tot01, tot01_t):
    dt = _softplus(dt_raw + dtb_row)
    dta = dt * a_row
    acs = _dot_exact_lhs01(same01, dta)
    tot = _dot_exact_lhs01(tot01, dta)
    dtd = dt * jnp.exp2(tot - acs)
    dt_raw_t = dt_raw.T
    dt_t = _softplus(dt_raw_t + dtb_col)
    dta_t = dt_t * a_col
    acs_t = _dot_exact_rhs01(dta_t, same01_t)
    tot_t = _dot_exact_rhs01(dta_t, tot01_t)
    dtd_t = dt_t * jnp.exp2(tot_t - acs_t)
    return dt, acs, tot, dtd, dt_t, acs_t, dtd_t


def _ssd_pairs(x, bm, cm, acs, acs_t, dt_t, dtd_t, mask, st_ref, y_ref_write, st_scale_row, after_pair=None):
    lane = lax.broadcasted_iota(jnp.int32, (1, LANES), 1)
    half = [lane < ATT_HEAD_DIM, lane >= ATT_HEAD_DIM]
    bm_t = [bm[:, g * SSD_D_STATE:(g + 1) * SSD_D_STATE].T for g in range(SSD_N_GROUPS)]
    cms = [cm[:, g * SSD_D_STATE:(g + 1) * SSD_D_STATE] for g in range(SSD_N_GROUPS)]
    cb = [_dot(cms[g].astype(BF16), bm_t[g].astype(BF16)) for g in range(SSD_N_GROUPS)]
    for p in range(SSD_N_HEADS // 2):
        g = p // 2
        xp = x[:, p * LANES:(p + 1) * LANES]
        if st_ref is not None:
            stp = st_ref[:, p * LANES:(p + 1) * LANES]
        y_pair = jnp.zeros((SSD_CHUNK, LANES), F32)
        st_add = jnp.zeros((SSD_D_STATE, LANES), F32)
        for hh in range(2):
            h = 2 * p + hh
            col = acs[:, h:h + 1]
            row = acs_t[h:h + 1, :]
            lmat = jnp.exp2(jnp.where(mask, col - row, NEG_INF))
            m = cb[g] * lmat * dt_t[h:h + 1, :]
            xm = jnp.where(half[hh], xp, 0.0).astype(BF16)
            if st_ref is not None:
                cx = cms[g] * jnp.exp2(col)
                lhs = jnp.concatenate([m, cx], axis=1).astype(BF16)
                rhs = jnp.concatenate([xm, jnp.where(half[hh], stp, 0.0).astype(BF16)], axis=0)
                y_pair = y_pair + _dot(lhs, rhs)
                bt_h = (bm_t[g] * dtd_t[h:h + 1, :]).astype(BF16)
                st_add = st_add + _dot(bt_h, xm)
            else:
                y_pair = y_pair + _dot(m.astype(BF16), xm)
        y_ref_write(p, y_pair)
        if st_ref is not None:
            scale = jnp.where(half[0], st_scale_row[:, 2 * p:2 * p + 1], st_scale_row[:, 2 * p + 1:2 * p + 2])
            st_ref[:, p * LANES:(p + 1) * LANES] = stp * scale + st_add
        if after_pair is not None:
            after_pair(p)


def _ssd_prompt_kernel(z_ref, xs_ref, bc_ref, xsn_ref, bcn_ref, dt_ref, cw_ref, cb_ref, dtb_ref, a_ref, dtbc_ref,
                       ac_ref, dsk_ref, nw_ref, sh_ref, y_ref, ssm_ref, xc_ref, st_ref, ybuf):
    c = pl.program_id(1)
    q = SSD_CHUNK
    slot = c % 2

    def conv_act(prev, cur, lanes):
        shifted = _dot(sh_ref[...], jnp.concatenate([prev, cur], axis=0))
        acc = cb_ref[:, lanes] + cw_ref[SSD_CONV - 1:SSD_CONV, lanes] * cur.astype(F32)
        for s in range(1, SSD_CONV):
            acc = acc + cw_ref[SSD_CONV - 1 - s:SSD_CONV - s, lanes] * shifted[(s - 1) * q:s * q]
        return _silu(acc)

    @pl.when(c == 0)
    def _():
        st_ref[...] = jnp.zeros(st_ref.shape, F32)
        no_history = jnp.zeros((q, D_MODEL), xs_ref.dtype)
        xc_ref[0, :, 0:D_MODEL] = conv_act(no_history, xs_ref[...], slice(0, D_MODEL))
        xc_ref[0, :, D_MODEL:SSD_CONV_DIM] = conv_act(no_history, bc_ref[...], slice(D_MODEL, SSD_CONV_DIM))

    slab = SSD_CONV_DIM // (SSD_N_HEADS // 2)

    def conv_next(p):
        lanes = slice(p * slab, (p + 1) * slab)
        if lanes.start < D_MODEL:
            src = slice(lanes.start, lanes.stop)
            prev, cur = xs_ref[:, src], xsn_ref[:, src]
        else:
            src = slice(lanes.start - D_MODEL, lanes.stop - D_MODEL)
            prev, cur = bc_ref[:, src], bcn_ref[:, src]
        xc_ref[1 - slot, :, lanes] = conv_act(prev, cur, lanes)

    x = xc_ref[slot, :, 0:D_MODEL]
    bm = xc_ref[slot, :, D_MODEL:D_MODEL + 512]
    cm = xc_ref[slot, :, D_MODEL + 512:SSD_CONV_DIM]

    ri = lax.broadcasted_iota(jnp.int32, (q, q), 0)
    ci = lax.broadcasted_iota(jnp.int32, (q, q), 1)
    tril = ri >= ci
    tril01 = jnp.where(tril, 1.0, 0.0).astype(BF16)
    triu01 = jnp.where(ri <= ci, 1.0, 0.0).astype(BF16)
    ones01 = jnp.ones((q, q), BF16)
    dt, acs, tot, dtd, dt_t, acs_t, dtd_t = _ssd_small(
        dt_ref[...], dtb_ref[...], a_ref[...], dtbc_ref[...], ac_ref[...], tril01, triu01, ones01, ones01)
    decay_row = jnp.exp2(tot[0:1, :])

    def write_y(p, y_pair):
        ybuf[:, p * LANES:(p + 1) * LANES] = y_pair

    _ssd_pairs(x, bm, cm, acs, acs_t, dt_t, dtd_t, tril, st_ref, write_y, decay_row, after_pair=conv_next)

    y = (ybuf[...] + dsk_ref[...] * x) * _silu(z_ref[...].astype(F32))
    y_ref[...] = _rms(y, nw_ref[...]).astype(y_ref.dtype)

    @pl.when(c == pl.num_programs(1) - 1)
    def _():
        for p in range(D_MODEL // LANES):
            ssm_ref[p * LANES:(p + 1) * LANES, :] = st_ref[:, p * LANES:(p + 1) * LANES].T


def _ssd_prompt(zx, dt, cw, cb, dtb, a_row, dtb_col, a_col, dsk, nw, *, nbatch, seq):
    nc = seq // SSD_CHUNK
    q = SSD_CHUNK
    row = lambda b, c: b * nc + c
    nxt = lambda b, c: b * nc + jnp.minimum(c + 1, nc - 1)
    vec = lambda n: pl.BlockSpec((1, n), lambda b, c: (0, 0))
    col = pl.BlockSpec((LANES, 1), lambda b, c: (0, 0))
    nsh = SSD_CONV - 1
    out_t = jnp.arange(nsh * q) % q
    shift = 1 + jnp.arange(nsh * q) // q
    shift01 = (jnp.arange(2 * q)[None, :] == (q + out_t - shift)[:, None]).astype(BF16)
    return pl.pallas_call(
        _ssd_prompt_kernel,
        grid=(nbatch, nc),
        in_specs=[
            pl.BlockSpec((q, D_MODEL), lambda b, c: (row(b, c), 0)),
            pl.BlockSpec((q, D_MODEL), lambda b, c: (row(b, c), 1)),
            pl.BlockSpec((q, D_MODEL), lambda b, c: (row(b, c), 2)),
            pl.BlockSpec((q, D_MODEL), lambda b, c: (nxt(b, c), 1)),
            pl.BlockSpec((q, D_MODEL), lambda b, c: (nxt(b, c), 2)),
            pl.BlockSpec((q, LANES), lambda b, c: (row(b, c), 0)),
            pl.BlockSpec((SSD_CONV, SSD_CONV_DIM), lambda b, c: (0, 0)),
            vec(SSD_CONV_DIM), vec(LANES), vec(LANES), col, col, vec(D_MODEL), vec(D_MODEL),
            pl.BlockSpec(((SSD_CONV - 1) * q, 2 * q), lambda b, c: (0, 0)),
        ],
        out_specs=[
            pl.BlockSpec((q, D_MODEL), lambda b, c: (row(b, c), 0)),
            pl.BlockSpec((None, D_MODEL, SSD_D_STATE), lambda b, c: (b, 0, 0)),
        ],
        out_shape=[
            jax.ShapeDtypeStruct((nbatch * seq, D_MODEL), BF16),
            jax.ShapeDtypeStruct((nbatch, D_MODEL, SSD_D_STATE), F32),
        ],
        scratch_shapes=[
            pltpu.VMEM((2, q, SSD_CONV_DIM), F32),
            pltpu.VMEM((SSD_D_STATE, D_MODEL), F32),
            pltpu.VMEM((q, D_MODEL), F32),
        ],
        compiler_params=_cparams(("parallel", "arbitrary")),
        name="ssd_prompt",
    )(zx, zx, zx, zx, zx, dt, cw, cb, dtb, a_row, dtb_col, a_col, dsk, nw, shift01)


def _expand_heads(v, e01):
    return _dot_exact_rhs01(v, e01)


def _ssd_sample_rows_kernel(xs_ref, bc_ref, dt_ref, hist_ref, cw_ref, cb_ref, dtb_ref, a_ref, dtbc_ref, ac_ref,
                            dsk_ref, e01_ref,
                            ypart_ref, ea_ref, xdt_ref, bm_ref, cm_ref, cd_ref, ybuf, *, tdec):
    q = SSD_CHUNK
    x_raw = jnp.concatenate([xs_ref[...], bc_ref[...]], axis=1)
    hist = hist_ref[...]
    rowi = lax.broadcasted_iota(jnp.int32, (q, 1), 0)
    tpos = rowi % tdec
    acc = cb_ref[...] + cw_ref[SSD_CONV - 1:SSD_CONV, :] * x_raw
    for s in range(1, SSD_CONV):
        prev = jnp.where(tpos >= s, pltpu.roll(x_raw, s, 0), pltpu.roll(hist, q - tdec + s, 0))
        acc = acc + cw_ref[SSD_CONV - 1 - s:SSD_CONV - s, :] * prev
    xc = _silu(acc)
    x = xc[:, :D_MODEL]
    bm = xc[:, D_MODEL:D_MODEL + 512]
    cm = xc[:, D_MODEL + 512:]

    ri = lax.broadcasted_iota(jnp.int32, (q, q), 0)
    ci = lax.broadcasted_iota(jnp.int32, (q, q), 1)
    same = (ri // tdec) == (ci // tdec)
    mask = same & (ri >= ci)
    low01 = jnp.where(mask, 1.0, 0.0).astype(BF16)
    up01 = jnp.where(same & (ri <= ci), 1.0, 0.0).astype(BF16)
    same01 = jnp.where(same, 1.0, 0.0).astype(BF16)
    dt, acs, tot, dtd, dt_t, acs_t, dtd_t = _ssd_small(
        dt_ref[...], dtb_ref[...], a_ref[...], dtbc_ref[...], ac_ref[...], low01, up01, same01, same01)

    def write_y(p, y_pair):
        ybuf[:, p * LANES:(p + 1) * LANES] = y_pair

    _ssd_pairs(x, bm, cm, acs, acs_t, dt_t, dtd_t, mask, None, write_y, None)

    e01 = e01_ref[...]
    ypart_ref[...] = ybuf[...] + dsk_ref[...] * x
    ea_ref[...] = _expand_heads(jnp.exp2(acs), e01)
    xd = x * _expand_heads(dtd, e01)
    for p in range(D_MODEL // LANES):
        xdt_ref[p * LANES:(p + 1) * LANES, :] = xd[:, p * LANES:(p + 1) * LANES].T.astype(BF16)
    bm_ref[...] = bm.astype(BF16)
    cm_ref[...] = cm.astype(BF16)
    cd_ref[...] = jnp.exp2(tot)


def _ssd_sample_rows(zx, dt, hist, cw, cb, dtb, a_row, dtb_col, a_col, dsk, e01, *, tdec):
    t = zx.shape[0]
    q = SSD_CHUNK
    vec = lambda n: pl.BlockSpec((1, n), lambda i: (0, 0))
    col = pl.BlockSpec((LANES, 1), lambda i: (0, 0))
    kern = functools.partial(_ssd_sample_rows_kernel, tdec=tdec)
    return pl.pallas_call(
        kern,
        grid=(t // q,),
        in_specs=[
            pl.BlockSpec((q, D_MODEL), lambda i: (i, 1)),
            pl.BlockSpec((q, D_MODEL), lambda i: (i, 2)),
            pl.BlockSpec((q, LANES), lambda i: (i, 0)),
            pl.BlockSpec((q, SSD_CONV_DIM), lambda i: (i, 0)),
            pl.BlockSpec((SSD_CONV, SSD_CONV_DIM), lambda i: (0, 0)),
            vec(SSD_CONV_DIM), vec(LANES), vec(LANES), col, col, vec(D_MODEL),
            pl.BlockSpec((LANES, D_MODEL), lambda i: (0, 0)),
        ],
        out_specs=[
            pl.BlockSpec((q, D_MODEL), lambda i: (i, 0)),
            pl.BlockSpec((q, D_MODEL), lambda i: (i, 0)),
            pl.BlockSpec((D_MODEL, q), lambda i: (0, i)),
            pl.BlockSpec((q, 512), lambda i: (i, 0)),
            pl.BlockSpec((q, 512), lambda i: (i, 0)),
            pl.BlockSpec((q, LANES), lambda i: (i, 0)),
        ],
        out_shape=[
            jax.ShapeDtypeStruct((t, D_MODEL), F32),
            jax.ShapeDtypeStruct((t, D_MODEL), F32),
            jax.ShapeDtypeStruct((D_MODEL, t), BF16),
            jax.ShapeDtypeStruct((t, 512), BF16),
            jax.ShapeDtypeStruct((t, 512), BF16),
            jax.ShapeDtypeStruct((t, LANES), F32),
        ],
        scratch_shapes=[pltpu.VMEM((q, D_MODEL), F32)],
        compiler_params=_cparams(("parallel",)),
        name="ssd_sample_rows",
    )(zx, zx, dt, hist, cw, cb, dtb, a_row, dtb_col, a_col, dsk, e01)


def _ssd_sample_state_kernel(cd_ref, z_ref, ypart_ref, ea_ref, xdt_ref, bm_ref, cm_ref, st_ref, nw_ref,
                             y_ref, sto_ref, *, tdec, gseq):
    i = pl.program_id(0)
    q = SSD_CHUNK
    rows = gseq * tdec
    steps_per_blk = q // rows
    lane = lax.broadcasted_iota(jnp.int32, (1, q), 1)
    rg = 16
    rowi = lax.broadcasted_iota(jnp.int32, (rg, 1), 0)
    lane_base = (i % steps_per_blk) * rows
    seq_per_rg = rg // tdec
    yoff_groups = []
    for q8 in range(rows // rg):
        yoff8 = jnp.zeros((rg, D_MODEL), F32)
        for sj in range(seq_per_rg):
            jj = q8 * seq_per_rg + sj
            b = i * gseq + jj
            lmask = (lane >= lane_base + jj * tdec) & (lane < lane_base + (jj + 1) * tdec)
            parts = []
            for g in range(SSD_N_GROUPS):
                s_g = st_ref[jj, g * 256:(g + 1) * 256, :]
                c8 = cm_ref[q8 * rg:(q8 + 1) * rg, g * SSD_D_STATE:(g + 1) * SSD_D_STATE]
                parts.append(_dot_nt(c8, s_g.astype(BF16)))
                lhs = jnp.where(lmask, xdt_ref[g * 256:(g + 1) * 256, :], jnp.zeros((), BF16))
                add = _dot(lhs, bm_ref[:, g * SSD_D_STATE:(g + 1) * SSD_D_STATE])
                for hh in range(4):
                    h = 4 * g + hh
                    sl = slice(hh * SSD_HEAD_DIM, (hh + 1) * SSD_HEAD_DIM)
                    sto_ref[jj, g * 256 + hh * 64:g * 256 + (hh + 1) * 64, :] = (
                        s_g[sl, :] * cd_ref[b, h] + add[sl, :])
            yo = jnp.concatenate(parts, axis=1)
            in_seq = (rowi >= sj * tdec) & (rowi < (sj + 1) * tdec)
            yoff8 = jnp.where(in_seq, yo, yoff8)
        yoff_groups.append(yoff8)
    yoff = jnp.concatenate(yoff_groups, axis=0) if len(yoff_groups) > 1 else yoff_groups[0]
    y = (ypart_ref[...] + yoff * ea_ref[...]) * _silu(z_ref[...])
    y_ref[...] = _rms(y, nw_ref[...])


def _ssd_sample_state(cd, zx, ypart, ea, xdt, bmb, cmb, state, nw, *, tdec, gseq):
    t = zx.shape[0]
    nseq = t // tdec
    q = SSD_CHUNK
    rows = gseq * tdec
    spb = q // rows
    kern = functools.partial(_ssd_sample_state_kernel, tdec=tdec, gseq=gseq)
    return pl.pallas_call(
        kern,
        grid=(nseq // gseq,),
        in_specs=[
            pl.BlockSpec(memory_space=pltpu.SMEM),
            pl.BlockSpec((rows, D_MODEL), lambda i: (i, 0)),
            pl.BlockSpec((rows, D_MODEL), lambda i: (i, 0)),
            pl.BlockSpec((rows, D_MODEL), lambda i: (i, 0)),
            pl.BlockSpec((D_MODEL, q), lambda i: (0, i // spb)),
            pl.BlockSpec((q, 512), lambda i: (i // spb, 0)),
            pl.BlockSpec((rows, 512), lambda i: (i, 0)),
            pl.BlockSpec((gseq, D_MODEL, SSD_D_STATE), lambda i: (i, 0, 0)),
            pl.BlockSpec((1, D_MODEL), lambda i: (0, 0)),
        ],
        out_specs=[
            pl.BlockSpec((rows, D_MODEL), lambda i: (i, 0)),
            pl.BlockSpec((gseq, D_MODEL, SSD_D_STATE), lambda i: (i, 0, 0)),
        ],
        out_shape=[
            jax.ShapeDtypeStruct((t, D_MODEL), F32),
            jax.ShapeDtypeStruct((nseq, D_MODEL, SSD_D_STATE), F32),
        ],
        compiler_params=_cparams(("parallel",)),
        name="ssd_sample_state",
    )(cd, zx, ypart, ea, xdt, bmb, cmb, state, nw)


def _lse_lane(h):
    return 16 * h


def _attn_prompt_kernel(q_ref, kc_ref, kp_ref, vc_ref, vp_ref, o_ref, lse_ref, *, nq):
    n = pl.program_id(2)
    nk = 128
    k = jnp.concatenate([kp_ref[...], kc_ref[...]], axis=0)
    v = jnp.concatenate([vp_ref[...], vc_ref[...]], axis=0)
    lane = lax.broadcasted_iota(jnp.int32, (1, LANES), 1)
    lo = lane < ATT_HEAD_DIM
    qi = lax.broadcasted_iota(jnp.int32, (2 * nk, 2 * nk), 0) % nk
    kj = lax.broadcasted_iota(jnp.int32, (2 * nk, 2 * nk), 1)
    in_band = (kj >= qi) & (kj <= qi + nk)
    band = jnp.where(in_band, 0.0, NEG_INF)
    band_first = jnp.where(in_band & ((kj >= nk) | (n > 0)), 0.0, NEG_INF)
    q = q_ref[...]
    zero = jnp.zeros((), q.dtype)
    npairs = ATT_HEADS // 2
    pair_lanes = [slice(p * LANES, (p + 1) * LANES) for p in range(npairs)]
    units = [(j, p) for j in range(nq) for p in range(npairs)]

    def scores(u):
        j, p = u
        qp = q[j * nk:(j + 1) * nk, pair_lanes[p]]
        q2 = jnp.concatenate([jnp.where(lo, qp, zero), jnp.where(lo, zero, qp)], axis=0)
        return _dot_nt(q2, k[j * nk:(j + 2) * nk, pair_lanes[p]]) + (band_first if j == 0 else band)

    def softmax(s):
        m = jnp.max(s, axis=-1, keepdims=True)
        e = jnp.exp2(s - m)
        den = jnp.sum(e, axis=-1, keepdims=True)
        return e.astype(BF16), den, (m + jnp.log2(den)) * LN2

    ahead = 3
    s_next = [scores(u) for u in units[:ahead]]
    lse_full, o_parts = None, []
    for i, (j, p) in enumerate(units):
        e, den, lse = softmax(s_next.pop(0))
        if i + ahead < len(units):
            s_next.append(scores(units[i + ahead]))
        r = _dot(e, v[j * nk:(j + 2) * nk, pair_lanes[p]]) / den
        o_parts.append(jnp.where(lo, r[0:nk], r[nk:2 * nk]).astype(o_ref.dtype))
        if p == 0:
            lse_full = jnp.zeros((nk, LANES), F32)
        for hh in range(2):
            base = _lse_lane(2 * p + hh)
            lse_full = jnp.where((lane >= base) & (lane < base + 16), lse[hh * nk:(hh + 1) * nk], lse_full)
        if p == npairs - 1:
            o_ref[j * nk:(j + 1) * nk, :] = jnp.concatenate(o_parts, axis=1)
            lse_ref[j * nk:(j + 1) * nk, :] = lse_full
            o_parts = []


def _attn_prompt(qkv, *, d):
    nb, _, length, _ = qkv.shape
    nk = 128
    nq = next(c for c in (4, 2, 1) if length % (c * nk) == 0)
    nblk = length // (nq * nk)
    cur = lambda which: pl.BlockSpec((None, None, nq * nk, COL_TILE), lambda b, r, n: (b, r, n, which))
    prev = lambda which: pl.BlockSpec((None, None, nk, COL_TILE),
                                      lambda b, r, n: (b, r, jnp.maximum(nq * n - 1, 0), which))
    return pl.pallas_call(
        functools.partial(_attn_prompt_kernel, nq=nq),
        grid=(nb, d, nblk),
        in_specs=[cur(0), cur(1), prev(1), cur(2), prev(2)],
        out_specs=[
            pl.BlockSpec((None, None, nq * nk, ATT_WIDTH), lambda b, r, n: (b, r, n, 0)),
            pl.BlockSpec((None, None, nq * nk, LANES), lambda b, r, n: (b, r, n, 0)),
        ],
        out_shape=[
            jax.ShapeDtypeStruct((nb, d, length, ATT_WIDTH), BF16),
            jax.ShapeDtypeStruct((nb, d, length, LANES), F32),
        ],
        compiler_params=_cparams(("parallel", "parallel", "arbitrary")),
        name=f"attn_prompt_d{d}",
    )(qkv, qkv, qkv, qkv, qkv)


def _attn_sample_kernel(q0_ref, q1_ref, q2_ref, qm_ref, c0_ref, c1_ref, c2_ref, cm_ref, oa_ref, om_ref, *, tdec):
    nh = ATT_HEADS
    rows8 = 8
    j = pl.program_id(1)
    r64 = nh * rows8
    rowi = lax.broadcasted_iota(jnp.int32, (r64, 1), 0)
    head_of_row = rowi // rows8
    r8 = rowi % rows8
    seq_q = r8 // tdec
    t_q = r8 % tdec
    lane512 = lax.broadcasted_iota(jnp.int32, (1, ATT_WIDTH), 1)
    headmask = (lane512 // ATT_HEAD_DIM) == head_of_row
    col8 = lax.broadcasted_iota(jnp.int32, (1, rows8), 1)
    seq_k = col8 // tdec
    t_k = col8 % tdec

    outs, lses = [], []
    for g, (q_ref, c_ref, d) in enumerate(zip((q0_ref, q1_ref, q2_ref), (c0_ref, c1_ref, c2_ref), ATT_DILATIONS)):
        w = c_ref.shape[-1]
        qkv = q_ref[...]
        qv = qkv[:, :ATT_WIDTH]
        kn = qkv[:, ATT_WIDTH:2 * ATT_WIDTH]
        vn = qkv[:, 2 * ATT_WIDTH:]
        q64 = jnp.where(headmask, jnp.concatenate([qv] * nh, axis=0), 0.0)
        s_new = _dot_nt(q64, kn)
        if d == 1:
            valid_new = (seq_k == seq_q) & (t_k <= t_q)
        else:
            valid_new = (seq_k == seq_q) & (t_k == t_q)
        s_new = jnp.where(valid_new, s_new, NEG_INF)
        wi = lax.broadcasted_iota(jnp.int32, (1, w), 1)
        valid_c = (wi >= t_q) if d == 1 else ((wi % d) == t_q)
        s_c = jnp.where(valid_c, _dot(q64.astype(BF16), c_ref[0].astype(BF16)), NEG_INF)
        m = jnp.maximum(jnp.max(s_c, axis=-1, keepdims=True), jnp.max(s_new, axis=-1, keepdims=True))
        e_c = jnp.exp2(s_c - m)
        e_n = jnp.exp2(s_new - m)
        den = jnp.sum(e_c, axis=-1, keepdims=True) + jnp.sum(e_n, axis=-1, keepdims=True)
        o = _dot(e_n, vn) + _dot_nt(e_c.astype(BF16), c_ref[1].astype(BF16))
        outs.append(o / den)
        lses.append((m + jnp.log2(den)) * LN2)

    mx = jnp.maximum(jnp.maximum(lses[0], lses[1]), lses[2])
    ws = [jnp.exp(l - mx) for l in lses]
    om = (ws[0] * outs[0] + ws[1] * outs[1] + ws[2] * outs[2]) / (ws[0] + ws[1] + ws[2])
    om = jnp.where(headmask, om, 0.0)
    o8 = om[0:rows8]
    for h in range(1, nh):
        o8 = o8 + om[h * rows8:(h + 1) * rows8]

    qm = jnp.concatenate([qm_ref[...] * (MEM_HEAD_DIM ** -0.5), jnp.zeros((rows8, MEM_WIDTH), F32)], axis=0)
    mparts = []
    for h in range(MEM_HEADS):
        kh = cm_ref[pl.ds(h, MEM_LEN, stride=2 * MEM_HEADS), :].astype(BF16)
        vh = cm_ref[pl.ds(MEM_HEADS + h, MEM_LEN, stride=2 * MEM_HEADS), :].astype(BF16)
        s = _dot_nt(qm[:, h * MEM_HEAD_DIM:(h + 1) * MEM_HEAD_DIM].astype(BF16), kh)
        mm = jnp.max(s, axis=-1, keepdims=True)
        e = jnp.exp(s - mm)
        den = jnp.sum(e, axis=-1, keepdims=True)
        mparts.append((_dot(e.astype(BF16), vh) / den)[0:rows8])
    m8 = jnp.concatenate(mparts, axis=1)

    row8 = lax.broadcasted_iota(jnp.int32, (rows8, 1), 0)
    mine = (row8 // tdec) == j

    @pl.when(j == 0)
    def _():
        oa_ref[...] = jnp.where(mine, o8, 0.0)
        om_ref[...] = jnp.where(mine, m8, 0.0)

    @pl.when(j > 0)
    def _():
        oa_ref[...] = jnp.where(mine, o8, oa_ref[...])
        om_ref[...] = jnp.where(mine, m8, om_ref[...])


def _attn_sample(q0, q1, q2, rest, c0, c1, c2, cmem, *, tdec):
    t = q0.shape[0]
    spg = 8 // tdec
    row_spec = pl.BlockSpec((8, 3 * COL_TILE), lambda i, j: (i, 0))
    cache_spec = lambda c: pl.BlockSpec((None,) + c.shape[1:], lambda i, j: (i * spg + j,) + (0,) * (c.ndim - 1))
    kern = functools.partial(_attn_sample_kernel, tdec=tdec)
    return pl.pallas_call(
        kern,
        grid=(t // 8, spg),
        in_specs=[
            row_spec, row_spec, row_spec,
            pl.BlockSpec((8, COL_TILE), lambda i, j: (i, 1)),
            cache_spec(c0), cache_spec(c1), cache_spec(c2), cache_spec(cmem),
        ],
        out_specs=[
            pl.BlockSpec((8, ATT_WIDTH), lambda i, j: (i, 0)),
            pl.BlockSpec((8, MEM_WIDTH), lambda i, j: (i, 0)),
        ],
        out_shape=[
            jax.ShapeDtypeStruct((t, ATT_WIDTH), F32),
            jax.ShapeDtypeStruct((t, MEM_WIDTH), F32),
        ],
        compiler_params=_cparams(("parallel", "arbitrary")),
        name="attn_sample",
    )(q0, q1, q2, rest, c0, c1, c2, cmem)


def _final_tail(x, p_ssd, y_att, y_mem, gate_raw, watt_ref, wmem_ref, wout_ref, nf_ref):
    gates = _sigmoid(gate_raw)
    merged = (gates[:, 0:D_MODEL] * p_ssd
              + gates[:, D_MODEL:2 * D_MODEL] * _dot(y_att.astype(BF16), watt_ref[...])
              + gates[:, 2 * D_MODEL:] * _dot(y_mem.astype(BF16), wmem_ref[...]))
    return _rms(x + _dot(merged.astype(BF16), wout_ref[...]), nf_ref[...])


def _final_prompt_kernel(x_ref, ys_ref, o0_ref, o1_ref, o2_ref, l0_ref, l1_ref, l2_ref, rest_ref, mkv_ref,
                         wssd_ref, watt_ref, wmem_ref, wout_ref, nf_ref, e16_ref,
                         y_ref, obuf, lbuf, *, tm, dils, nsub):
    nslab = ATT_WIDTH // LANES
    o_refs, l_refs = (o0_ref, o1_ref, o2_ref), (l0_ref, l1_ref, l2_ref)
    for g, (o_ref, l_ref, d) in enumerate(zip(o_refs, l_refs, dils)):
        if d == 1:
            continue
        for r in range(d):
            blk = o_ref[r].astype(F32)
            for c in range(nslab):
                obuf[g, c, pl.ds(r, tm // d, stride=d), :] = blk[:, c * LANES:(c + 1) * LANES]
            lbuf[g, pl.ds(r, tm // d, stride=d), :] = l_ref[r]
    e16 = e16_ref[...]

    for rs in range(nsub):
        rows = slice(rs * (tm // nsub), (rs + 1) * (tm // nsub))
        on, ln = [], []
        for g in range(3):
            if dils[g] == 1:
                on.append(o_refs[g][0, rows, :].astype(F32))
                ln.append(l_refs[g][0, rows, :])
            else:
                on.append(jnp.concatenate([obuf[g, c, rows, :] for c in range(nslab)], axis=1))
                ln.append(lbuf[g, rows, :])
        mx = jnp.maximum(jnp.maximum(ln[0], ln[1]), ln[2])
        ws = [jnp.exp(l - mx) for l in ln]
        wsum = ws[0] + ws[1] + ws[2]
        o_att = jnp.zeros((tm // nsub, ATT_WIDTH), F32)
        for g in range(3):
            hi, mid, _ = _split3(ws[g] / wsum)
            o_att = o_att + _dot(jnp.concatenate([hi, mid], axis=1), e16) * on[g]

        rest = rest_ref[rows, :]
        z_att = rest[:, 0:512].astype(F32)
        q_mem = rest[:, 512:1024]
        z_mem = rest[:, 1024:1536].astype(F32)
        gate_raw = rest[:, 1536:].astype(F32)
        y_att = o_att * _silu(z_att)

        outs = []
        for h in range(MEM_HEADS):
            lanes = slice(h * MEM_HEAD_DIM, (h + 1) * MEM_HEAD_DIM)
            vh = mkv_ref[:, MEM_WIDTH + h * MEM_HEAD_DIM:MEM_WIDTH + (h + 1) * MEM_HEAD_DIM]
            s = _dot_nt(q_mem[:, lanes], mkv_ref[:, lanes]) * (MEM_HEAD_DIM ** -0.5)
            m = jnp.max(s, axis=-1, keepdims=True)
            e = jnp.exp(s - m)
            den = jnp.sum(e, axis=-1, keepdims=True)
            outs.append(_dot(e.astype(BF16), vh) / den)
        y_mem = jnp.concatenate(outs, axis=1) * _silu(z_mem)

        p_ssd = _dot(ys_ref[rows, :], wssd_ref[...])
        y_ref[rows, :] = _final_tail(x_ref[rows, :], p_ssd, y_att, y_mem, gate_raw,
                                     watt_ref, wmem_ref, wout_ref, nf_ref)


def _final_prompt(x2d, y_ssd, o_g, l_g, rest, mkv_bf, wssd, watt, wmem, wout, nf, e16, *, seq, tm, dils, nsub):
    t = x2d.shape[0]
    tps = seq // tm

    def res_spec(d, width):
        return pl.BlockSpec((None, d, tm // d, width), lambda i: (i // tps, 0, i % tps, 0))

    full = lambda a: pl.BlockSpec(a.shape, lambda i: (0,) * a.ndim)
    kern = functools.partial(_final_prompt_kernel, tm=tm, dils=dils, nsub=nsub)
    return pl.pallas_call(
        kern,
        grid=(t // tm,),
        in_specs=[
            pl.BlockSpec((tm, D_MODEL), lambda i: (i, 0)),
            pl.BlockSpec((tm, D_MODEL), lambda i: (i, 0)),
            res_spec(dils[0], ATT_WIDTH), res_spec(dils[1], ATT_WIDTH), res_spec(dils[2], ATT_WIDTH),
            res_spec(dils[0], LANES), res_spec(dils[1], LANES), res_spec(dils[2], LANES),
            pl.BlockSpec((tm, rest.shape[1]), lambda i: (i, 0)),
            pl.BlockSpec((None, MEM_LEN, 2 * MEM_WIDTH), lambda i: (i // tps, 0, 0)),
            full(wssd), full(watt), full(wmem), full(wout), full(nf), full(e16),
        ],
        out_specs=pl.BlockSpec((tm, D_MODEL), lambda i: (i, 0)),
        out_shape=jax.ShapeDtypeStruct((t, D_MODEL), F32),
        scratch_shapes=[
            pltpu.VMEM((3, ATT_WIDTH // LANES, tm, LANES), F32),
            pltpu.VMEM((3, tm, LANES), F32),
        ],
        compiler_params=_cparams(("parallel",)),
        name="final_prompt",
    )(x2d, y_ssd, *o_g, *l_g, rest, mkv_bf, wssd, watt, wmem, wout, nf, e16)


def _final_sample_kernel(x_ref, ys_ref, oa_ref, om_ref, rest_ref,
                         wssd_ref, watt_ref, wmem_ref, wout_ref, nf_ref, y_ref):
    p_ssd = _dot(ys_ref[...].astype(BF16), wssd_ref[...])
    rest = rest_ref[...]
    y_att = oa_ref[...] * _silu(rest[:, 0:512])
    y_mem = om_ref[...] * _silu(rest[:, 1024:1536])
    y_ref[...] = _final_tail(x_ref[...], p_ssd, y_att, y_mem, rest[:, 1536:], watt_ref, wmem_ref, wout_ref, nf_ref)


def _final_sample(x2d, y_ssd, o_att, o_mem, rest, wssd, watt, wmem, wout, nf, *, tm):
    t = x2d.shape[0]
    full = lambda a: pl.BlockSpec(a.shape, lambda i: (0,) * a.ndim)
    rows = lambda w: pl.BlockSpec((tm, w), lambda i: (i, 0))
    return pl.pallas_call(
        _final_sample_kernel,
        grid=(t // tm,),
        in_specs=[rows(D_MODEL), rows(D_MODEL), rows(ATT_WIDTH), rows(MEM_WIDTH), rows(rest.shape[1]),
                  full(wssd), full(watt), full(wmem), full(wout), full(nf)],
        out_specs=rows(D_MODEL),
        out_shape=jax.ShapeDtypeStruct((t, D_MODEL), F32),
        compiler_params=_cparams(("parallel",)),
        name="final_sample",
    )(x2d, y_ssd, o_att, o_mem, rest, wssd, watt, wmem, wout, nf)


def kernel(x_prompt, x_sample, mem_prompt, cache_win128_kv, cache_win512_kv, cache_win2048_kv, cache_mem_kv,
           state_conv, state_ssm, norm_in_w, w_in, conv_w, conv_b, dt_bias, a_log, d_skip, ssd_norm_w,
           mem_norm_w, w_mem_kv, w_br_ssd, w_br_att, w_br_mem, w_out, norm_f_w):
    bp, seq, _ = x_prompt.shape
    bs, tdec, _ = x_sample.shape
    dils = ATT_DILATIONS

    row = lambda v: v.reshape(1, -1).astype(F32)
    wt = w_in.T.astype(BF16)
    pad_heads = lambda v: jnp.pad(v.astype(F32), (0, LANES - SSD_N_HEADS)).reshape(1, LANES)
    dtb = pad_heads(dt_bias)
    a_row = jnp.pad(-jnp.exp(a_log.astype(F32)) * LOG2E, (0, LANES - SSD_N_HEADS)).reshape(1, LANES)
    dtb_col, a_col = dtb.reshape(LANES, 1), a_row.reshape(LANES, 1)
    dsk = jnp.repeat(d_skip.astype(F32), SSD_HEAD_DIM).reshape(1, D_MODEL)
    cw = conv_w.astype(F32)
    cb = row(conv_b)
    nin, nssd, nmem, nf = row(norm_in_w), row(ssd_norm_w), row(mem_norm_w), row(norm_f_w)
    wssd, watt, wmem, wout = (w.astype(BF16) for w in (w_br_ssd, w_br_att, w_br_mem, w_out))
    lane = jnp.arange(LANES)
    e_ssd = (lane[:, None] == (jnp.arange(D_MODEL)[None, :] // SSD_HEAD_DIM)).astype(BF16)
    lse_lane_of_ch = jnp.repeat(jnp.array([_lse_lane(h) for h in range(ATT_HEADS)]), ATT_HEAD_DIM)
    e_att = (lane[:, None] == lse_lane_of_ch[None, :]).astype(BF16)
    e_att = jnp.concatenate([e_att, e_att], axis=0)

    xp2 = x_prompt.reshape(bp * seq, D_MODEL)
    mkv_f32, mkv_bf = _mem_kv(mem_prompt.reshape(bp * MEM_LEN, D_MODEL), nmem, w_mem_kv.astype(BF16))
    tm_p = 256
    tabs_p = _rope_tables(jnp.arange(seq))
    windows = tuple(min(w, seq) for w in ATT_WINDOWS)
    zx, q0, q1, q2, rest, dtr, kv0, kv1, kv2 = _in_proj(
        xp2, nin, wt, tabs_p, seq=seq, tm=tm_p, dils=dils, out_dtype=BF16, windows=windows)
    y_ssd, p_ssm = _ssd_prompt(zx, dtr, cw, cb, dtb, a_row, dtb_col, a_col, dsk, nssd, nbatch=bp, seq=seq)
    o_g, l_g = [], []
    for qkv, d in zip((q0, q1, q2), dils):
        o, l = _attn_prompt(qkv, d=d)
        o_g.append(o)
        l_g.append(l)
    y_prompt = _final_prompt(xp2, y_ssd, o_g, l_g, rest, mkv_bf.reshape(bp, MEM_LEN, 2 * MEM_WIDTH),
                             wssd, watt, wmem, wout, nf, e_att, seq=seq, tm=512, dils=dils, nsub=1)
    y_prompt = y_prompt.reshape(bp, seq, D_MODEL)

    p_win = [kv.reshape(bp, 2, ATT_HEADS, ATT_HEAD_DIM, kv.shape[-1]).transpose(0, 4, 1, 2, 3)
             for kv in (kv0, kv1, kv2)]
    p_mem_kv = mkv_f32.reshape(bp, MEM_LEN, 2, MEM_HEADS, MEM_HEAD_DIM)
    p_conv = zx.reshape(bp, seq, -1)[:, seq - (SSD_CONV - 1):, D_MODEL:].astype(F32)
    p_ssm = p_ssm.reshape(bp, SSD_N_HEADS, SSD_HEAD_DIM, SSD_D_STATE)

    ts = bs * tdec
    pos_s = PAST_LEN + jnp.tile(jnp.arange(tdec), bs)
    xs2 = x_sample.reshape(ts, D_MODEL)
    tabs_s = _rope_tables(pos_s)
    zx_s, q0_s, q1_s, q2_s, rest_s, dt_s = _in_proj(
        xs2, nin, wt, tabs_s, seq=ts, tm=min(ts, 128), dils=(1, 1, 1), out_dtype=F32)
    q_s = [a.reshape(ts, 3 * COL_TILE) for a in (q0_s, q1_s, q2_s)]
    hist = jnp.pad(state_conv.astype(F32), ((0, 0), (tdec - (SSD_CONV - 1), 0), (0, 0))).reshape(ts, SSD_CONV_DIM)
    ypart, ea, xdt, bmb, cmb, cd = _ssd_sample_rows(
        zx_s, dt_s, hist, cw, cb, dtb, a_row, dtb_col, a_col, dsk, e_ssd, tdec=tdec)
    cd_seq = cd.reshape(bs, tdec, LANES)[:, 0, :SSD_N_HEADS]
    y_ssd_s, s_ssm = _ssd_sample_state(cd_seq, zx_s, ypart, ea, xdt, bmb, cmb,
                                       state_ssm.reshape(bs, D_MODEL, SSD_D_STATE), nssd, tdec=tdec, gseq=8)
    to_kt = lambda c: c.transpose(0, 2, 3, 4, 1).reshape(bs, 2, ATT_WIDTH, c.shape[1])
    c0, c1, c2 = to_kt(cache_win128_kv), to_kt(cache_win512_kv), to_kt(cache_win2048_kv)
    cmem = cache_mem_kv.reshape(bs, MEM_LEN * 2 * MEM_HEADS, MEM_HEAD_DIM)
    o_att_s, o_mem_s = _attn_sample(*q_s, rest_s, c0, c1, c2, cmem, tdec=tdec)
    y_sample = _final_sample(xs2, y_ssd_s, o_att_s, o_mem_s, rest_s, wssd, watt, wmem, wout, nf, tm=min(ts, 256))
    y_sample = y_sample.reshape(bs, tdec, D_MODEL)

    s_win = [a[:, COL_TILE:].reshape(bs, tdec, 2, ATT_HEADS, ATT_HEAD_DIM) for a in q_s]
    s_conv = zx_s.reshape(bs, tdec, -1)[:, tdec - (SSD_CONV - 1):, D_MODEL:]
    s_ssm = s_ssm.reshape(bs, SSD_N_HEADS, SSD_HEAD_DIM, SSD_D_STATE)

    return (y_prompt, y_sample, p_win[0], p_win[1], p_win[2], p_mem_kv, p_conv, p_ssm,
            s_win[0], s_win[1], s_win[2], s_conv, s_ssm)
```

```python
import functools
import math

import jax
import jax.numpy as jnp
from jax import lax
from jax.experimental import pallas as pl
from jax.experimental.pallas import tpu as pltpu

F32 = jnp.float32
BF16 = jnp.bfloat16

D_MODEL = 1024
NORM_EPS = 1e-6
SSD_HEAD_DIM = 64
SSD_N_HEADS = 16
SSD_N_GROUPS = 4
SSD_D_STATE = 128
SSD_CONV = 4
SSD_CHUNK = 128
SSD_CONV_DIM = 2048
ATT_WINDOWS = (128, 512, 2048)
ATT_DILATIONS = (1, 4, 16)
ATT_HEADS = 8
ATT_HEAD_DIM = 64
ATT_WIDTH = 512
ROPE_DIM = 16
ROPE_THETA = 500000.0
MEM_LEN = 256
MEM_HEADS = 4
MEM_HEAD_DIM = 128
MEM_WIDTH = 512
NEG_INF = -1e30
PAST_LEN = 8192

LANES = 128
COL_TILE = 512
N_COL_TILES = 24
DT_COL0 = 3072
QKV_TILE0 = 6
REST_TILE0 = 15
VMEM_LIMIT = 56 * 1024 * 1024


def _cparams(sem):
    return pltpu.CompilerParams(dimension_semantics=sem, vmem_limit_bytes=VMEM_LIMIT)


def _dot(a, b):
    return jnp.dot(a, b, preferred_element_type=F32)


def _dot_nt(a, b):
    return lax.dot_general(a, b, (((1,), (1,)), ((), ())), preferred_element_type=F32)


def _rms(x, w):
    return x * lax.rsqrt(jnp.mean(x * x, axis=-1, keepdims=True) + NORM_EPS) * w


def _sigmoid(x):
    return 0.5 * jnp.tanh(0.5 * x) + 0.5


def _silu(x):
    u = 0.5 * x
    return u * jnp.tanh(u) + u


LOG2E = 1.4426950408889634
LN2 = 0.6931471805599453


def _softplus(x):
    return jnp.maximum(x, 0.0) + jnp.log(1.0 + jnp.exp(-jnp.abs(x)))


def _split3(x):
    hi = x.astype(BF16)
    r1 = x - hi.astype(F32)
    mid = r1.astype(BF16)
    lo = (r1 - mid.astype(F32)).astype(BF16)
    return hi, mid, lo


def _dot_exact_lhs01(m01, x):
    hi, mid, lo = _split3(x)
    return _dot(m01, hi) + _dot(m01, mid) + _dot(m01, lo)


def _dot_exact_rhs01(x, m01):
    hi, mid, lo = _split3(x)
    return _dot(hi, m01) + _dot(mid, m01) + _dot(lo, m01)


def _mem_kv_kernel(x_ref, nw_ref, w_ref, o_ref, ob_ref):
    h = _rms(x_ref[...], nw_ref[...]).astype(BF16)
    acc = _dot(h, w_ref[...])
    o_ref[...] = acc
    ob_ref[...] = acc.astype(BF16)


def _mem_kv(mem2d, nw, w_bf):
    m = mem2d.shape[0]
    tm = 256
    return pl.pallas_call(
        _mem_kv_kernel,
        grid=(m // tm,),
        in_specs=[
            pl.BlockSpec((tm, D_MODEL), lambda i: (i, 0)),
            pl.BlockSpec((1, D_MODEL), lambda i: (0, 0)),
            pl.BlockSpec((D_MODEL, 2 * MEM_WIDTH), lambda i: (0, 0)),
        ],
        out_specs=[
            pl.BlockSpec((tm, 2 * MEM_WIDTH), lambda i: (i, 0)),
            pl.BlockSpec((tm, 2 * MEM_WIDTH), lambda i: (i, 0)),
        ],
        out_shape=[
            jax.ShapeDtypeStruct((m, 2 * MEM_WIDTH), F32),
            jax.ShapeDtypeStruct((m, 2 * MEM_WIDTH), BF16),
        ],
        compiler_params=_cparams(("parallel",)),
        name="mem_kv",
    )(mem2d, nw, w_bf)


def _rope_tables(pos):
    half = ROPE_DIM // 2
    inv = jnp.power(ROPE_THETA, -jnp.arange(half, dtype=F32) * 2.0 / ROPE_DIM)
    ang = pos.astype(F32)[:, None] * inv[None, :]
    cos, sin = jnp.cos(ang), jnp.sin(ang)
    m = jnp.arange(LANES) % ATT_HEAD_DIM
    idx = m % half
    c = jnp.where(m[None, :] < ROPE_DIM, cos[:, idx], 1.0)
    s1 = jnp.where(m[None, :] < half, -sin[:, idx], 0.0)
    s2 = jnp.where((m[None, :] >= half) & (m[None, :] < ROPE_DIM), sin[:, idx], 0.0)
    return c.astype(F32), s1.astype(F32), s2.astype(F32)


def _in_proj_kernel(x_ref, nw_ref, wt_ref, c_ref, s1_ref, s2_ref,
                    zx_ref, q0_ref, q1_ref, q2_ref, rest_ref, dt_ref, *more_refs, tm, dils, win_rows):
    pw_refs = more_refs[:3] if win_rows else ()
    h_ref, acc_ref, tmp_ref = more_refs[-3:]
    nslab = COL_TILE // LANES
    h_ref[...] = _rms(x_ref[...], nw_ref[...]).astype(BF16)
    dt_ref[...] = _dot_nt(h_ref[...], wt_ref[DT_COL0:DT_COL0 + LANES, :])
    q_refs = (q0_ref, q1_ref, q2_ref)
    strided = 0
    for j in range(N_COL_TILES):
        col0 = j * COL_TILE + (SSD_N_HEADS if j >= QKV_TILE0 else 0)
        acc = _dot_nt(h_ref[...], wt_ref[col0:col0 + COL_TILE, :])
        if j < QKV_TILE0:
            zx_ref[:, j * COL_TILE:(j + 1) * COL_TILE] = acc.astype(zx_ref.dtype)
            continue
        if j >= REST_TILE0:
            rest_ref[:, (j - REST_TILE0) * COL_TILE:(j - REST_TILE0 + 1) * COL_TILE] = acc.astype(rest_ref.dtype)
            continue
        g, kind = divmod(j - QKV_TILE0, 3)
        out_ref, d = q_refs[g], dils[g]
        for c in range(nslab):
            a = acc[:, c * LANES:(c + 1) * LANES]
            if kind < 2:
                a = (a * c_ref[...] + pltpu.roll(a, LANES - ROPE_DIM // 2, 1) * s1_ref[...]
                     + pltpu.roll(a, ROPE_DIM // 2, 1) * s2_ref[...])
            if kind == 0:
                a = a * (ATT_HEAD_DIM ** -0.5 * LOG2E)
            lanes = slice(kind * COL_TILE + c * LANES, kind * COL_TILE + (c + 1) * LANES)
            if win_rows and kind > 0:
                ch0 = (kind - 1) * COL_TILE + c * LANES
                pw_refs[g][ch0:ch0 + LANES, :] = a[tm - win_rows[g]:, :].T
            if d == 1:
                out_ref[0, :, lanes] = a.astype(out_ref.dtype)
            elif d == 4:
                acc_ref[strided, c] = a
                for r in range(4):
                    out_ref[r, :, lanes] = acc_ref[strided, c, pl.ds(r, tm // 4, stride=4), :].astype(out_ref.dtype)
            else:
                acc_ref[strided, c] = a
                q4 = tm // 4
                for r1 in range(4):
                    tmp_ref[kind, c, r1 * q4:(r1 + 1) * q4, :] = acc_ref[strided, c, pl.ds(r1, q4, stride=4), :]
                for r1 in range(4):
                    for r2 in range(4):
                        out_ref[r1 + 4 * r2, :, lanes] = tmp_ref[
                            kind, c, pl.ds(r1 * q4 + r2, tm // 16, stride=4), :].astype(out_ref.dtype)
        if d > 1:
            strided += 1


def _in_proj(x2d, nw, wt, tabs, *, seq, tm, dils, out_dtype, windows=None):
    t = x2d.shape[0]
    nb = t // seq
    tiles_per_seq = seq // tm
    tab_blocks = tabs[0].shape[0] // tm
    n_strided = 3 * sum(1 for d in dils if d > 1)
    nslab = COL_TILE // LANES

    def qkv_spec(d):
        return pl.BlockSpec((None, d, tm // d, 3 * COL_TILE),
                            lambda i: (i // tiles_per_seq, 0, i % tiles_per_seq, 0))

    win_rows, win_specs, win_shapes = None, [], []
    if windows:
        win_rows = tuple(min(tm, w) for w in windows)
        for w, wr in zip(windows, win_rows):
            first = tiles_per_seq - w // wr
            win_specs.append(pl.BlockSpec(
                (None, 2 * COL_TILE, wr),
                lambda i, first=first: (i // tiles_per_seq, 0, jnp.maximum(i % tiles_per_seq - first, 0))))
            win_shapes.append(jax.ShapeDtypeStruct((nb, 2 * COL_TILE, w), F32))

    tab_spec = pl.BlockSpec((tm, LANES), lambda i: (i % tab_blocks, 0))
    resident = lambda shape: pl.BlockSpec(shape, lambda i: (0, 0), pipeline_mode=pl.Buffered(1))
    kern = functools.partial(_in_proj_kernel, tm=tm, dils=dils, win_rows=win_rows)
    n_zx, n_rest = QKV_TILE0 * COL_TILE, (N_COL_TILES - REST_TILE0) * COL_TILE
    return pl.pallas_call(
        kern,
        grid=(t // tm,),
        in_specs=[
            pl.BlockSpec((tm, D_MODEL), lambda i: (i, 0)),
            resident((1, D_MODEL)),
            resident(wt.shape),
            tab_spec, tab_spec, tab_spec,
        ],
        out_specs=[
            pl.BlockSpec((tm, n_zx), lambda i: (i, 0)),
            qkv_spec(dils[0]), qkv_spec(dils[1]), qkv_spec(dils[2]),
            pl.BlockSpec((tm, n_rest), lambda i: (i, 0)),
            pl.BlockSpec((tm, LANES), lambda i: (i, 0)),
        ] + win_specs,
        out_shape=[
            jax.ShapeDtypeStruct((t, n_zx), out_dtype),
            jax.ShapeDtypeStruct((nb, dils[0], seq // dils[0], 3 * COL_TILE), out_dtype),
            jax.ShapeDtypeStruct((nb, dils[1], seq // dils[1], 3 * COL_TILE), out_dtype),
            jax.ShapeDtypeStruct((nb, dils[2], seq // dils[2], 3 * COL_TILE), out_dtype),
            jax.ShapeDtypeStruct((t, n_rest), out_dtype),
            jax.ShapeDtypeStruct((t, LANES), F32),
        ] + win_shapes,
        scratch_shapes=[
            pltpu.VMEM((tm, D_MODEL), BF16),
            pltpu.VMEM((max(n_strided, 1), nslab, tm, LANES), F32),
            pltpu.VMEM((3, nslab, tm, LANES), F32),
        ],
        compiler_params=_cparams(("arbitrary",)),
        name="in_proj",
    )(x2d, nw, wt, *tabs)


def _ssd_small(dt_raw, dtb_row, a_row, dtb_col, a_col, same01, same01_t, tot01, tot01_t):
    dt = _softplus(dt_raw + dtb_row)
    dta = dt * a_row
    acs = _dot_exact_lhs01(same01, dta)
    tot = _dot_exact_lhs01(tot01, dta)
    dtd = dt * jnp.exp2(tot - acs)
    dt_raw_t = dt_raw.T
    dt_t = _softplus(dt_raw_t + dtb_col)
    dta_t = dt_t * a_col
    acs_t = _dot_exact_rhs01(dta_t, same01_t)
    tot_t = _dot_exact_rhs01(dta_t, tot01_t)
    dtd_t = dt_t * jnp.exp2(tot_t - acs_t)
    return dt, acs, tot, dtd, dt_t, acs_t, dtd_t


def _ssd_pairs(x, bm, cm, acs, acs_t, dt_t, dtd_t, mask, st_ref, y_ref_write, st_scale_row, after_pair=None):
    lane = lax.broadcasted_iota(jnp.int32, (1, LANES), 1)
    half = [lane < ATT_HEAD_DIM, lane >= ATT_HEAD_DIM]
    bm_t = [bm[:, g * SSD_D_STATE:(g + 1) * SSD_D_STATE].T for g in range(SSD_N_GROUPS)]
    cms = [cm[:, g * SSD_D_STATE:(g + 1) * SSD_D_STATE] for g in range(SSD_N_GROUPS)]
    cb = [_dot(cms[g].astype(BF16), bm_t[g].astype(BF16)) for g in range(SSD_N_GROUPS)]
    for p in range(SSD_N_HEADS // 2):
        g = p // 2
        xp = x[:, p * LANES:(p + 1) * LANES]
        if st_ref is not None:
            stp = st_ref[:, p * LANES:(p + 1) * LANES]
        y_pair = jnp.zeros((SSD_CHUNK, LANES), F32)
        st_add = jnp.zeros((SSD_D_STATE, LANES), F32)
        for hh in range(2):
            h = 2 * p + hh
            col = acs[:, h:h + 1]
            row = acs_t[h:h + 1, :]
            lmat = jnp.exp2(jnp.where(mask, col - row, NEG_INF))
            m = cb[g] * lmat * dt_t[h:h + 1, :]
            xm = jnp.where(half[hh], xp, 0.0).astype(BF16)
            if st_ref is not None:
                cx = cms[g] * jnp.exp2(col)
                lhs = jnp.concatenate([m, cx], axis=1).astype(BF16)
                rhs = jnp.concatenate([xm, jnp.where(half[hh], stp, 0.0).astype(BF16)], axis=0)
                y_pair = y_pair + _dot(lhs, rhs)
                bt_h = (bm_t[g] * dtd_t[h:h + 1, :]).astype(BF16)
                st_add = st_add + _dot(bt_h, xm)
            else:
                y_pair = y_pair + _dot(m.astype(BF16), xm)
        y_ref_write(p, y_pair)
        if st_ref is not None:
            scale = jnp.where(half[0], st_scale_row[:, 2 * p:2 * p + 1], st_scale_row[:, 2 * p + 1:2 * p + 2])
            st_ref[:, p * LANES:(p + 1) * LANES] = stp * scale + st_add
        if after_pair is not None:
            after_pair(p)


def _ssd_prompt_kernel(z_ref, xs_ref, bc_ref, xsn_ref, bcn_ref, dt_ref, cw_ref, cb_ref, dtb_ref, a_ref, dtbc_ref,
                       ac_ref, dsk_ref, nw_ref, sh_ref, y_ref, ssm_ref, xc_ref, st_ref, ybuf, *, ncs):
    c = pl.program_id(1)
    q = SSD_CHUNK

    def conv_act(prev, cur, lanes):
        shifted = _dot(sh_ref[...], jnp.concatenate([prev, cur], axis=0))
        acc = cb_ref[:, lanes] + cw_ref[SSD_CONV - 1:SSD_CONV, lanes] * cur.astype(F32)
        for s in range(1, SSD_CONV):
            acc = acc + cw_ref[SSD_CONV - 1 - s:SSD_CONV - s, lanes] * shifted[(s - 1) * q:s * q]
        return _silu(acc)

    @pl.when(c == 0)
    def _():
        st_ref[...] = jnp.zeros(st_ref.shape, F32)
        no_history = jnp.zeros((q, D_MODEL), xs_ref.dtype)
        xc_ref[0, :, 0:D_MODEL] = conv_act(no_history, xs_ref[0:q, :], slice(0, D_MODEL))
        xc_ref[0, :, D_MODEL:SSD_CONV_DIM] = conv_act(no_history, bc_ref[0:q, :], slice(D_MODEL, SSD_CONV_DIM))

    slab = SSD_CONV_DIM // (SSD_N_HEADS // 2)
    ri = lax.broadcasted_iota(jnp.int32, (q, q), 0)
    ci = lax.broadcasted_iota(jnp.int32, (q, q), 1)
    tril = ri >= ci
    tril01 = jnp.where(tril, 1.0, 0.0).astype(BF16)
    triu01 = jnp.where(ri <= ci, 1.0, 0.0).astype(BF16)
    ones01 = jnp.ones((q, q), BF16)

    def write_y(p, y_pair):
        ybuf[:, p * LANES:(p + 1) * LANES] = y_pair

    for i in range(ncs):
        rows = slice(i * q, (i + 1) * q)

        def conv_next(p, i=i, rows=rows):
            lanes = slice(p * slab, (p + 1) * slab)
            src_ref, nxt_ref, off = (xs_ref, xsn_ref, 0) if lanes.start < D_MODEL else (bc_ref, bcn_ref, D_MODEL)
            src = slice(lanes.start - off, lanes.stop - off)
            cur = src_ref[(i + 1) * q:(i + 2) * q, src] if i + 1 < ncs else nxt_ref[:, src]
            xc_ref[(i + 1) % ncs, :, lanes] = conv_act(src_ref[rows, src], cur, lanes)

        x = xc_ref[i, :, 0:D_MODEL]
        bm = xc_ref[i, :, D_MODEL:D_MODEL + 512]
        cm = xc_ref[i, :, D_MODEL + 512:SSD_CONV_DIM]
        dt, acs, tot, dtd, dt_t, acs_t, dtd_t = _ssd_small(
            dt_ref[rows, :], dtb_ref[...], a_ref[...], dtbc_ref[...], ac_ref[...], tril01, triu01, ones01, ones01)
        decay_row = jnp.exp2(tot[0:1, :])
        _ssd_pairs(x, bm, cm, acs, acs_t, dt_t, dtd_t, tril, st_ref, write_y, decay_row, after_pair=conv_next)
        y = (ybuf[...] + dsk_ref[...] * x) * _silu(z_ref[rows, :].astype(F32))
        y_ref[rows, :] = _rms(y, nw_ref[...]).astype(y_ref.dtype)

    @pl.when(c == pl.num_programs(1) - 1)
    def _():
        for p in range(D_MODEL // LANES):
            ssm_ref[p * LANES:(p + 1) * LANES, :] = st_ref[:, p * LANES:(p + 1) * LANES].T


def _ssd_prompt(zx, dt, cw, cb, dtb, a_row, dtb_col, a_col, dsk, nw, *, nbatch, seq):
    q = SSD_CHUNK
    ncs = next(n for n in (4, 2, 1) if (seq // q) % n == 0)
    rows = ncs * q
    nc = seq // q
    ns = nc // ncs
    row = lambda b, c: b * ns + c
    nxt = lambda b, c: b * nc + jnp.minimum((c + 1) * ncs, nc - 1)
    vec = lambda n: pl.BlockSpec((1, n), lambda b, c: (0, 0))
    col = pl.BlockSpec((LANES, 1), lambda b, c: (0, 0))
    nsh = SSD_CONV - 1
    out_t = jnp.arange(nsh * q) % q
    shift = 1 + jnp.arange(nsh * q) // q
    shift01 = (jnp.arange(2 * q)[None, :] == (q + out_t - shift)[:, None]).astype(BF16)
    return pl.pallas_call(
        functools.partial(_ssd_prompt_kernel, ncs=ncs),
        grid=(nbatch, ns),
        in_specs=[
            pl.BlockSpec((rows, D_MODEL), lambda b, c: (row(b, c), 0)),
            pl.BlockSpec((rows, D_MODEL), lambda b, c: (row(b, c), 1)),
            pl.BlockSpec((rows, D_MODEL), lambda b, c: (row(b, c), 2)),
            pl.BlockSpec((q, D_MODEL), lambda b, c: (nxt(b, c), 1)),
            pl.BlockSpec((q, D_MODEL), lambda b, c: (nxt(b, c), 2)),
            pl.BlockSpec((rows, LANES), lambda b, c: (row(b, c), 0)),
            pl.BlockSpec((SSD_CONV, SSD_CONV_DIM), lambda b, c: (0, 0)),
            vec(SSD_CONV_DIM), vec(LANES), vec(LANES), col, col, vec(D_MODEL), vec(D_MODEL),
            pl.BlockSpec(((SSD_CONV - 1) * q, 2 * q), lambda b, c: (0, 0)),
        ],
        out_specs=[
            pl.BlockSpec((rows, D_MODEL), lambda b, c: (row(b, c), 0)),
            pl.BlockSpec((None, D_MODEL, SSD_D_STATE), lambda b, c: (b, 0, 0)),
        ],
        out_shape=[
            jax.ShapeDtypeStruct((nbatch * seq, D_MODEL), BF16),
            jax.ShapeDtypeStruct((nbatch, D_MODEL, SSD_D_STATE), F32),
        ],
        scratch_shapes=[
            pltpu.VMEM((max(ncs, 2), q, SSD_CONV_DIM), F32),
            pltpu.VMEM((SSD_D_STATE, D_MODEL), F32),
            pltpu.VMEM((q, D_MODEL), F32),
        ],
        compiler_params=_cparams(("parallel", "arbitrary")),
        name="ssd_prompt",
    )(zx, zx, zx, zx, zx, dt, cw, cb, dtb, a_row, dtb_col, a_col, dsk, nw, shift01)


def _expand_heads(v, e01):
    return _dot_exact_rhs01(v, e01)


def _ssd_sample_rows_kernel(xs_ref, bc_ref, dt_ref, hist_ref, cw_ref, cb_ref, dtb_ref, a_ref, dtbc_ref, ac_ref,
                            dsk_ref, e01_ref,
                            ypart_ref, ea_ref, xdt_ref, bm_ref, cm_ref, cd_ref, ybuf, *, tdec):
    q = SSD_CHUNK
    x_raw = jnp.concatenate([xs_ref[...], bc_ref[...]], axis=1)
    hist = hist_ref[...]
    rowi = lax.broadcasted_iota(jnp.int32, (q, 1), 0)
    tpos = rowi % tdec
    acc = cb_ref[...] + cw_ref[SSD_CONV - 1:SSD_CONV, :] * x_raw
    for s in range(1, SSD_CONV):
        prev = jnp.where(tpos >= s, pltpu.roll(x_raw, s, 0), pltpu.roll(hist, q - tdec + s, 0))
        acc = acc + cw_ref[SSD_CONV - 1 - s:SSD_CONV - s, :] * prev
    xc = _silu(acc)
    x = xc[:, :D_MODEL]
    bm = xc[:, D_MODEL:D_MODEL + 512]
    cm = xc[:, D_MODEL + 512:]

    ri = lax.broadcasted_iota(jnp.int32, (q, q), 0)
    ci = lax.broadcasted_iota(jnp.int32, (q, q), 1)
    same = (ri // tdec) == (ci // tdec)
    mask = same & (ri >= ci)
    low01 = jnp.where(mask, 1.0, 0.0).astype(BF16)
    up01 = jnp.where(same & (ri <= ci), 1.0, 0.0).astype(BF16)
    same01 = jnp.where(same, 1.0, 0.0).astype(BF16)
    dt, acs, tot, dtd, dt_t, acs_t, dtd_t = _ssd_small(
        dt_ref[...], dtb_ref[...], a_ref[...], dtbc_ref[...], ac_ref[...], low01, up01, same01, same01)

    def write_y(p, y_pair):
        ybuf[:, p * LANES:(p + 1) * LANES] = y_pair

    _ssd_pairs(x, bm, cm, acs, acs_t, dt_t, dtd_t, mask, None, write_y, None)

    e01 = e01_ref[...]
    ypart_ref[...] = ybuf[...] + dsk_ref[...] * x
    ea_ref[...] = _expand_heads(jnp.exp2(acs), e01)
    xd = x * _expand_heads(dtd, e01)
    for p in range(D_MODEL // LANES):
        xdt_ref[p * LANES:(p + 1) * LANES, :] = xd[:, p * LANES:(p + 1) * LANES].T.astype(BF16)
    bm_ref[...] = bm.astype(BF16)
    cm_ref[...] = cm.astype(BF16)
    cd_ref[...] = jnp.exp2(tot)


def _ssd_sample_rows(zx, dt, hist, cw, cb, dtb, a_row, dtb_col, a_col, dsk, e01, *, tdec):
    t = zx.shape[0]
    q = SSD_CHUNK
    vec = lambda n: pl.BlockSpec((1, n), lambda i: (0, 0))
    col = pl.BlockSpec((LANES, 1), lambda i: (0, 0))
    kern = functools.partial(_ssd_sample_rows_kernel, tdec=tdec)
    return pl.pallas_call(
        kern,
        grid=(t // q,),
        in_specs=[
            pl.BlockSpec((q, D_MODEL), lambda i: (i, 1)),
            pl.BlockSpec((q, D_MODEL), lambda i: (i, 2)),
            pl.BlockSpec((q, LANES), lambda i: (i, 0)),
            pl.BlockSpec((q, SSD_CONV_DIM), lambda i: (i, 0)),
            pl.BlockSpec((SSD_CONV, SSD_CONV_DIM), lambda i: (0, 0)),
            vec(SSD_CONV_DIM), vec(LANES), vec(LANES), col, col, vec(D_MODEL),
            pl.BlockSpec((LANES, D_MODEL), lambda i: (0, 0)),
        ],
        out_specs=[
            pl.BlockSpec((q, D_MODEL), lambda i: (i, 0)),
            pl.BlockSpec((q, D_MODEL), lambda i: (i, 0)),
            pl.BlockSpec((D_MODEL, q), lambda i: (0, i)),
            pl.BlockSpec((q, 512), lambda i: (i, 0)),
            pl.BlockSpec((q, 512), lambda i: (i, 0)),
            pl.BlockSpec((q, LANES), lambda i: (i, 0)),
        ],
        out_shape=[
            jax.ShapeDtypeStruct((t, D_MODEL), F32),
            jax.ShapeDtypeStruct((t, D_MODEL), F32),
            jax.ShapeDtypeStruct((D_MODEL, t), BF16),
            jax.ShapeDtypeStruct((t, 512), BF16),
            jax.ShapeDtypeStruct((t, 512), BF16),
            jax.ShapeDtypeStruct((t, LANES), F32),
        ],
        scratch_shapes=[pltpu.VMEM((q, D_MODEL), F32)],
        compiler_params=_cparams(("parallel",)),
        name="ssd_sample_rows",
    )(zx, zx, dt, hist, cw, cb, dtb, a_row, dtb_col, a_col, dsk, e01)


def _ssd_sample_state_kernel(cd_ref, z_ref, ypart_ref, ea_ref, xdt_ref, bm_ref, cm_ref, st_ref, nw_ref,
                             y_ref, sto_ref, *, tdec, gseq):
    i = pl.program_id(0)
    q = SSD_CHUNK
    rows = gseq * tdec
    steps_per_blk = q // rows
    lane = lax.broadcasted_iota(jnp.int32, (1, q), 1)
    rg = 16
    rowi = lax.broadcasted_iota(jnp.int32, (rg, 1), 0)
    lane_base = (i % steps_per_blk) * rows
    seq_per_rg = rg // tdec
    yoff_groups = []
    for q8 in range(rows // rg):
        yoff8 = jnp.zeros((rg, D_MODEL), F32)
        for sj in range(seq_per_rg):
            jj = q8 * seq_per_rg + sj
            b = i * gseq + jj
            lmask = (lane >= lane_base + jj * tdec) & (lane < lane_base + (jj + 1) * tdec)
            parts = []
            for g in range(SSD_N_GROUPS):
                s_g = st_ref[jj, g * 256:(g + 1) * 256, :]
                c8 = cm_ref[q8 * rg:(q8 + 1) * rg, g * SSD_D_STATE:(g + 1) * SSD_D_STATE]
                parts.append(_dot_nt(c8, s_g.astype(BF16)))
                lhs = jnp.where(lmask, xdt_ref[g * 256:(g + 1) * 256, :], jnp.zeros((), BF16))
                add = _dot(lhs, bm_ref[:, g * SSD_D_STATE:(g + 1) * SSD_D_STATE])
                for hh in range(4):
                    h = 4 * g + hh
                    sl = slice(hh * SSD_HEAD_DIM, (hh + 1) * SSD_HEAD_DIM)
                    sto_ref[jj, g * 256 + hh * 64:g * 256 + (hh + 1) * 64, :] = (
                        s_g[sl, :] * cd_ref[b, h] + add[sl, :])
            yo = jnp.concatenate(parts, axis=1)
            in_seq = (rowi >= sj * tdec) & (rowi < (sj + 1) * tdec)
            yoff8 = jnp.where(in_seq, yo, yoff8)
        yoff_groups.append(yoff8)
    yoff = jnp.concatenate(yoff_groups, axis=0) if len(yoff_groups) > 1 else yoff_groups[0]
    y = (ypart_ref[...] + yoff * ea_ref[...]) * _silu(z_ref[...])
    y_ref[...] = _rms(y, nw_ref[...])


def _ssd_sample_state(cd, zx, ypart, ea, xdt, bmb, cmb, state, nw, *, tdec, gseq):
    t = zx.shape[0]
    nseq = t // tdec
    q = SSD_CHUNK
    rows = gseq * tdec
    spb = q // rows
    kern = functools.partial(_ssd_sample_state_kernel, tdec=tdec, gseq=gseq)
    return pl.pallas_call(
        kern,
        grid=(nseq // gseq,),
        in_specs=[
            pl.BlockSpec(memory_space=pltpu.SMEM),
            pl.BlockSpec((rows, D_MODEL), lambda i: (i, 0)),
            pl.BlockSpec((rows, D_MODEL), lambda i: (i, 0)),
            pl.BlockSpec((rows, D_MODEL), lambda i: (i, 0)),
            pl.BlockSpec((D_MODEL, q), lambda i: (0, i // spb)),
            pl.BlockSpec((q, 512), lambda i: (i // spb, 0)),
            pl.BlockSpec((rows, 512), lambda i: (i, 0)),
            pl.BlockSpec((gseq, D_MODEL, SSD_D_STATE), lambda i: (i, 0, 0)),
            pl.BlockSpec((1, D_MODEL), lambda i: (0, 0)),
        ],
        out_specs=[
            pl.BlockSpec((rows, D_MODEL), lambda i: (i, 0)),
            pl.BlockSpec((gseq, D_MODEL, SSD_D_STATE), lambda i: (i, 0, 0)),
        ],
        out_shape=[
            jax.ShapeDtypeStruct((t, D_MODEL), F32),
            jax.ShapeDtypeStruct((nseq, D_MODEL, SSD_D_STATE), F32),
        ],
        compiler_params=_cparams(("parallel",)),
        name="ssd_sample_state",
    )(cd, zx, ypart, ea, xdt, bmb, cmb, state, nw)


def _lse_lane(h):
    return 16 * h


def _attn_prompt_kernel(q_ref, kc_ref, kp_ref, vc_ref, vp_ref, o_ref, lse_ref, *, nr, nq):
    n = pl.program_id(2)
    nk = 128
    ks = [jnp.concatenate([kp_ref[rr], kc_ref[rr]], axis=0) for rr in range(nr)]
    vs = [jnp.concatenate([vp_ref[rr], vc_ref[rr]], axis=0) for rr in range(nr)]
    lane = lax.broadcasted_iota(jnp.int32, (1, LANES), 1)
    lo = lane < ATT_HEAD_DIM
    qi = lax.broadcasted_iota(jnp.int32, (2 * nk, 2 * nk), 0) % nk
    kj = lax.broadcasted_iota(jnp.int32, (2 * nk, 2 * nk), 1)
    in_band = (kj >= qi) & (kj <= qi + nk)
    band = jnp.where(in_band, 0.0, NEG_INF)
    band_first = jnp.where(in_band & ((kj >= nk) | (n > 0)), 0.0, NEG_INF)
    zero = jnp.zeros((), q_ref.dtype)
    npairs = ATT_HEADS // 2
    pair_lanes = [slice(p * LANES, (p + 1) * LANES) for p in range(npairs)]
    units = [(rr, j, p) for rr in range(nr) for j in range(nq) for p in range(npairs)]

    def scores(u):
        rr, j, p = u
        qp = q_ref[rr, j * nk:(j + 1) * nk, pair_lanes[p]]
        q2 = jnp.concatenate([jnp.where(lo, qp, zero), jnp.where(lo, zero, qp)], axis=0)
        kk = ks[rr][j * nk:(j + 2) * nk, pair_lanes[p]]
        return _dot_nt(q2, kk) + (band_first if j == 0 else band)

    def softmax(s):
        m = jnp.max(s, axis=-1, keepdims=True)
        e = jnp.exp2(s - m)
        den = jnp.sum(e, axis=-1, keepdims=True)
        return e.astype(BF16), den, (m + jnp.log2(den)) * LN2

    ahead = 3
    s_next = [scores(u) for u in units[:ahead]]
    lse_full, o_parts = None, []
    for i, (rr, j, p) in enumerate(units):
        e, den, lse = softmax(s_next.pop(0))
        if i + ahead < len(units):
            s_next.append(scores(units[i + ahead]))
        r = _dot(e, vs[rr][j * nk:(j + 2) * nk, pair_lanes[p]]) / den
        o_parts.append(jnp.where(lo, r[0:nk], r[nk:2 * nk]).astype(o_ref.dtype))
        if p == 0:
            lse_full = jnp.zeros((nk, LANES), F32)
        for hh in range(2):
            base = _lse_lane(2 * p + hh)
            lse_full = jnp.where((lane >= base) & (lane < base + 16), lse[hh * nk:(hh + 1) * nk], lse_full)
        if p == npairs - 1:
            o_ref[rr, j * nk:(j + 1) * nk, :] = jnp.concatenate(o_parts, axis=1)
            lse_ref[rr, j * nk:(j + 1) * nk, :] = lse_full
            o_parts = []


ATT_BLOCKS_PER_STEP = 8


def _attn_prompt(qkv, *, d):
    nb, _, length, _ = qkv.shape
    nk = 128
    nq = next(c for c in (8, 4, 2, 1) if c <= ATT_BLOCKS_PER_STEP and length % (c * nk) == 0)
    nr = next(c for c in (8, 4, 2, 1) if c * nq <= ATT_BLOCKS_PER_STEP and d % c == 0)
    nblk = length // (nq * nk)
    cur = lambda which: pl.BlockSpec((None, nr, nq * nk, COL_TILE), lambda b, r, n: (b, r, n, which))
    prev = lambda which: pl.BlockSpec((None, nr, nk, COL_TILE),
                                      lambda b, r, n: (b, r, jnp.maximum(nq * n - 1, 0), which))
    return pl.pallas_call(
        functools.partial(_attn_prompt_kernel, nr=nr, nq=nq),
        grid=(nb, d // nr, nblk),
        in_specs=[cur(0), cur(1), prev(1), cur(2), prev(2)],
        out_specs=[
            pl.BlockSpec((None, nr, nq * nk, ATT_WIDTH), lambda b, r, n: (b, r, n, 0)),
            pl.BlockSpec((None, nr, nq * nk, LANES), lambda b, r, n: (b, r, n, 0)),
        ],
        out_shape=[
            jax.ShapeDtypeStruct((nb, d, length, ATT_WIDTH), BF16),
            jax.ShapeDtypeStruct((nb, d, length, LANES), F32),
        ],
        compiler_params=_cparams(("parallel", "parallel", "arbitrary")),
        name=f"attn_prompt_d{d}",
    )(qkv, qkv, qkv, qkv, qkv)


def _attn_sample_kernel(q0_ref, q1_ref, q2_ref, qm_ref, c0_ref, c1_ref, c2_ref, cm_ref, oa_ref, om_ref, *, tdec):
    nh = ATT_HEADS
    rows8 = 8
    j = pl.program_id(1)
    r64 = nh * rows8
    rowi = lax.broadcasted_iota(jnp.int32, (r64, 1), 0)
    head_of_row = rowi // rows8
    r8 = rowi % rows8
    seq_q = r8 // tdec
    t_q = r8 % tdec
    lane512 = lax.broadcasted_iota(jnp.int32, (1, ATT_WIDTH), 1)
    headmask = (lane512 // ATT_HEAD_DIM) == head_of_row
    col8 = lax.broadcasted_iota(jnp.int32, (1, rows8), 1)
    seq_k = col8 // tdec
    t_k = col8 % tdec

    outs, lses = [], []
    for g, (q_ref, c_ref, d) in enumerate(zip((q0_ref, q1_ref, q2_ref), (c0_ref, c1_ref, c2_ref), ATT_DILATIONS)):
        w = c_ref.shape[-1]
        qkv = q_ref[...]
        qv = qkv[:, :ATT_WIDTH]
        kn = qkv[:, ATT_WIDTH:2 * ATT_WIDTH]
        vn = qkv[:, 2 * ATT_WIDTH:]
        q64 = jnp.where(headmask, jnp.concatenate([qv] * nh, axis=0), 0.0)
        s_new = _dot_nt(q64, kn)
        if d == 1:
            valid_new = (seq_k == seq_q) & (t_k <= t_q)
        else:
            valid_new = (seq_k == seq_q) & (t_k == t_q)
        s_new = jnp.where(valid_new, s_new, NEG_INF)
        wi = lax.broadcasted_iota(jnp.int32, (1, w), 1)
        valid_c = (wi >= t_q) if d == 1 else ((wi % d) == t_q)
        s_c = jnp.where(valid_c, _dot(q64.astype(BF16), c_ref[0].astype(BF16)), NEG_INF)
        m = jnp.maximum(jnp.max(s_c, axis=-1, keepdims=True), jnp.max(s_new, axis=-1, keepdims=True))
        e_c = jnp.exp2(s_c - m)
        e_n = jnp.exp2(s_new - m)
        den = jnp.sum(e_c, axis=-1, keepdims=True) + jnp.sum(e_n, axis=-1, keepdims=True)
        o = _dot(e_n, vn) + _dot_nt(e_c.astype(BF16), c_ref[1].astype(BF16))
        outs.append(o / den)
        lses.append((m + jnp.log2(den)) * LN2)

    mx = jnp.maximum(jnp.maximum(lses[0], lses[1]), lses[2])
    ws = [jnp.exp(l - mx) for l in lses]
    om = (ws[0] * outs[0] + ws[1] * outs[1] + ws[2] * outs[2]) / (ws[0] + ws[1] + ws[2])
    om = jnp.where(headmask, om, 0.0)
    o8 = om[0:rows8]
    for h in range(1, nh):
        o8 = o8 + om[h * rows8:(h + 1) * rows8]

    qm = jnp.concatenate([qm_ref[...] * (MEM_HEAD_DIM ** -0.5), jnp.zeros((rows8, MEM_WIDTH), F32)], axis=0)
    mparts = []
    for h in range(MEM_HEADS):
        kh = cm_ref[pl.ds(h, MEM_LEN, stride=2 * MEM_HEADS), :].astype(BF16)
        vh = cm_ref[pl.ds(MEM_HEADS + h, MEM_LEN, stride=2 * MEM_HEADS), :].astype(BF16)
        s = _dot_nt(qm[:, h * MEM_HEAD_DIM:(h + 1) * MEM_HEAD_DIM].astype(BF16), kh)
        mm = jnp.max(s, axis=-1, keepdims=True)
        e = jnp.exp(s - mm)
        den = jnp.sum(e, axis=-1, keepdims=True)
        mparts.append((_dot(e.astype(BF16), vh) / den)[0:rows8])
    m8 = jnp.concatenate(mparts, axis=1)

    row8 = lax.broadcasted_iota(jnp.int32, (rows8, 1), 0)
    mine = (row8 // tdec) == j

    @pl.when(j == 0)
    def _():
        oa_ref[...] = jnp.where(mine, o8, 0.0)
        om_ref[...] = jnp.where(mine, m8, 0.0)

    @pl.when(j > 0)
    def _():
        oa_ref[...] = jnp.where(mine, o8, oa_ref[...])
        om_ref[...] = jnp.where(mine, m8, om_ref[...])


def _attn_sample(q0, q1, q2, rest, c0, c1, c2, cmem, *, tdec):
    t = q0.shape[0]
    spg = 8 // tdec
    row_spec = pl.BlockSpec((8, 3 * COL_TILE), lambda i, j: (i, 0))
    cache_spec = lambda c: pl.BlockSpec((None,) + c.shape[1:], lambda i, j: (i * spg + j,) + (0,) * (c.ndim - 1))
    kern = functools.partial(_attn_sample_kernel, tdec=tdec)
    return pl.pallas_call(
        kern,
        grid=(t // 8, spg),
        in_specs=[
            row_spec, row_spec, row_spec,
            pl.BlockSpec((8, COL_TILE), lambda i, j: (i, 1)),
            cache_spec(c0), cache_spec(c1), cache_spec(c2), cache_spec(cmem),
        ],
        out_specs=[
            pl.BlockSpec((8, ATT_WIDTH), lambda i, j: (i, 0)),
            pl.BlockSpec((8, MEM_WIDTH), lambda i, j: (i, 0)),
        ],
        out_shape=[
            jax.ShapeDtypeStruct((t, ATT_WIDTH), F32),
            jax.ShapeDtypeStruct((t, MEM_WIDTH), F32),
        ],
        compiler_params=_cparams(("parallel", "arbitrary")),
        name="attn_sample",
    )(q0, q1, q2, rest, c0, c1, c2, cmem)


def _final_tail(x, p_ssd, y_att, y_mem, gate_raw, watt_ref, wmem_ref, wout_ref, nf_ref):
    gates = _sigmoid(gate_raw)
    merged = (gates[:, 0:D_MODEL] * p_ssd
              + gates[:, D_MODEL:2 * D_MODEL] * _dot(y_att.astype(BF16), watt_ref[...])
              + gates[:, 2 * D_MODEL:] * _dot(y_mem.astype(BF16), wmem_ref[...]))
    return _rms(x + _dot(merged.astype(BF16), wout_ref[...]), nf_ref[...])


def _final_prompt_kernel(x_ref, ys_ref, o0_ref, o1_ref, o2_ref, l0_ref, l1_ref, l2_ref, rest_ref, mkv_ref,
                         wssd_ref, watt_ref, wmem_ref, wout_ref, nf_ref, e16_ref,
                         y_ref, obuf, lbuf, *, tm, dils, nsub):
    nslab = ATT_WIDTH // LANES
    o_refs, l_refs = (o0_ref, o1_ref, o2_ref), (l0_ref, l1_ref, l2_ref)
    for g, (o_ref, l_ref, d) in enumerate(zip(o_refs, l_refs, dils)):
        if d == 1:
            continue
        for r in range(d):
            blk = o_ref[r].astype(F32)
            for c in range(nslab):
                obuf[g, c, pl.ds(r, tm // d, stride=d), :] = blk[:, c * LANES:(c + 1) * LANES]
            lbuf[g, pl.ds(r, tm // d, stride=d), :] = l_ref[r]
    e16 = e16_ref[...]

    for rs in range(nsub):
        rows = slice(rs * (tm // nsub), (rs + 1) * (tm // nsub))
        on, ln = [], []
        for g in range(3):
            if dils[g] == 1:
                on.append(o_refs[g][0, rows, :].astype(F32))
                ln.append(l_refs[g][0, rows, :])
            else:
                on.append(jnp.concatenate([obuf[g, c, rows, :] for c in range(nslab)], axis=1))
                ln.append(lbuf[g, rows, :])
        mx = jnp.maximum(jnp.maximum(ln[0], ln[1]), ln[2])
        ws = [jnp.exp(l - mx) for l in ln]
        wsum = ws[0] + ws[1] + ws[2]
        o_att = jnp.zeros((tm // nsub, ATT_WIDTH), F32)
        for g in range(3):
            hi, mid, _ = _split3(ws[g] / wsum)
            o_att = o_att + _dot(jnp.concatenate([hi, mid], axis=1), e16) * on[g]

        rest = rest_ref[rows, :]
        z_att = rest[:, 0:512].astype(F32)
        q_mem = rest[:, 512:1024]
        z_mem = rest[:, 1024:1536].astype(F32)
        gate_raw = rest[:, 1536:].astype(F32)
        y_att = o_att * _silu(z_att)

        outs = []
        for h in range(MEM_HEADS):
            lanes = slice(h * MEM_HEAD_DIM, (h + 1) * MEM_HEAD_DIM)
            vh = mkv_ref[:, MEM_WIDTH + h * MEM_HEAD_DIM:MEM_WIDTH + (h + 1) * MEM_HEAD_DIM]
            s = _dot_nt(q_mem[:, lanes], mkv_ref[:, lanes]) * (MEM_HEAD_DIM ** -0.5)
            m = jnp.max(s, axis=-1, keepdims=True)
            e = jnp.exp(s - m)
            den = jnp.sum(e, axis=-1, keepdims=True)
            outs.append(_dot(e.astype(BF16), vh) / den)
        y_mem = jnp.concatenate(outs, axis=1) * _silu(z_mem)

        p_ssd = _dot(ys_ref[rows, :], wssd_ref[...])
        y_ref[rows, :] = _final_tail(x_ref[rows, :], p_ssd, y_att, y_mem, gate_raw,
                                     watt_ref, wmem_ref, wout_ref, nf_ref)


def _final_prompt(x2d, y_ssd, o_g, l_g, rest, mkv_bf, wssd, watt, wmem, wout, nf, e16, *, seq, tm, dils, nsub):
    t = x2d.shape[0]
    tps = seq // tm

    def res_spec(d, width):
        return pl.BlockSpec((None, d, tm // d, width), lambda i: (i // tps, 0, i % tps, 0))

    full = lambda a: pl.BlockSpec(a.shape, lambda i: (0,) * a.ndim)
    kern = functools.partial(_final_prompt_kernel, tm=tm, dils=dils, nsub=nsub)
    return pl.pallas_call(
        kern,
        grid=(t // tm,),
        in_specs=[
            pl.BlockSpec((tm, D_MODEL), lambda i: (i, 0)),
            pl.BlockSpec((tm, D_MODEL), lambda i: (i, 0)),
            res_spec(dils[0], ATT_WIDTH), res_spec(dils[1], ATT_WIDTH), res_spec(dils[2], ATT_WIDTH),
            res_spec(dils[0], LANES), res_spec(dils[1], LANES), res_spec(dils[2], LANES),
            pl.BlockSpec((tm, rest.shape[1]), lambda i: (i, 0)),
            pl.BlockSpec((None, MEM_LEN, 2 * MEM_WIDTH), lambda i: (i // tps, 0, 0)),
            full(wssd), full(watt), full(wmem), full(wout), full(nf), full(e16),
        ],
        out_specs=pl.BlockSpec((tm, D_MODEL), lambda i: (i, 0)),
        out_shape=jax.ShapeDtypeStruct((t, D_MODEL), F32),
        scratch_shapes=[
            pltpu.VMEM((3, ATT_WIDTH // LANES, tm, LANES), F32),
            pltpu.VMEM((3, tm, LANES), F32),
        ],
        compiler_params=_cparams(("parallel",)),
        name="final_prompt",
    )(x2d, y_ssd, *o_g, *l_g, rest, mkv_bf, wssd, watt, wmem, wout, nf, e16)


def _final_sample_kernel(x_ref, ys_ref, oa_ref, om_ref, rest_ref,
                         wssd_ref, watt_ref, wmem_ref, wout_ref, nf_ref, y_ref):
    p_ssd = _dot(ys_ref[...].astype(BF16), wssd_ref[...])
    rest = rest_ref[...]
    y_att = oa_ref[...] * _silu(rest[:, 0:512])
    y_mem = om_ref[...] * _silu(rest[:, 1024:1536])
    y_ref[...] = _final_tail(x_ref[...], p_ssd, y_att, y_mem, rest[:, 1536:], watt_ref, wmem_ref, wout_ref, nf_ref)


def _final_sample(x2d, y_ssd, o_att, o_mem, rest, wssd, watt, wmem, wout, nf, *, tm):
    t = x2d.shape[0]
    full = lambda a: pl.BlockSpec(a.shape, lambda i: (0,) * a.ndim)
    rows = lambda w: pl.BlockSpec((tm, w), lambda i: (i, 0))
    return pl.pallas_call(
        _final_sample_kernel,
        grid=(t // tm,),
        in_specs=[rows(D_MODEL), rows(D_MODEL), rows(ATT_WIDTH), rows(MEM_WIDTH), rows(rest.shape[1]),
                  full(wssd), full(watt), full(wmem), full(wout), full(nf)],
        out_specs=rows(D_MODEL),
        out_shape=jax.ShapeDtypeStruct((t, D_MODEL), F32),
        compiler_params=_cparams(("parallel",)),
        name="final_sample",
    )(x2d, y_ssd, o_att, o_mem, rest, wssd, watt, wmem, wout, nf)


def kernel(x_prompt, x_sample, mem_prompt, cache_win128_kv, cache_win512_kv, cache_win2048_kv, cache_mem_kv,
           state_conv, state_ssm, norm_in_w, w_in, conv_w, conv_b, dt_bias, a_log, d_skip, ssd_norm_w,
           mem_norm_w, w_mem_kv, w_br_ssd, w_br_att, w_br_mem, w_out, norm_f_w):
    bp, seq, _ = x_prompt.shape
    bs, tdec, _ = x_sample.shape
    dils = ATT_DILATIONS

    row = lambda v: v.reshape(1, -1).astype(F32)
    wt = w_in.T.astype(BF16)
    pad_heads = lambda v: jnp.pad(v.astype(F32), (0, LANES - SSD_N_HEADS)).reshape(1, LANES)
    dtb = pad_heads(dt_bias)
    a_row = jnp.pad(-jnp.exp(a_log.astype(F32)) * LOG2E, (0, LANES - SSD_N_HEADS)).reshape(1, LANES)
    dtb_col, a_col = dtb.reshape(LANES, 1), a_row.reshape(LANES, 1)
    dsk = jnp.repeat(d_skip.astype(F32), SSD_HEAD_DIM).reshape(1, D_MODEL)
    cw = conv_w.astype(F32)
    cb = row(conv_b)
    nin, nssd, nmem, nf = row(norm_in_w), row(ssd_norm_w), row(mem_norm_w), row(norm_f_w)
    wssd, watt, wmem, wout = (w.astype(BF16) for w in (w_br_ssd, w_br_att, w_br_mem, w_out))
    lane = jnp.arange(LANES)
    e_ssd = (lane[:, None] == (jnp.arange(D_MODEL)[None, :] // SSD_HEAD_DIM)).astype(BF16)
    lse_lane_of_ch = jnp.repeat(jnp.array([_lse_lane(h) for h in range(ATT_HEADS)]), ATT_HEAD_DIM)
    e_att = (lane[:, None] == lse_lane_of_ch[None, :]).astype(BF16)
    e_att = jnp.concatenate([e_att, e_att], axis=0)

    xp2 = x_prompt.reshape(bp * seq, D_MODEL)
    mkv_f32, mkv_bf = _mem_kv(mem_prompt.reshape(bp * MEM_LEN, D_MODEL), nmem, w_mem_kv.astype(BF16))
    tm_p = 256
    tabs_p = _rope_tables(jnp.arange(seq))
    windows = tuple(min(w, seq) for w in ATT_WINDOWS)
    zx, q0, q1, q2, rest, dtr, kv0, kv1, kv2 = _in_proj(
        xp2, nin, wt, tabs_p, seq=seq, tm=tm_p, dils=dils, out_dtype=BF16, windows=windows)
    y_ssd, p_ssm = _ssd_prompt(zx, dtr, cw, cb, dtb, a_row, dtb_col, a_col, dsk, nssd, nbatch=bp, seq=seq)
    o_g, l_g = [], []
    for qkv, d in zip((q0, q1, q2), dils):
        o, l = _attn_prompt(qkv, d=d)
        o_g.append(o)
        l_g.append(l)
    y_prompt = _final_prompt(xp2, y_ssd, o_g, l_g, rest, mkv_bf.reshape(bp, MEM_LEN, 2 * MEM_WIDTH),
                             wssd, watt, wmem, wout, nf, e_att, seq=seq, tm=512, dils=dils, nsub=1)
    y_prompt = y_prompt.reshape(bp, seq, D_MODEL)

    p_win = [kv.reshape(bp, 2, ATT_HEADS, ATT_HEAD_DIM, kv.shape[-1]).transpose(0, 4, 1, 2, 3)
             for kv in (kv0, kv1, kv2)]
    p_mem_kv = mkv_f32.reshape(bp, MEM_LEN, 2, MEM_HEADS, MEM_HEAD_DIM)
    p_conv = zx.reshape(bp, seq, -1)[:, seq - (SSD_CONV - 1):, D_MODEL:].astype(F32)
    p_ssm = p_ssm.reshape(bp, SSD_N_HEADS, SSD_HEAD_DIM, SSD_D_STATE)

    ts = bs * tdec
    pos_s = PAST_LEN + jnp.tile(jnp.arange(tdec), bs)
    xs2 = x_sample.reshape(ts, D_MODEL)
    tabs_s = _rope_tables(pos_s)
    zx_s, q0_s, q1_s, q2_s, rest_s, dt_s = _in_proj(
        xs2, nin, wt, tabs_s, seq=ts, tm=min(ts, 128), dils=(1, 1, 1), out_dtype=F32)
    q_s = [a.reshape(ts, 3 * COL_TILE) for a in (q0_s, q1_s, q2_s)]
    hist = jnp.pad(state_conv.astype(F32), ((0, 0), (tdec - (SSD_CONV - 1), 0), (0, 0))).reshape(ts, SSD_CONV_DIM)
    ypart, ea, xdt, bmb, cmb, cd = _ssd_sample_rows(
        zx_s, dt_s, hist, cw, cb, dtb, a_row, dtb_col, a_col, dsk, e_ssd, tdec=tdec)
    cd_seq = cd.reshape(bs, tdec, LANES)[:, 0, :SSD_N_HEADS]
    y_ssd_s, s_ssm = _ssd_sample_state(cd_seq, zx_s, ypart, ea, xdt, bmb, cmb,
                                       state_ssm.reshape(bs, D_MODEL, SSD_D_STATE), nssd, tdec=tdec, gseq=8)
    to_kt = lambda c: c.transpose(0, 2, 3, 4, 1).reshape(bs, 2, ATT_WIDTH, c.shape[1])
    c0, c1, c2 = to_kt(cache_win128_kv), to_kt(cache_win512_kv), to_kt(cache_win2048_kv)
    cmem = cache_mem_kv.reshape(bs, MEM_LEN * 2 * MEM_HEADS, MEM_HEAD_DIM)
    o_att_s, o_mem_s = _attn_sample(*q_s, rest_s, c0, c1, c2, cmem, tdec=tdec)
    y_sample = _final_sample(xs2, y_ssd_s, o_att_s, o_mem_s, rest_s, wssd, watt, wmem, wout, nf, tm=min(ts, 256))
    y_sample = y_sample.reshape(bs, tdec, D_MODEL)

    s_win = [a[:, COL_TILE:].reshape(bs, tdec, 2, ATT_HEADS, ATT_HEAD_DIM) for a in q_s]
    s_conv = zx_s.reshape(bs, tdec, -1)[:, tdec - (SSD_CONV - 1):, D_MODEL:]
    s_ssm = s_ssm.reshape(bs, SSD_N_HEADS, SSD_HEAD_DIM, SSD_D_STATE)

    return (y_prompt, y_sample, p_win[0], p_win[1], p_win[2], p_mem_kv, p_conv, p_ssm,
            s_win[0], s_win[1], s_win[2], s_conv, s_ssm)
```

```python
import functools
import math

import jax
import jax.numpy as jnp
from jax import lax
from jax.experimental import pallas as pl
from jax.experimental.pallas import tpu as pltpu

F32 = jnp.float32
BF16 = jnp.bfloat16

D_MODEL = 1024
NORM_EPS = 1e-6
SSD_HEAD_DIM = 64
SSD_N_HEADS = 16
SSD_N_GROUPS = 4
SSD_D_STATE = 128
SSD_CONV = 4
SSD_CHUNK = 128
SSD_CONV_DIM = 2048
ATT_WINDOWS = (128, 512, 2048)
ATT_DILATIONS = (1, 4, 16)
ATT_HEADS = 8
ATT_HEAD_DIM = 64
ATT_WIDTH = 512
ROPE_DIM = 16
ROPE_THETA = 500000.0
MEM_LEN = 256
MEM_HEADS = 4
MEM_HEAD_DIM = 128
MEM_WIDTH = 512
NEG_INF = -1e30
PAST_LEN = 8192

LANES = 128
COL_TILE = 512
N_COL_TILES = 24
DT_COL0 = 3072
QKV_TILE0 = 6
REST_TILE0 = 15
VMEM_LIMIT = 56 * 1024 * 1024


def _cparams(sem):
    return pltpu.CompilerParams(dimension_semantics=sem, vmem_limit_bytes=VMEM_LIMIT)


def _dot(a, b):
    return jnp.dot(a, b, preferred_element_type=F32)


def _dot_nt(a, b):
    return lax.dot_general(a, b, (((1,), (1,)), ((), ())), preferred_element_type=F32)


def _rms(x, w):
    return x * lax.rsqrt(jnp.mean(x * x, axis=-1, keepdims=True) + NORM_EPS) * w


def _silu(x):
    u = 0.5 * x
    return u * jnp.tanh(u) + u


def _silu_of_half(u):
    return u * jnp.tanh(u) + u


LOG2E = 1.4426950408889634
LN2 = 0.6931471805599453


def _softplus(x):
    return jnp.maximum(x, 0.0) + jnp.log(1.0 + jnp.exp(-jnp.abs(x)))


def _split3(x):
    hi = x.astype(BF16)
    r1 = x - hi.astype(F32)
    mid = r1.astype(BF16)
    lo = (r1 - mid.astype(F32)).astype(BF16)
    return hi, mid, lo


def _dot_exact_lhs01(m01, x):
    hi, mid, lo = _split3(x)
    return _dot(m01, hi) + _dot(m01, mid) + _dot(m01, lo)


def _dot_exact_rhs01(x, m01):
    hi, mid, lo = _split3(x)
    return _dot(hi, m01) + _dot(mid, m01) + _dot(lo, m01)


def _mem_kv_kernel(x_ref, nw_ref, w_ref, o_ref, ob_ref):
    h = _rms(x_ref[...], nw_ref[...]).astype(BF16)
    acc = _dot(h, w_ref[...])
    o_ref[...] = acc
    ob_ref[...] = acc.astype(BF16)


def _mem_kv(mem2d, nw, w_bf):
    m = mem2d.shape[0]
    tm = 256
    return pl.pallas_call(
        _mem_kv_kernel,
        grid=(m // tm,),
        in_specs=[
            pl.BlockSpec((tm, D_MODEL), lambda i: (i, 0)),
            pl.BlockSpec((1, D_MODEL), lambda i: (0, 0)),
            pl.BlockSpec((D_MODEL, 2 * MEM_WIDTH), lambda i: (0, 0)),
        ],
        out_specs=[
            pl.BlockSpec((tm, 2 * MEM_WIDTH), lambda i: (i, 0)),
            pl.BlockSpec((tm, 2 * MEM_WIDTH), lambda i: (i, 0)),
        ],
        out_shape=[
            jax.ShapeDtypeStruct((m, 2 * MEM_WIDTH), F32),
            jax.ShapeDtypeStruct((m, 2 * MEM_WIDTH), BF16),
        ],
        compiler_params=_cparams(("parallel",)),
        name="mem_kv",
    )(mem2d, nw, w_bf)


def _rope_tables(pos):
    half = ROPE_DIM // 2
    inv = jnp.power(ROPE_THETA, -jnp.arange(half, dtype=F32) * 2.0 / ROPE_DIM)
    ang = pos.astype(F32)[:, None] * inv[None, :]
    cos, sin = jnp.cos(ang), jnp.sin(ang)
    m = jnp.arange(LANES) % ATT_HEAD_DIM
    idx = m % half
    c = jnp.where(m[None, :] < ROPE_DIM, cos[:, idx], 1.0)
    s1 = jnp.where(m[None, :] < half, -sin[:, idx], 0.0)
    s2 = jnp.where((m[None, :] >= half) & (m[None, :] < ROPE_DIM), sin[:, idx], 0.0)
    return c.astype(F32), s1.astype(F32), s2.astype(F32)


def _in_proj_kernel(x_ref, nw_ref, wt_ref, c_ref, s1_ref, s2_ref,
                    zx_ref, q0_ref, q1_ref, q2_ref, rest_ref, dt_ref, *more_refs, tm, dils, win_rows):
    pw_refs = more_refs[:3] if win_rows else ()
    h_ref, acc_ref, tmp_ref = more_refs[-3:]
    nslab = COL_TILE // LANES
    h_ref[...] = _rms(x_ref[...], nw_ref[...]).astype(BF16)
    dt_ref[...] = _dot_nt(h_ref[...], wt_ref[DT_COL0:DT_COL0 + LANES, :])
    q_refs = (q0_ref, q1_ref, q2_ref)
    strided = 0
    for j in range(N_COL_TILES):
        col0 = j * COL_TILE + (SSD_N_HEADS if j >= QKV_TILE0 else 0)
        acc = _dot_nt(h_ref[...], wt_ref[col0:col0 + COL_TILE, :])
        if j < QKV_TILE0:
            zx_ref[:, j * COL_TILE:(j + 1) * COL_TILE] = acc.astype(zx_ref.dtype)
            continue
        if j >= REST_TILE0:
            rest_ref[:, (j - REST_TILE0) * COL_TILE:(j - REST_TILE0 + 1) * COL_TILE] = acc.astype(rest_ref.dtype)
            continue
        g, kind = divmod(j - QKV_TILE0, 3)
        out_ref, d = q_refs[g], dils[g]
        for c in range(nslab):
            a = acc[:, c * LANES:(c + 1) * LANES]
            if kind < 2:
                a = (a * c_ref[...] + pltpu.roll(a, LANES - ROPE_DIM // 2, 1) * s1_ref[...]
                     + pltpu.roll(a, ROPE_DIM // 2, 1) * s2_ref[...])
            if kind == 0:
                a = a * (ATT_HEAD_DIM ** -0.5 * LOG2E)
            lanes = slice(kind * COL_TILE + c * LANES, kind * COL_TILE + (c + 1) * LANES)
            if win_rows and kind > 0:
                ch0 = (kind - 1) * COL_TILE + c * LANES
                pw_refs[g][ch0:ch0 + LANES, :] = a[tm - win_rows[g]:, :].T
            if d == 1:
                out_ref[0, :, lanes] = a.astype(out_ref.dtype)
            elif d == 4:
                acc_ref[strided, c] = a
                for r in range(4):
                    out_ref[r, :, lanes] = acc_ref[strided, c, pl.ds(r, tm // 4, stride=4), :].astype(out_ref.dtype)
            else:
                acc_ref[strided, c] = a
                q4 = tm // 4
                for r1 in range(4):
                    tmp_ref[kind, c, r1 * q4:(r1 + 1) * q4, :] = acc_ref[strided, c, pl.ds(r1, q4, stride=4), :]
                for r1 in range(4):
                    for r2 in range(4):
                        out_ref[r1 + 4 * r2, :, lanes] = tmp_ref[
                            kind, c, pl.ds(r1 * q4 + r2, tm // 16, stride=4), :].astype(out_ref.dtype)
        if d > 1:
            strided += 1


def _in_proj(x2d, nw, wt, tabs, *, seq, tm, dils, out_dtype, windows=None):
    t = x2d.shape[0]
    nb = t // seq
    tiles_per_seq = seq // tm
    tab_blocks = tabs[0].shape[0] // tm
    n_strided = 3 * sum(1 for d in dils if d > 1)
    nslab = COL_TILE // LANES

    def qkv_spec(d):
        return pl.BlockSpec((None, d, tm // d, 3 * COL_TILE),
                            lambda i: (i // tiles_per_seq, 0, i % tiles_per_seq, 0))

    win_rows, win_specs, win_shapes = None, [], []
    if windows:
        win_rows = tuple(min(tm, w) for w in windows)
        for w, wr in zip(windows, win_rows):
            first = tiles_per_seq - w // wr
            win_specs.append(pl.BlockSpec(
                (None, 2 * COL_TILE, wr),
                lambda i, first=first: (i // tiles_per_seq, 0, jnp.maximum(i % tiles_per_seq - first, 0))))
            win_shapes.append(jax.ShapeDtypeStruct((nb, 2 * COL_TILE, w), F32))

    tab_spec = pl.BlockSpec((tm, LANES), lambda i: (i % tab_blocks, 0))
    resident = lambda shape: pl.BlockSpec(shape, lambda i: (0, 0), pipeline_mode=pl.Buffered(1))
    kern = functools.partial(_in_proj_kernel, tm=tm, dils=dils, win_rows=win_rows)
    n_zx, n_rest = QKV_TILE0 * COL_TILE, (N_COL_TILES - REST_TILE0) * COL_TILE
    return pl.pallas_call(
        kern,
        grid=(t // tm,),
        in_specs=[
            pl.BlockSpec((tm, D_MODEL), lambda i: (i, 0)),
            resident((1, D_MODEL)),
            resident(wt.shape),
            tab_spec, tab_spec, tab_spec,
        ],
        out_specs=[
            pl.BlockSpec((tm, n_zx), lambda i: (i, 0)),
            qkv_spec(dils[0]), qkv_spec(dils[1]), qkv_spec(dils[2]),
            pl.BlockSpec((tm, n_rest), lambda i: (i, 0)),
            pl.BlockSpec((tm, LANES), lambda i: (i, 0)),
        ] + win_specs,
        out_shape=[
            jax.ShapeDtypeStruct((t, n_zx), out_dtype),
            jax.ShapeDtypeStruct((nb, dils[0], seq // dils[0], 3 * COL_TILE), out_dtype),
            jax.ShapeDtypeStruct((nb, dils[1], seq // dils[1], 3 * COL_TILE), out_dtype),
            jax.ShapeDtypeStruct((nb, dils[2], seq // dils[2], 3 * COL_TILE), out_dtype),
            jax.ShapeDtypeStruct((t, n_rest), out_dtype),
            jax.ShapeDtypeStruct((t, LANES), F32),
        ] + win_shapes,
        scratch_shapes=[
            pltpu.VMEM((tm, D_MODEL), BF16),
            pltpu.VMEM((max(n_strided, 1), nslab, tm, LANES), F32),
            pltpu.VMEM((3, nslab, tm, LANES), F32),
        ],
        compiler_params=_cparams(("arbitrary",)),
        name="in_proj",
    )(x2d, nw, wt, *tabs)


def _ssd_small(dt_raw, dtb_row, a_row, dtb_col, a_col, same01, same01_t, tot01, tot01_t):
    dt = _softplus(dt_raw + dtb_row)
    dta = dt * a_row
    acs = _dot_exact_lhs01(same01, dta)
    tot = _dot_exact_lhs01(tot01, dta)
    dtd = dt * jnp.exp2(tot - acs)
    dt_raw_t = dt_raw.T
    dt_t = _softplus(dt_raw_t + dtb_col)
    dta_t = dt_t * a_col
    acs_t = _dot_exact_rhs01(dta_t, same01_t)
    tot_t = _dot_exact_rhs01(dta_t, tot01_t)
    dtd_t = dt_t * jnp.exp2(tot_t - acs_t)
    return dt, acs, tot, dtd, dt_t, acs_t, dtd_t


def _ssd_pairs(x, bm, cm, acs, acs_t, dt_t, dtd_t, mask, st_ref, y_ref_write, st_scale_row, after_pair=None):
    lane = lax.broadcasted_iota(jnp.int32, (1, LANES), 1)
    half = [lane < ATT_HEAD_DIM, lane >= ATT_HEAD_DIM]
    bm_t = [bm[:, g * SSD_D_STATE:(g + 1) * SSD_D_STATE].T for g in range(SSD_N_GROUPS)]
    cms = [cm[:, g * SSD_D_STATE:(g + 1) * SSD_D_STATE] for g in range(SSD_N_GROUPS)]
    cb = [_dot(cms[g].astype(BF16), bm_t[g].astype(BF16)) for g in range(SSD_N_GROUPS)]
    for p in range(SSD_N_HEADS // 2):
        g = p // 2
        xp = x[:, p * LANES:(p + 1) * LANES]
        if st_ref is not None:
            stp = st_ref[:, p * LANES:(p + 1) * LANES]
        y_pair = jnp.zeros((SSD_CHUNK, LANES), F32)
        st_add = jnp.zeros((SSD_D_STATE, LANES), F32)
        for hh in range(2):
            h = 2 * p + hh
            col = acs[:, h:h + 1]
            row = acs_t[h:h + 1, :]
            lmat = jnp.exp2(jnp.where(mask, col - row, NEG_INF))
            m = cb[g] * lmat * dt_t[h:h + 1, :]
            xm = jnp.where(half[hh], xp, 0.0).astype(BF16)
            if st_ref is not None:
                cx = cms[g] * jnp.exp2(col)
                lhs = jnp.concatenate([m, cx], axis=1).astype(BF16)
                rhs = jnp.concatenate([xm, jnp.where(half[hh], stp, 0.0).astype(BF16)], axis=0)
                y_pair = y_pair + _dot(lhs, rhs)
                bt_h = (bm_t[g] * dtd_t[h:h + 1, :]).astype(BF16)
                st_add = st_add + _dot(bt_h, xm)
            else:
                y_pair = y_pair + _dot(m.astype(BF16), xm)
        y_ref_write(p, y_pair)
        if st_ref is not None:
            scale = jnp.where(half[0], st_scale_row[:, 2 * p:2 * p + 1], st_scale_row[:, 2 * p + 1:2 * p + 2])
            st_ref[:, p * LANES:(p + 1) * LANES] = stp * scale + st_add
        if after_pair is not None:
            after_pair(p)


def _ssd_prompt_kernel(z_ref, xs_ref, bc_ref, xsn_ref, bcn_ref, dt_ref, cw_ref, cb_ref, dtb_ref, a_ref, dtbc_ref,
                       ac_ref, dsk_ref, nw_ref, sh_ref, y_ref, ssm_ref, xc_ref, st_ref, ybuf, *, ncs):
    c = pl.program_id(1)
    q = SSD_CHUNK

    def conv_act(prev, cur, lanes):
        shifted = _dot(sh_ref[...], jnp.concatenate([prev, cur], axis=0))
        acc = cb_ref[:, lanes] + cw_ref[SSD_CONV - 1:SSD_CONV, lanes] * cur.astype(F32)
        for s in range(1, SSD_CONV):
            acc = acc + cw_ref[SSD_CONV - 1 - s:SSD_CONV - s, lanes] * shifted[(s - 1) * q:s * q]
        return _silu(acc)

    @pl.when(c == 0)
    def _():
        st_ref[...] = jnp.zeros(st_ref.shape, F32)
        no_history = jnp.zeros((q, D_MODEL), xs_ref.dtype)
        xc_ref[0, :, 0:D_MODEL] = conv_act(no_history, xs_ref[0:q, :], slice(0, D_MODEL))
        xc_ref[0, :, D_MODEL:SSD_CONV_DIM] = conv_act(no_history, bc_ref[0:q, :], slice(D_MODEL, SSD_CONV_DIM))

    slab = SSD_CONV_DIM // (SSD_N_HEADS // 2)
    ri = lax.broadcasted_iota(jnp.int32, (q, q), 0)
    ci = lax.broadcasted_iota(jnp.int32, (q, q), 1)
    tril = ri >= ci
    tril01 = jnp.where(tril, 1.0, 0.0).astype(BF16)
    triu01 = jnp.where(ri <= ci, 1.0, 0.0).astype(BF16)
    ones01 = jnp.ones((q, q), BF16)

    def write_y(p, y_pair):
        ybuf[:, p * LANES:(p + 1) * LANES] = y_pair

    for i in range(ncs):
        rows = slice(i * q, (i + 1) * q)

        def conv_next(p, i=i, rows=rows):
            lanes = slice(p * slab, (p + 1) * slab)
            src_ref, nxt_ref, off = (xs_ref, xsn_ref, 0) if lanes.start < D_MODEL else (bc_ref, bcn_ref, D_MODEL)
            src = slice(lanes.start - off, lanes.stop - off)
            cur = src_ref[(i + 1) * q:(i + 2) * q, src] if i + 1 < ncs else nxt_ref[:, src]
            xc_ref[(i + 1) % ncs, :, lanes] = conv_act(src_ref[rows, src], cur, lanes)

        x = xc_ref[i, :, 0:D_MODEL]
        bm = xc_ref[i, :, D_MODEL:D_MODEL + 512]
        cm = xc_ref[i, :, D_MODEL + 512:SSD_CONV_DIM]
        dt, acs, tot, dtd, dt_t, acs_t, dtd_t = _ssd_small(
            dt_ref[rows, :], dtb_ref[...], a_ref[...], dtbc_ref[...], ac_ref[...], tril01, triu01, ones01, ones01)
        decay_row = jnp.exp2(tot[0:1, :])
        _ssd_pairs(x, bm, cm, acs, acs_t, dt_t, dtd_t, tril, st_ref, write_y, decay_row, after_pair=conv_next)
        y = (ybuf[...] + dsk_ref[...] * x) * _silu_of_half(z_ref[rows, :].astype(F32))
        y_ref[rows, :] = _rms(y, nw_ref[...]).astype(y_ref.dtype)

    @pl.when(c == pl.num_programs(1) - 1)
    def _():
        for p in range(D_MODEL // LANES):
            ssm_ref[p * LANES:(p + 1) * LANES, :] = st_ref[:, p * LANES:(p + 1) * LANES].T


def _ssd_prompt(zx, dt, cw, cb, dtb, a_row, dtb_col, a_col, dsk, nw, *, nbatch, seq):
    q = SSD_CHUNK
    ncs = next(n for n in (8, 4, 2, 1) if (seq // q) % n == 0)
    rows = ncs * q
    nc = seq // q
    ns = nc // ncs
    row = lambda b, c: b * ns + c
    nxt = lambda b, c: b * nc + jnp.minimum((c + 1) * ncs, nc - 1)
    vec = lambda n: pl.BlockSpec((1, n), lambda b, c: (0, 0))
    col = pl.BlockSpec((LANES, 1), lambda b, c: (0, 0))
    nsh = SSD_CONV - 1
    out_t = jnp.arange(nsh * q) % q
    shift = 1 + jnp.arange(nsh * q) // q
    shift01 = (jnp.arange(2 * q)[None, :] == (q + out_t - shift)[:, None]).astype(BF16)
    return pl.pallas_call(
        functools.partial(_ssd_prompt_kernel, ncs=ncs),
        grid=(nbatch, ns),
        in_specs=[
            pl.BlockSpec((rows, D_MODEL), lambda b, c: (row(b, c), 0)),
            pl.BlockSpec((rows, D_MODEL), lambda b, c: (row(b, c), 1)),
            pl.BlockSpec((rows, D_MODEL), lambda b, c: (row(b, c), 2)),
            pl.BlockSpec((q, D_MODEL), lambda b, c: (nxt(b, c), 1)),
            pl.BlockSpec((q, D_MODEL), lambda b, c: (nxt(b, c), 2)),
            pl.BlockSpec((rows, LANES), lambda b, c: (row(b, c), 0)),
            pl.BlockSpec((SSD_CONV, SSD_CONV_DIM), lambda b, c: (0, 0)),
            vec(SSD_CONV_DIM), vec(LANES), vec(LANES), col, col, vec(D_MODEL), vec(D_MODEL),
            pl.BlockSpec(((SSD_CONV - 1) * q, 2 * q), lambda b, c: (0, 0)),
        ],
        out_specs=[
            pl.BlockSpec((rows, D_MODEL), lambda b, c: (row(b, c), 0)),
            pl.BlockSpec((None, D_MODEL, SSD_D_STATE), lambda b, c: (b, 0, 0)),
        ],
        out_shape=[
            jax.ShapeDtypeStruct((nbatch * seq, D_MODEL), BF16),
            jax.ShapeDtypeStruct((nbatch, D_MODEL, SSD_D_STATE), F32),
        ],
        scratch_shapes=[
            pltpu.VMEM((max(ncs, 2), q, SSD_CONV_DIM), F32),
            pltpu.VMEM((SSD_D_STATE, D_MODEL), F32),
            pltpu.VMEM((q, D_MODEL), F32),
        ],
        compiler_params=_cparams(("parallel", "arbitrary")),
        name="ssd_prompt",
    )(zx, zx, zx, zx, zx, dt, cw, cb, dtb, a_row, dtb_col, a_col, dsk, nw, shift01)


def _expand_heads(v, e01):
    return _dot_exact_rhs01(v, e01)


def _ssd_sample_rows_kernel(xs_ref, bc_ref, dt_ref, hist_ref, cw_ref, cb_ref, dtb_ref, a_ref, dtbc_ref, ac_ref,
                            dsk_ref, e01_ref,
                            ypart_ref, ea_ref, xdt_ref, bm_ref, cm_ref, cd_ref, ybuf, *, tdec):
    q = SSD_CHUNK
    x_raw = jnp.concatenate([xs_ref[...], bc_ref[...]], axis=1)
    hist = hist_ref[...]
    rowi = lax.broadcasted_iota(jnp.int32, (q, 1), 0)
    tpos = rowi % tdec
    acc = cb_ref[...] + cw_ref[SSD_CONV - 1:SSD_CONV, :] * x_raw
    for s in range(1, SSD_CONV):
        prev = jnp.where(tpos >= s, pltpu.roll(x_raw, s, 0), pltpu.roll(hist, q - tdec + s, 0))
        acc = acc + cw_ref[SSD_CONV - 1 - s:SSD_CONV - s, :] * prev
    xc = _silu(acc)
    x = xc[:, :D_MODEL]
    bm = xc[:, D_MODEL:D_MODEL + 512]
    cm = xc[:, D_MODEL + 512:]

    ri = lax.broadcasted_iota(jnp.int32, (q, q), 0)
    ci = lax.broadcasted_iota(jnp.int32, (q, q), 1)
    same = (ri // tdec) == (ci // tdec)
    mask = same & (ri >= ci)
    low01 = jnp.where(mask, 1.0, 0.0).astype(BF16)
    up01 = jnp.where(same & (ri <= ci), 1.0, 0.0).astype(BF16)
    same01 = jnp.where(same, 1.0, 0.0).astype(BF16)
    dt, acs, tot, dtd, dt_t, acs_t, dtd_t = _ssd_small(
        dt_ref[...], dtb_ref[...], a_ref[...], dtbc_ref[...], ac_ref[...], low01, up01, same01, same01)

    def write_y(p, y_pair):
        ybuf[:, p * LANES:(p + 1) * LANES] = y_pair

    _ssd_pairs(x, bm, cm, acs, acs_t, dt_t, dtd_t, mask, None, write_y, None)

    e01 = e01_ref[...]
    ypart_ref[...] = ybuf[...] + dsk_ref[...] * x
    ea_ref[...] = _expand_heads(jnp.exp2(acs), e01)
    xd = x * _expand_heads(dtd, e01)
    for p in range(D_MODEL // LANES):
        xdt_ref[p * LANES:(p + 1) * LANES, :] = xd[:, p * LANES:(p + 1) * LANES].T.astype(BF16)
    bm_ref[...] = bm.astype(BF16)
    cm_ref[...] = cm.astype(BF16)
    cd_ref[...] = jnp.exp2(tot)


def _ssd_sample_rows(zx, dt, hist, cw, cb, dtb, a_row, dtb_col, a_col, dsk, e01, *, tdec):
    t = zx.shape[0]
    q = SSD_CHUNK
    vec = lambda n: pl.BlockSpec((1, n), lambda i: (0, 0))
    col = pl.BlockSpec((LANES, 1), lambda i: (0, 0))
    kern = functools.partial(_ssd_sample_rows_kernel, tdec=tdec)
    return pl.pallas_call(
        kern,
        grid=(t // q,),
        in_specs=[
            pl.BlockSpec((q, D_MODEL), lambda i: (i, 1)),
            pl.BlockSpec((q, D_MODEL), lambda i: (i, 2)),
            pl.BlockSpec((q, LANES), lambda i: (i, 0)),
            pl.BlockSpec((q, SSD_CONV_DIM), lambda i: (i, 0)),
            pl.BlockSpec((SSD_CONV, SSD_CONV_DIM), lambda i: (0, 0)),
            vec(SSD_CONV_DIM), vec(LANES), vec(LANES), col, col, vec(D_MODEL),
            pl.BlockSpec((LANES, D_MODEL), lambda i: (0, 0)),
        ],
        out_specs=[
            pl.BlockSpec((q, D_MODEL), lambda i: (i, 0)),
            pl.BlockSpec((q, D_MODEL), lambda i: (i, 0)),
            pl.BlockSpec((D_MODEL, q), lambda i: (0, i)),
            pl.BlockSpec((q, 512), lambda i: (i, 0)),
            pl.BlockSpec((q, 512), lambda i: (i, 0)),
            pl.BlockSpec((q, LANES), lambda i: (i, 0)),
        ],
        out_shape=[
            jax.ShapeDtypeStruct((t, D_MODEL), F32),
            jax.ShapeDtypeStruct((t, D_MODEL), F32),
            jax.ShapeDtypeStruct((D_MODEL, t), BF16),
            jax.ShapeDtypeStruct((t, 512), BF16),
            jax.ShapeDtypeStruct((t, 512), BF16),
            jax.ShapeDtypeStruct((t, LANES), F32),
        ],
        scratch_shapes=[pltpu.VMEM((q, D_MODEL), F32)],
        compiler_params=_cparams(("parallel",)),
        name="ssd_sample_rows",
    )(zx, zx, dt, hist, cw, cb, dtb, a_row, dtb_col, a_col, dsk, e01)


def _ssd_sample_state_kernel(cd_ref, z_ref, ypart_ref, ea_ref, xdt_ref, bm_ref, cm_ref, st_ref, nw_ref,
                             y_ref, sto_ref, *, tdec, gseq):
    i = pl.program_id(0)
    q = SSD_CHUNK
    rows = gseq * tdec
    steps_per_blk = q // rows
    lane = lax.broadcasted_iota(jnp.int32, (1, q), 1)
    rg = 16
    rowi = lax.broadcasted_iota(jnp.int32, (rg, 1), 0)
    lane_base = (i % steps_per_blk) * rows
    seq_per_rg = rg // tdec
    yoff_groups = []
    for q8 in range(rows // rg):
        yoff8 = jnp.zeros((rg, D_MODEL), F32)
        for sj in range(seq_per_rg):
            jj = q8 * seq_per_rg + sj
            b = i * gseq + jj
            lmask = (lane >= lane_base + jj * tdec) & (lane < lane_base + (jj + 1) * tdec)
            parts = []
            for g in range(SSD_N_GROUPS):
                s_g = st_ref[jj, g * 256:(g + 1) * 256, :]
                c8 = cm_ref[q8 * rg:(q8 + 1) * rg, g * SSD_D_STATE:(g + 1) * SSD_D_STATE]
                parts.append(_dot_nt(c8, s_g.astype(BF16)))
                lhs = jnp.where(lmask, xdt_ref[g * 256:(g + 1) * 256, :], jnp.zeros((), BF16))
                add = _dot(lhs, bm_ref[:, g * SSD_D_STATE:(g + 1) * SSD_D_STATE])
                for hh in range(4):
                    h = 4 * g + hh
                    sl = slice(hh * SSD_HEAD_DIM, (hh + 1) * SSD_HEAD_DIM)
                    sto_ref[jj, g * 256 + hh * 64:g * 256 + (hh + 1) * 64, :] = (
                        s_g[sl, :] * cd_ref[b, h] + add[sl, :])
            yo = jnp.concatenate(parts, axis=1)
            in_seq = (rowi >= sj * tdec) & (rowi < (sj + 1) * tdec)
            yoff8 = jnp.where(in_seq, yo, yoff8)
        yoff_groups.append(yoff8)
    yoff = jnp.concatenate(yoff_groups, axis=0) if len(yoff_groups) > 1 else yoff_groups[0]
    y = (ypart_ref[...] + yoff * ea_ref[...]) * _silu_of_half(z_ref[...])
    y_ref[...] = _rms(y, nw_ref[...])


def _ssd_sample_state(cd, zx, ypart, ea, xdt, bmb, cmb, state, nw, *, tdec, gseq):
    t = zx.shape[0]
    nseq = t // tdec
    q = SSD_CHUNK
    rows = gseq * tdec
    spb = q // rows
    kern = functools.partial(_ssd_sample_state_kernel, tdec=tdec, gseq=gseq)
    return pl.pallas_call(
        kern,
        grid=(nseq // gseq,),
        in_specs=[
            pl.BlockSpec(memory_space=pltpu.SMEM),
            pl.BlockSpec((rows, D_MODEL), lambda i: (i, 0)),
            pl.BlockSpec((rows, D_MODEL), lambda i: (i, 0)),
            pl.BlockSpec((rows, D_MODEL), lambda i: (i, 0)),
            pl.BlockSpec((D_MODEL, q), lambda i: (0, i // spb)),
            pl.BlockSpec((q, 512), lambda i: (i // spb, 0)),
            pl.BlockSpec((rows, 512), lambda i: (i, 0)),
            pl.BlockSpec((gseq, D_MODEL, SSD_D_STATE), lambda i: (i, 0, 0)),
            pl.BlockSpec((1, D_MODEL), lambda i: (0, 0)),
        ],
        out_specs=[
            pl.BlockSpec((rows, D_MODEL), lambda i: (i, 0)),
            pl.BlockSpec((gseq, D_MODEL, SSD_D_STATE), lambda i: (i, 0, 0)),
        ],
        out_shape=[
            jax.ShapeDtypeStruct((t, D_MODEL), F32),
            jax.ShapeDtypeStruct((nseq, D_MODEL, SSD_D_STATE), F32),
        ],
        compiler_params=_cparams(("parallel",)),
        name="ssd_sample_state",
    )(cd, zx, ypart, ea, xdt, bmb, cmb, state, nw)


def _lse_lane(h):
    return 16 * h


def _attn_prompt_kernel(q_ref, kc_ref, kp_ref, vc_ref, vp_ref, o_ref, lse_ref, *, nr, nq):
    n = pl.program_id(2)
    nk = 128
    ks = [jnp.concatenate([kp_ref[rr], kc_ref[rr]], axis=0) for rr in range(nr)]
    vs = [jnp.concatenate([vp_ref[rr], vc_ref[rr]], axis=0) for rr in range(nr)]
    lane = lax.broadcasted_iota(jnp.int32, (1, LANES), 1)
    lo = lane < ATT_HEAD_DIM
    qi = lax.broadcasted_iota(jnp.int32, (2 * nk, 2 * nk), 0) % nk
    kj = lax.broadcasted_iota(jnp.int32, (2 * nk, 2 * nk), 1)
    in_band = (kj >= qi) & (kj <= qi + nk)
    band = jnp.where(in_band, 0.0, NEG_INF)
    band_first = jnp.where(in_band & ((kj >= nk) | (n > 0)), 0.0, NEG_INF)
    zero = jnp.zeros((), q_ref.dtype)
    npairs = ATT_HEADS // 2
    pair_lanes = [slice(p * LANES, (p + 1) * LANES) for p in range(npairs)]
    units = [(rr, j, p) for rr in range(nr) for j in range(nq) for p in range(npairs)]

    def scores(u):
        rr, j, p = u
        qp = q_ref[rr, j * nk:(j + 1) * nk, pair_lanes[p]]
        q2 = jnp.concatenate([jnp.where(lo, qp, zero), jnp.where(lo, zero, qp)], axis=0)
        kk = ks[rr][j * nk:(j + 2) * nk, pair_lanes[p]]
        return _dot_nt(q2, kk) + (band_first if j == 0 else band)

    def softmax(s):
        m = jnp.max(s, axis=-1, keepdims=True)
        e = jnp.exp2(s - m)
        den = jnp.sum(e, axis=-1, keepdims=True)
        return e.astype(BF16), den, m

    ahead = 3
    s_next = [scores(u) for u in units[:ahead]]
    m_full = den_full = None
    o_parts = []
    for i, (rr, j, p) in enumerate(units):
        e, den, m = softmax(s_next.pop(0))
        if i + ahead < len(units):
            s_next.append(scores(units[i + ahead]))
        r = _dot(e, vs[rr][j * nk:(j + 2) * nk, pair_lanes[p]])
        den_pair = jnp.where(lo, den[0:nk], den[nk:2 * nk])
        o_parts.append((jnp.where(lo, r[0:nk], r[nk:2 * nk]) / den_pair).astype(o_ref.dtype))
        if p == 0:
            m_full = jnp.zeros((nk, LANES), F32)
            den_full = jnp.ones((nk, LANES), F32)
        for hh in range(2):
            base = _lse_lane(2 * p + hh)
            grp = (lane >= base) & (lane < base + 16)
            m_full = jnp.where(grp, m[hh * nk:(hh + 1) * nk], m_full)
            den_full = jnp.where(grp, den[hh * nk:(hh + 1) * nk], den_full)
        if p == npairs - 1:
            o_ref[rr, j * nk:(j + 1) * nk, :] = jnp.concatenate(o_parts, axis=1)
            lse_ref[rr, j * nk:(j + 1) * nk, :] = (m_full + jnp.log2(den_full)) * LN2
            o_parts = []


ATT_BLOCKS_PER_STEP = 8


def _attn_prompt(qkv, *, d):
    nb, _, length, _ = qkv.shape
    nk = 128
    nq = next(c for c in (8, 4, 2, 1) if c <= ATT_BLOCKS_PER_STEP and length % (c * nk) == 0)
    nr = next(c for c in (8, 4, 2, 1) if c * nq <= ATT_BLOCKS_PER_STEP and d % c == 0)
    nblk = length // (nq * nk)
    cur = lambda which: pl.BlockSpec((None, nr, nq * nk, COL_TILE), lambda b, r, n: (b, r, n, which))
    prev = lambda which: pl.BlockSpec((None, nr, nk, COL_TILE),
                                      lambda b, r, n: (b, r, jnp.maximum(nq * n - 1, 0), which))
    return pl.pallas_call(
        functools.partial(_attn_prompt_kernel, nr=nr, nq=nq),
        grid=(nb, d // nr, nblk),
        in_specs=[cur(0), cur(1), prev(1), cur(2), prev(2)],
        out_specs=[
            pl.BlockSpec((None, nr, nq * nk, ATT_WIDTH), lambda b, r, n: (b, r, n, 0)),
            pl.BlockSpec((None, nr, nq * nk, LANES), lambda b, r, n: (b, r, n, 0)),
        ],
        out_shape=[
            jax.ShapeDtypeStruct((nb, d, length, ATT_WIDTH), BF16),
            jax.ShapeDtypeStruct((nb, d, length, LANES), F32),
        ],
        compiler_params=_cparams(("parallel", "parallel", "arbitrary")),
        name=f"attn_prompt_d{d}",
    )(qkv, qkv, qkv, qkv, qkv)


def _attn_sample_kernel(q0_ref, q1_ref, q2_ref, qm_ref, c0_ref, c1_ref, c2_ref, cm_ref, oa_ref, om_ref, *, tdec):
    nh = ATT_HEADS
    rows8 = 8
    j = pl.program_id(1)
    r64 = nh * rows8
    rowi = lax.broadcasted_iota(jnp.int32, (r64, 1), 0)
    head_of_row = rowi // rows8
    r8 = rowi % rows8
    seq_q = r8 // tdec
    t_q = r8 % tdec
    lane512 = lax.broadcasted_iota(jnp.int32, (1, ATT_WIDTH), 1)
    headmask = (lane512 // ATT_HEAD_DIM) == head_of_row
    col8 = lax.broadcasted_iota(jnp.int32, (1, rows8), 1)
    seq_k = col8 // tdec
    t_k = col8 % tdec

    outs, lses = [], []
    for g, (q_ref, c_ref, d) in enumerate(zip((q0_ref, q1_ref, q2_ref), (c0_ref, c1_ref, c2_ref), ATT_DILATIONS)):
        w = c_ref.shape[-1]
        qkv = q_ref[...]
        qv = qkv[:, :ATT_WIDTH]
        kn = qkv[:, ATT_WIDTH:2 * ATT_WIDTH]
        vn = qkv[:, 2 * ATT_WIDTH:]
        q64 = jnp.where(headmask, jnp.concatenate([qv] * nh, axis=0), 0.0)
        s_new = _dot_nt(q64, kn)
        if d == 1:
            valid_new = (seq_k == seq_q) & (t_k <= t_q)
        else:
            valid_new = (seq_k == seq_q) & (t_k == t_q)
        s_new = jnp.where(valid_new, s_new, NEG_INF)
        wi = lax.broadcasted_iota(jnp.int32, (1, w), 1)
        valid_c = (wi >= t_q) if d == 1 else ((wi % d) == t_q)
        s_c = jnp.where(valid_c, _dot(q64.astype(BF16), c_ref[0].astype(BF16)), NEG_INF)
        m = jnp.maximum(jnp.max(s_c, axis=-1, keepdims=True), jnp.max(s_new, axis=-1, keepdims=True))
        e_c = jnp.exp2(s_c - m)
        e_n = jnp.exp2(s_new - m)
        den = jnp.sum(e_c, axis=-1, keepdims=True) + jnp.sum(e_n, axis=-1, keepdims=True)
        o = _dot(e_n, vn) + _dot_nt(e_c.astype(BF16), c_ref[1].astype(BF16))
        outs.append(o / den)
        lses.append((m + jnp.log2(den)) * LN2)

    mx = jnp.maximum(jnp.maximum(lses[0], lses[1]), lses[2])
    ws = [jnp.exp(l - mx) for l in lses]
    om = (ws[0] * outs[0] + ws[1] * outs[1] + ws[2] * outs[2]) / (ws[0] + ws[1] + ws[2])
    om = jnp.where(headmask, om, 0.0)
    o8 = om[0:rows8]
    for h in range(1, nh):
        o8 = o8 + om[h * rows8:(h + 1) * rows8]

    qm = jnp.concatenate([qm_ref[...] * (MEM_HEAD_DIM ** -0.5), jnp.zeros((rows8, MEM_WIDTH), F32)], axis=0)
    mparts = []
    for h in range(MEM_HEADS):
        kh = cm_ref[pl.ds(h, MEM_LEN, stride=2 * MEM_HEADS), :].astype(BF16)
        vh = cm_ref[pl.ds(MEM_HEADS + h, MEM_LEN, stride=2 * MEM_HEADS), :].astype(BF16)
        s = _dot_nt(qm[:, h * MEM_HEAD_DIM:(h + 1) * MEM_HEAD_DIM].astype(BF16), kh)
        mm = jnp.max(s, axis=-1, keepdims=True)
        e = jnp.exp(s - mm)
        den = jnp.sum(e, axis=-1, keepdims=True)
        mparts.append((_dot(e.astype(BF16), vh) / den)[0:rows8])
    m8 = jnp.concatenate(mparts, axis=1)

    row8 = lax.broadcasted_iota(jnp.int32, (rows8, 1), 0)
    mine = (row8 // tdec) == j

    @pl.when(j == 0)
    def _():
        oa_ref[...] = jnp.where(mine, o8, 0.0)
        om_ref[...] = jnp.where(mine, m8, 0.0)

    @pl.when(j > 0)
    def _():
        oa_ref[...] = jnp.where(mine, o8, oa_ref[...])
        om_ref[...] = jnp.where(mine, m8, om_ref[...])


def _attn_sample(q0, q1, q2, rest, c0, c1, c2, cmem, *, tdec):
    t = q0.shape[0]
    spg = 8 // tdec
    row_spec = pl.BlockSpec((8, 3 * COL_TILE), lambda i, j: (i, 0))
    cache_spec = lambda c: pl.BlockSpec((None,) + c.shape[1:], lambda i, j: (i * spg + j,) + (0,) * (c.ndim - 1))
    kern = functools.partial(_attn_sample_kernel, tdec=tdec)
    return pl.pallas_call(
        kern,
        grid=(t // 8, spg),
        in_specs=[
            row_spec, row_spec, row_spec,
            pl.BlockSpec((8, COL_TILE), lambda i, j: (i, 1)),
            cache_spec(c0), cache_spec(c1), cache_spec(c2), cache_spec(cmem),
        ],
        out_specs=[
            pl.BlockSpec((8, ATT_WIDTH), lambda i, j: (i, 0)),
            pl.BlockSpec((8, MEM_WIDTH), lambda i, j: (i, 0)),
        ],
        out_shape=[
            jax.ShapeDtypeStruct((t, ATT_WIDTH), F32),
            jax.ShapeDtypeStruct((t, MEM_WIDTH), F32),
        ],
        compiler_params=_cparams(("parallel", "arbitrary")),
        name="attn_sample",
    )(q0, q1, q2, rest, c0, c1, c2, cmem)


def _final_tail(x, p_ssd, y_att, y_mem, gate_raw, watt_ref, wmem_ref, wout_ref, nf_ref):
    t = jnp.tanh(gate_raw)
    p_att = _dot(y_att.astype(BF16), watt_ref[...])
    p_mem = _dot(y_mem.astype(BF16), wmem_ref[...])
    merged = ((t[:, 0:D_MODEL] * p_ssd + p_ssd) + (t[:, D_MODEL:2 * D_MODEL] * p_att + p_att)
              + (t[:, 2 * D_MODEL:] * p_mem + p_mem))
    return _rms(x + _dot(merged.astype(BF16), wout_ref[...]), nf_ref[...])


def _final_prompt_kernel(x_ref, ys_ref, o0_ref, o1_ref, o2_ref, l0_ref, l1_ref, l2_ref, rest_ref, mkv_ref,
                         wssd_ref, watt_ref, wmem_ref, wout_ref, nf_ref, e16_ref,
                         y_ref, obuf, lbuf, *, tm, dils, nsub):
    nslab = ATT_WIDTH // LANES
    o_refs, l_refs = (o0_ref, o1_ref, o2_ref), (l0_ref, l1_ref, l2_ref)
    for g, (o_ref, l_ref, d) in enumerate(zip(o_refs, l_refs, dils)):
        if d == 1:
            continue
        for r in range(d):
            blk = o_ref[r].astype(F32)
            for c in range(nslab):
                obuf[g, c, pl.ds(r, tm // d, stride=d), :] = blk[:, c * LANES:(c + 1) * LANES]
            lbuf[g, pl.ds(r, tm // d, stride=d), :] = l_ref[r]
    e16 = e16_ref[...]

    for rs in range(nsub):
        rows = slice(rs * (tm // nsub), (rs + 1) * (tm // nsub))
        on, ln = [], []
        for g in range(3):
            if dils[g] == 1:
                on.append(o_refs[g][0, rows, :].astype(F32))
                ln.append(l_refs[g][0, rows, :])
            else:
                on.append(jnp.concatenate([obuf[g, c, rows, :] for c in range(nslab)], axis=1))
                ln.append(lbuf[g, rows, :])
        mx = jnp.maximum(jnp.maximum(ln[0], ln[1]), ln[2])
        ws = [jnp.exp(l - mx) for l in ln]
        wsum = ws[0] + ws[1] + ws[2]
        o_att = jnp.zeros((tm // nsub, ATT_WIDTH), F32)
        for g in range(3):
            hi, mid, _ = _split3(ws[g] / wsum)
            o_att = o_att + _dot(jnp.concatenate([hi, mid], axis=1), e16) * on[g]

        rest = rest_ref[rows, :]
        z_att = rest[:, 0:512].astype(F32)
        q_mem = rest[:, 512:1024]
        z_mem = rest[:, 1024:1536].astype(F32)
        gate_raw = rest[:, 1536:].astype(F32)
        y_att = o_att * _silu_of_half(z_att)

        outs = []
        for h in range(MEM_HEADS):
            lanes = slice(h * MEM_HEAD_DIM, (h + 1) * MEM_HEAD_DIM)
            vh = mkv_ref[:, MEM_WIDTH + h * MEM_HEAD_DIM:MEM_WIDTH + (h + 1) * MEM_HEAD_DIM]
            s = _dot_nt(q_mem[:, lanes], mkv_ref[:, lanes]) * (MEM_HEAD_DIM ** -0.5)
            m = jnp.max(s, axis=-1, keepdims=True)
            e = jnp.exp(s - m)
            den = jnp.sum(e, axis=-1, keepdims=True)
            outs.append(_dot(e.astype(BF16), vh) / den)
        y_mem = jnp.concatenate(outs, axis=1) * _silu_of_half(z_mem)

        p_ssd = _dot(ys_ref[rows, :], wssd_ref[...])
        y_ref[rows, :] = _final_tail(x_ref[rows, :], p_ssd, y_att, y_mem, gate_raw,
                                     watt_ref, wmem_ref, wout_ref, nf_ref)


def _final_prompt(x2d, y_ssd, o_g, l_g, rest, mkv_bf, wssd, watt, wmem, wout, nf, e16, *, seq, tm, dils, nsub):
    t = x2d.shape[0]
    tps = seq // tm

    def res_spec(d, width):
        return pl.BlockSpec((None, d, tm // d, width), lambda i: (i // tps, 0, i % tps, 0))

    full = lambda a: pl.BlockSpec(a.shape, lambda i: (0,) * a.ndim)
    kern = functools.partial(_final_prompt_kernel, tm=tm, dils=dils, nsub=nsub)
    return pl.pallas_call(
        kern,
        grid=(t // tm,),
        in_specs=[
            pl.BlockSpec((tm, D_MODEL), lambda i: (i, 0)),
            pl.BlockSpec((tm, D_MODEL), lambda i: (i, 0)),
            res_spec(dils[0], ATT_WIDTH), res_spec(dils[1], ATT_WIDTH), res_spec(dils[2], ATT_WIDTH),
            res_spec(dils[0], LANES), res_spec(dils[1], LANES), res_spec(dils[2], LANES),
            pl.BlockSpec((tm, rest.shape[1]), lambda i: (i, 0)),
            pl.BlockSpec((None, MEM_LEN, 2 * MEM_WIDTH), lambda i: (i // tps, 0, 0)),
            full(wssd), full(watt), full(wmem), full(wout), full(nf), full(e16),
        ],
        out_specs=pl.BlockSpec((tm, D_MODEL), lambda i: (i, 0)),
        out_shape=jax.ShapeDtypeStruct((t, D_MODEL), F32),
        scratch_shapes=[
            pltpu.VMEM((3, ATT_WIDTH // LANES, tm, LANES), F32),
            pltpu.VMEM((3, tm, LANES), F32),
        ],
        compiler_params=_cparams(("parallel",)),
        name="final_prompt",
    )(x2d, y_ssd, *o_g, *l_g, rest, mkv_bf, wssd, watt, wmem, wout, nf, e16)


def _final_sample_kernel(x_ref, ys_ref, oa_ref, om_ref, rest_ref,
                         wssd_ref, watt_ref, wmem_ref, wout_ref, nf_ref, y_ref):
    p_ssd = _dot(ys_ref[...].astype(BF16), wssd_ref[...])
    rest = rest_ref[...]
    y_att = oa_ref[...] * _silu_of_half(rest[:, 0:512])
    y_mem = om_ref[...] * _silu_of_half(rest[:, 1024:1536])
    y_ref[...] = _final_tail(x_ref[...], p_ssd, y_att, y_mem, rest[:, 1536:], watt_ref, wmem_ref, wout_ref, nf_ref)


def _final_sample(x2d, y_ssd, o_att, o_mem, rest, wssd, watt, wmem, wout, nf, *, tm):
    t = x2d.shape[0]
    full = lambda a: pl.BlockSpec(a.shape, lambda i: (0,) * a.ndim)
    rows = lambda w: pl.BlockSpec((tm, w), lambda i: (i, 0))
    return pl.pallas_call(
        _final_sample_kernel,
        grid=(t // tm,),
        in_specs=[rows(D_MODEL), rows(D_MODEL), rows(ATT_WIDTH), rows(MEM_WIDTH), rows(rest.shape[1]),
                  full(wssd), full(watt), full(wmem), full(wout), full(nf)],
        out_specs=rows(D_MODEL),
        out_shape=jax.ShapeDtypeStruct((t, D_MODEL), F32),
        compiler_params=_cparams(("parallel",)),
        name="final_sample",
    )(x2d, y_ssd, o_att, o_mem, rest, wssd, watt, wmem, wout, nf)


def kernel(x_prompt, x_sample, mem_prompt, cache_win128_kv, cache_win512_kv, cache_win2048_kv, cache_mem_kv,
           state_conv, state_ssm, norm_in_w, w_in, conv_w, conv_b, dt_bias, a_log, d_skip, ssd_norm_w,
           mem_norm_w, w_mem_kv, w_br_ssd, w_br_att, w_br_mem, w_out, norm_f_w):
    bp, seq, _ = x_prompt.shape
    bs, tdec, _ = x_sample.shape
    dils = ATT_DILATIONS

    row = lambda v: v.reshape(1, -1).astype(F32)
    col = jnp.arange(w_in.shape[1])
    rest0 = DT_COL0 + SSD_N_HEADS + (REST_TILE0 - QKV_TILE0) * COL_TILE
    is_gate = (col < D_MODEL) | ((col >= rest0) & ((col < rest0 + ATT_WIDTH) | (col >= rest0 + ATT_WIDTH + MEM_WIDTH)))
    wt = (w_in.T * jnp.where(is_gate, 0.5, 1.0)[:, None]).astype(BF16)
    pad_heads = lambda v: jnp.pad(v.astype(F32), (0, LANES - SSD_N_HEADS)).reshape(1, LANES)
    dtb = pad_heads(dt_bias)
    a_row = jnp.pad(-jnp.exp(a_log.astype(F32)) * LOG2E, (0, LANES - SSD_N_HEADS)).reshape(1, LANES)
    dtb_col, a_col = dtb.reshape(LANES, 1), a_row.reshape(LANES, 1)
    dsk = jnp.repeat(d_skip.astype(F32), SSD_HEAD_DIM).reshape(1, D_MODEL)
    cw = conv_w.astype(F32)
    cb = row(conv_b)
    nin, nssd, nmem, nf = row(norm_in_w), row(ssd_norm_w), row(mem_norm_w), row(norm_f_w)
    wssd, watt, wmem = ((0.5 * w).astype(BF16) for w in (w_br_ssd, w_br_att, w_br_mem))
    wout = w_out.astype(BF16)
    lane = jnp.arange(LANES)
    e_ssd = (lane[:, None] == (jnp.arange(D_MODEL)[None, :] // SSD_HEAD_DIM)).astype(BF16)
    lse_lane_of_ch = jnp.repeat(jnp.array([_lse_lane(h) for h in range(ATT_HEADS)]), ATT_HEAD_DIM)
    e_att = (lane[:, None] == lse_lane_of_ch[None, :]).astype(BF16)
    e_att = jnp.concatenate([e_att, e_att], axis=0)

    xp2 = x_prompt.reshape(bp * seq, D_MODEL)
    mkv_f32, mkv_bf = _mem_kv(mem_prompt.reshape(bp * MEM_LEN, D_MODEL), nmem, w_mem_kv.astype(BF16))
    tm_p = 256
    tabs_p = _rope_tables(jnp.arange(seq))
    windows = tuple(min(w, seq) for w in ATT_WINDOWS)
    zx, q0, q1, q2, rest, dtr, kv0, kv1, kv2 = _in_proj(
        xp2, nin, wt, tabs_p, seq=seq, tm=tm_p, dils=dils, out_dtype=BF16, windows=windows)
    y_ssd, p_ssm = _ssd_prompt(zx, dtr, cw, cb, dtb, a_row, dtb_col, a_col, dsk, nssd, nbatch=bp, seq=seq)
    o_g, l_g = [], []
    for qkv, d in zip((q0, q1, q2), dils):
        o, l = _attn_prompt(qkv, d=d)
        o_g.append(o)
        l_g.append(l)
    y_prompt = _final_prompt(xp2, y_ssd, o_g, l_g, rest, mkv_bf.reshape(bp, MEM_LEN, 2 * MEM_WIDTH),
                             wssd, watt, wmem, wout, nf, e_att, seq=seq, tm=512, dils=dils, nsub=1)
    y_prompt = y_prompt.reshape(bp, seq, D_MODEL)

    p_win = [kv.reshape(bp, 2, ATT_HEADS, ATT_HEAD_DIM, kv.shape[-1]).transpose(0, 4, 1, 2, 3)
             for kv in (kv0, kv1, kv2)]
    p_mem_kv = mkv_f32.reshape(bp, MEM_LEN, 2, MEM_HEADS, MEM_HEAD_DIM)
    p_conv = zx.reshape(bp, seq, -1)[:, seq - (SSD_CONV - 1):, D_MODEL:].astype(F32)
    p_ssm = p_ssm.reshape(bp, SSD_N_HEADS, SSD_HEAD_DIM, SSD_D_STATE)

    ts = bs * tdec
    pos_s = PAST_LEN + jnp.tile(jnp.arange(tdec), bs)
    xs2 = x_sample.reshape(ts, D_MODEL)
    tabs_s = _rope_tables(pos_s)
    zx_s, q0_s, q1_s, q2_s, rest_s, dt_s = _in_proj(
        xs2, nin, wt, tabs_s, seq=ts, tm=min(ts, 128), dils=(1, 1, 1), out_dtype=F32)
    q_s = [a.reshape(ts, 3 * COL_TILE) for a in (q0_s, q1_s, q2_s)]
    hist = jnp.pad(state_conv.astype(F32), ((0, 0), (tdec - (SSD_CONV - 1), 0), (0, 0))).reshape(ts, SSD_CONV_DIM)
    ypart, ea, xdt, bmb, cmb, cd = _ssd_sample_rows(
        zx_s, dt_s, hist, cw, cb, dtb, a_row, dtb_col, a_col, dsk, e_ssd, tdec=tdec)
    cd_seq = cd.reshape(bs, tdec, LANES)[:, 0, :SSD_N_HEADS]
    y_ssd_s, s_ssm = _ssd_sample_state(cd_seq, zx_s, ypart, ea, xdt, bmb, cmb,
                                       state_ssm.reshape(bs, D_MODEL, SSD_D_STATE), nssd, tdec=tdec, gseq=8)
    to_kt = lambda c: c.transpose(0, 2, 3, 4, 1).reshape(bs, 2, ATT_WIDTH, c.shape[1])
    c0, c1, c2 = to_kt(cache_win128_kv), to_kt(cache_win512_kv), to_kt(cache_win2048_kv)
    cmem = cache_mem_kv.reshape(bs, MEM_LEN * 2 * MEM_HEADS, MEM_HEAD_DIM)
    o_att_s, o_mem_s = _attn_sample(*q_s, rest_s, c0, c1, c2, cmem, tdec=tdec)
    y_sample = _final_sample(xs2, y_ssd_s, o_att_s, o_mem_s, rest_s, wssd, watt, wmem, wout, nf, tm=min(ts, 256))
    y_sample = y_sample.reshape(bs, tdec, D_MODEL)

    s_win = [a[:, COL_TILE:].reshape(bs, tdec, 2, ATT_HEADS, ATT_HEAD_DIM) for a in q_s]
    s_conv = zx_s.reshape(bs, tdec, -1)[:, tdec - (SSD_CONV - 1):, D_MODEL:]
    s_ssm = s_ssm.reshape(bs, SSD_N_HEADS, SSD_HEAD_DIM, SSD_D_STATE)

    return (y_prompt, y_sample, p_win[0], p_win[1], p_win[2], p_mem_kv, p_conv, p_ssm,
            s_win[0], s_win[1], s_win[2], s_conv, s_ssm)
```

```python
import functools
import math

import jax
import jax.numpy as jnp
from jax import lax
from jax.experimental import pallas as pl
from jax.experimental.pallas import tpu as pltpu

F32 = jnp.float32
BF16 = jnp.bfloat16

D_MODEL = 1024
NORM_EPS = 1e-6
SSD_HEAD_DIM = 64
SSD_N_HEADS = 16
SSD_N_GROUPS = 4
SSD_D_STATE = 128
SSD_CONV = 4
SSD_CHUNK = 128
SSD_CONV_DIM = 2048
ATT_WINDOWS = (128, 512, 2048)
ATT_DILATIONS = (1, 4, 16)
ATT_HEADS = 8
ATT_HEAD_DIM = 64
ATT_WIDTH = 512
ROPE_DIM = 16
ROPE_THETA = 500000.0
MEM_LEN = 256
MEM_HEADS = 4
MEM_HEAD_DIM = 128
MEM_WIDTH = 512
NEG_INF = -1e30
PAST_LEN = 8192

LANES = 128
COL_TILE = 512
N_COL_TILES = 24
DT_COL0 = 3072
QKV_TILE0 = 6
REST_TILE0 = 15
VMEM_LIMIT = 56 * 1024 * 1024


def _cparams(sem):
    return pltpu.CompilerParams(dimension_semantics=sem, vmem_limit_bytes=VMEM_LIMIT)


def _dot(a, b):
    return jnp.dot(a, b, preferred_element_type=F32)


def _dot_nt(a, b):
    return lax.dot_general(a, b, (((1,), (1,)), ((), ())), preferred_element_type=F32)


def _rms(x, w):
    return x * lax.rsqrt(jnp.mean(x * x, axis=-1, keepdims=True) + NORM_EPS) * w


def _silu(x):
    u = 0.5 * x
    return u * jnp.tanh(u) + u


def _silu_of_half(u):
    return u * jnp.tanh(u) + u


LOG2E = 1.4426950408889634
LN2 = 0.6931471805599453


def _softplus(x):
    return jnp.maximum(x, 0.0) + jnp.log(1.0 + jnp.exp(-jnp.abs(x)))


def _split3(x):
    hi = x.astype(BF16)
    r1 = x - hi.astype(F32)
    mid = r1.astype(BF16)
    lo = (r1 - mid.astype(F32)).astype(BF16)
    return hi, mid, lo


def _dot_exact_lhs01(m01, x):
    hi, mid, lo = _split3(x)
    return _dot(m01, hi) + _dot(m01, mid) + _dot(m01, lo)


def _dot_exact_rhs01(x, m01):
    hi, mid, lo = _split3(x)
    return _dot(hi, m01) + _dot(mid, m01) + _dot(lo, m01)


def _mem_kv_kernel(x_ref, nw_ref, w_ref, o_ref, ob_ref):
    h = _rms(x_ref[...], nw_ref[...]).astype(BF16)
    acc = _dot(h, w_ref[...])
    o_ref[...] = acc
    ob_ref[...] = acc.astype(BF16)


def _mem_kv(mem2d, nw, w_bf):
    m = mem2d.shape[0]
    tm = 256
    return pl.pallas_call(
        _mem_kv_kernel,
        grid=(m // tm,),
        in_specs=[
            pl.BlockSpec((tm, D_MODEL), lambda i: (i, 0)),
            pl.BlockSpec((1, D_MODEL), lambda i: (0, 0)),
            pl.BlockSpec((D_MODEL, 2 * MEM_WIDTH), lambda i: (0, 0)),
        ],
        out_specs=[
            pl.BlockSpec((tm, 2 * MEM_WIDTH), lambda i: (i, 0)),
            pl.BlockSpec((tm, 2 * MEM_WIDTH), lambda i: (i, 0)),
        ],
        out_shape=[
            jax.ShapeDtypeStruct((m, 2 * MEM_WIDTH), F32),
            jax.ShapeDtypeStruct((m, 2 * MEM_WIDTH), BF16),
        ],
        compiler_params=_cparams(("parallel",)),
        name="mem_kv",
    )(mem2d, nw, w_bf)


def _rope_tables(pos):
    half = ROPE_DIM // 2
    inv = jnp.power(ROPE_THETA, -jnp.arange(half, dtype=F32) * 2.0 / ROPE_DIM)
    ang = pos.astype(F32)[:, None] * inv[None, :]
    cos, sin = jnp.cos(ang), jnp.sin(ang)
    m = jnp.arange(LANES) % ATT_HEAD_DIM
    idx = m % half
    c = jnp.where(m[None, :] < ROPE_DIM, cos[:, idx], 1.0)
    s1 = jnp.where(m[None, :] < half, -sin[:, idx], 0.0)
    s2 = jnp.where((m[None, :] >= half) & (m[None, :] < ROPE_DIM), sin[:, idx], 0.0)
    return c.astype(F32), s1.astype(F32), s2.astype(F32)


def _in_proj_kernel(x_ref, nw_ref, wt_ref, c_ref, s1_ref, s2_ref,
                    zx_ref, q0_ref, q1_ref, q2_ref, rest_ref, dt_ref, *more_refs, tm, dils, win_rows):
    pw_refs = more_refs[:3] if win_rows else ()
    h_ref, acc_ref, tmp_ref = more_refs[-3:]
    nslab = COL_TILE // LANES
    h_ref[...] = _rms(x_ref[...], nw_ref[...]).astype(BF16)
    dt_ref[...] = _dot_nt(h_ref[...], wt_ref[DT_COL0:DT_COL0 + LANES, :])
    q_refs = (q0_ref, q1_ref, q2_ref)
    strided = 0
    for j in range(N_COL_TILES):
        col0 = j * COL_TILE + (SSD_N_HEADS if j >= QKV_TILE0 else 0)
        acc = _dot_nt(h_ref[...], wt_ref[col0:col0 + COL_TILE, :])
        if j < QKV_TILE0:
            zx_ref[:, j * COL_TILE:(j + 1) * COL_TILE] = acc.astype(zx_ref.dtype)
            continue
        if j >= REST_TILE0:
            rest_ref[:, (j - REST_TILE0) * COL_TILE:(j - REST_TILE0 + 1) * COL_TILE] = acc.astype(rest_ref.dtype)
            continue
        g, kind = divmod(j - QKV_TILE0, 3)
        out_ref, d = q_refs[g], dils[g]
        for c in range(nslab):
            a = acc[:, c * LANES:(c + 1) * LANES]
            if kind < 2:
                a = (a * c_ref[...] + pltpu.roll(a, LANES - ROPE_DIM // 2, 1) * s1_ref[...]
                     + pltpu.roll(a, ROPE_DIM // 2, 1) * s2_ref[...])
            if kind == 0:
                a = a * (ATT_HEAD_DIM ** -0.5 * LOG2E)
            lanes = slice(kind * COL_TILE + c * LANES, kind * COL_TILE + (c + 1) * LANES)
            if win_rows and kind > 0:
                ch0 = (kind - 1) * COL_TILE + c * LANES
                pw_refs[g][ch0:ch0 + LANES, :] = a[tm - win_rows[g]:, :].T
            if d == 1:
                out_ref[0, :, lanes] = a.astype(out_ref.dtype)
            elif d == 4:
                acc_ref[strided, c] = a
                for r in range(4):
                    out_ref[r, :, lanes] = acc_ref[strided, c, pl.ds(r, tm // 4, stride=4), :].astype(out_ref.dtype)
            else:
                acc_ref[strided, c] = a
                q4 = tm // 4
                for r1 in range(4):
                    tmp_ref[kind, c, r1 * q4:(r1 + 1) * q4, :] = acc_ref[strided, c, pl.ds(r1, q4, stride=4), :]
                for r1 in range(4):
                    for r2 in range(4):
                        out_ref[r1 + 4 * r2, :, lanes] = tmp_ref[
                            kind, c, pl.ds(r1 * q4 + r2, tm // 16, stride=4), :].astype(out_ref.dtype)
        if d > 1:
            strided += 1


def _in_proj(x2d, nw, wt, tabs, *, seq, tm, dils, out_dtype, windows=None):
    t = x2d.shape[0]
    nb = t // seq
    tiles_per_seq = seq // tm
    tab_blocks = tabs[0].shape[0] // tm
    n_strided = 3 * sum(1 for d in dils if d > 1)
    nslab = COL_TILE // LANES

    def qkv_spec(d):
        return pl.BlockSpec((None, d, tm // d, 3 * COL_TILE),
                            lambda i: (i // tiles_per_seq, 0, i % tiles_per_seq, 0))

    win_rows, win_specs, win_shapes = None, [], []
    if windows:
        win_rows = tuple(min(tm, w) for w in windows)
        for w, wr in zip(windows, win_rows):
            first = tiles_per_seq - w // wr
            win_specs.append(pl.BlockSpec(
                (None, 2 * COL_TILE, wr),
                lambda i, first=first: (i // tiles_per_seq, 0, jnp.maximum(i % tiles_per_seq - first, 0))))
            win_shapes.append(jax.ShapeDtypeStruct((nb, 2 * COL_TILE, w), F32))

    tab_spec = pl.BlockSpec((tm, LANES), lambda i: (i % tab_blocks, 0))
    resident = lambda shape: pl.BlockSpec(shape, lambda i: (0, 0), pipeline_mode=pl.Buffered(1))
    kern = functools.partial(_in_proj_kernel, tm=tm, dils=dils, win_rows=win_rows)
    n_zx, n_rest = QKV_TILE0 * COL_TILE, (N_COL_TILES - REST_TILE0) * COL_TILE
    return pl.pallas_call(
        kern,
        grid=(t // tm,),
        in_specs=[
            pl.BlockSpec((tm, D_MODEL), lambda i: (i, 0)),
            resident((1, D_MODEL)),
            resident(wt.shape),
            tab_spec, tab_spec, tab_spec,
        ],
        out_specs=[
            pl.BlockSpec((tm, n_zx), lambda i: (i, 0)),
            qkv_spec(dils[0]), qkv_spec(dils[1]), qkv_spec(dils[2]),
            pl.BlockSpec((tm, n_rest), lambda i: (i, 0)),
            pl.BlockSpec((tm, LANES), lambda i: (i, 0)),
        ] + win_specs,
        out_shape=[
            jax.ShapeDtypeStruct((t, n_zx), out_dtype),
            jax.ShapeDtypeStruct((nb, dils[0], seq // dils[0], 3 * COL_TILE), out_dtype),
            jax.ShapeDtypeStruct((nb, dils[1], seq // dils[1], 3 * COL_TILE), out_dtype),
            jax.ShapeDtypeStruct((nb, dils[2], seq // dils[2], 3 * COL_TILE), out_dtype),
            jax.ShapeDtypeStruct((t, n_rest), out_dtype),
            jax.ShapeDtypeStruct((t, LANES), F32),
        ] + win_shapes,
        scratch_shapes=[
            pltpu.VMEM((tm, D_MODEL), BF16),
            pltpu.VMEM((max(n_strided, 1), nslab, tm, LANES), F32),
            pltpu.VMEM((3, nslab, tm, LANES), F32),
        ],
        compiler_params=_cparams(("arbitrary",)),
        name="in_proj",
    )(x2d, nw, wt, *tabs)


def _ssd_small(dt_raw, dtb_row, a_row, dtb_col, a_col, same01, same01_t, tot01, tot01_t):
    dt = _softplus(dt_raw + dtb_row)
    dta = dt * a_row
    acs = _dot_exact_lhs01(same01, dta)
    tot = _dot_exact_lhs01(tot01, dta)
    dtd = dt * jnp.exp2(tot - acs)
    dt_raw_t = dt_raw.T
    dt_t = _softplus(dt_raw_t + dtb_col)
    dta_t = dt_t * a_col
    acs_t = _dot_exact_rhs01(dta_t, same01_t)
    tot_t = _dot_exact_rhs01(dta_t, tot01_t)
    dtd_t = dt_t * jnp.exp2(tot_t - acs_t)
    return dt, acs, tot, dtd, dt_t, acs_t, dtd_t


def _ssd_pairs(x, bm, cm, acs, acs_t, dt_t, dtd_t, mask, st_ref, y_ref_write, st_scale_row, after_pair=None):
    lane = lax.broadcasted_iota(jnp.int32, (1, LANES), 1)
    half = [lane < ATT_HEAD_DIM, lane >= ATT_HEAD_DIM]
    bm_t = [bm[:, g * SSD_D_STATE:(g + 1) * SSD_D_STATE].T for g in range(SSD_N_GROUPS)]
    cms = [cm[:, g * SSD_D_STATE:(g + 1) * SSD_D_STATE] for g in range(SSD_N_GROUPS)]
    cb = [_dot(cms[g].astype(BF16), bm_t[g].astype(BF16)) for g in range(SSD_N_GROUPS)]
    for p in range(SSD_N_HEADS // 2):
        g = p // 2
        xp = x[:, p * LANES:(p + 1) * LANES]
        if st_ref is not None:
            stp = st_ref[:, p * LANES:(p + 1) * LANES]
        y_pair = jnp.zeros((SSD_CHUNK, LANES), F32)
        st_add = jnp.zeros((SSD_D_STATE, LANES), F32)
        for hh in range(2):
            h = 2 * p + hh
            col = acs[:, h:h + 1]
            row = acs_t[h:h + 1, :]
            lmat = jnp.exp2(jnp.where(mask, col - row, NEG_INF))
            m = cb[g] * lmat * dt_t[h:h + 1, :]
            xm = jnp.where(half[hh], xp, 0.0).astype(BF16)
            if st_ref is not None:
                cx = cms[g] * jnp.exp2(col)
                lhs = jnp.concatenate([m, cx], axis=1).astype(BF16)
                rhs = jnp.concatenate([xm, jnp.where(half[hh], stp, 0.0).astype(BF16)], axis=0)
                y_pair = y_pair + _dot(lhs, rhs)
                bt_h = (bm_t[g] * dtd_t[h:h + 1, :]).astype(BF16)
                st_add = st_add + _dot(bt_h, xm)
            else:
                y_pair = y_pair + _dot(m.astype(BF16), xm)
        y_ref_write(p, y_pair)
        if st_ref is not None:
            scale = jnp.where(half[0], st_scale_row[:, 2 * p:2 * p + 1], st_scale_row[:, 2 * p + 1:2 * p + 2])
            st_ref[:, p * LANES:(p + 1) * LANES] = stp * scale + st_add
        if after_pair is not None:
            after_pair(p)


def _ssd_prompt_kernel(z_ref, xs_ref, bc_ref, xsn_ref, bcn_ref, dt_ref, cw_ref, cb_ref, dtb_ref, a_ref, dtbc_ref,
                       ac_ref, dsk_ref, nw_ref, sh_ref, y_ref, ssm_ref, xc_ref, st_ref, ybuf, *, ncs):
    c = pl.program_id(1)
    q = SSD_CHUNK

    def conv_act(prev, cur, lanes):
        shifted = _dot(sh_ref[...], jnp.concatenate([prev, cur], axis=0))
        acc = cb_ref[:, lanes] + cw_ref[SSD_CONV - 1:SSD_CONV, lanes] * cur.astype(F32)
        for s in range(1, SSD_CONV):
            acc = acc + cw_ref[SSD_CONV - 1 - s:SSD_CONV - s, lanes] * shifted[(s - 1) * q:s * q]
        return _silu(acc)

    @pl.when(c == 0)
    def _():
        st_ref[...] = jnp.zeros(st_ref.shape, F32)
        no_history = jnp.zeros((q, D_MODEL), xs_ref.dtype)
        xc_ref[0, :, 0:D_MODEL] = conv_act(no_history, xs_ref[0:q, :], slice(0, D_MODEL))
        xc_ref[0, :, D_MODEL:SSD_CONV_DIM] = conv_act(no_history, bc_ref[0:q, :], slice(D_MODEL, SSD_CONV_DIM))

    slab = SSD_CONV_DIM // (SSD_N_HEADS // 2)
    ri = lax.broadcasted_iota(jnp.int32, (q, q), 0)
    ci = lax.broadcasted_iota(jnp.int32, (q, q), 1)
    tril = ri >= ci
    tril01 = jnp.where(tril, 1.0, 0.0).astype(BF16)
    triu01 = jnp.where(ri <= ci, 1.0, 0.0).astype(BF16)
    ones01 = jnp.ones((q, q), BF16)

    def write_y(p, y_pair):
        ybuf[:, p * LANES:(p + 1) * LANES] = y_pair

    for i in range(ncs):
        rows = slice(i * q, (i + 1) * q)

        def conv_next(p, i=i, rows=rows):
            lanes = slice(p * slab, (p + 1) * slab)
            src_ref, nxt_ref, off = (xs_ref, xsn_ref, 0) if lanes.start < D_MODEL else (bc_ref, bcn_ref, D_MODEL)
            src = slice(lanes.start - off, lanes.stop - off)
            cur = src_ref[(i + 1) * q:(i + 2) * q, src] if i + 1 < ncs else nxt_ref[:, src]
            xc_ref[(i + 1) % ncs, :, lanes] = conv_act(src_ref[rows, src], cur, lanes)

        x = xc_ref[i, :, 0:D_MODEL]
        bm = xc_ref[i, :, D_MODEL:D_MODEL + 512]
        cm = xc_ref[i, :, D_MODEL + 512:SSD_CONV_DIM]
        dt, acs, tot, dtd, dt_t, acs_t, dtd_t = _ssd_small(
            dt_ref[rows, :], dtb_ref[...], a_ref[...], dtbc_ref[...], ac_ref[...], tril01, triu01, ones01, ones01)
        decay_row = jnp.exp2(tot[0:1, :])
        _ssd_pairs(x, bm, cm, acs, acs_t, dt_t, dtd_t, tril, st_ref, write_y, decay_row, after_pair=conv_next)
        y = (ybuf[...] + dsk_ref[...] * x) * _silu_of_half(z_ref[rows, :].astype(F32))
        y_ref[rows, :] = _rms(y, nw_ref[...]).astype(y_ref.dtype)

    @pl.when(c == pl.num_programs(1) - 1)
    def _():
        for p in range(D_MODEL // LANES):
            ssm_ref[p * LANES:(p + 1) * LANES, :] = st_ref[:, p * LANES:(p + 1) * LANES].T


def _ssd_prompt(zx, dt, cw, cb, dtb, a_row, dtb_col, a_col, dsk, nw, *, nbatch, seq):
    q = SSD_CHUNK
    ncs = next(n for n in (8, 4, 2, 1) if (seq // q) % n == 0)
    rows = ncs * q
    nc = seq // q
    ns = nc // ncs
    row = lambda b, c: b * ns + c
    nxt = lambda b, c: b * nc + jnp.minimum((c + 1) * ncs, nc - 1)
    vec = lambda n: pl.BlockSpec((1, n), lambda b, c: (0, 0))
    col = pl.BlockSpec((LANES, 1), lambda b, c: (0, 0))
    nsh = SSD_CONV - 1
    out_t = jnp.arange(nsh * q) % q
    shift = 1 + jnp.arange(nsh * q) // q
    shift01 = (jnp.arange(2 * q)[None, :] == (q + out_t - shift)[:, None]).astype(BF16)
    return pl.pallas_call(
        functools.partial(_ssd_prompt_kernel, ncs=ncs),
        grid=(nbatch, ns),
        in_specs=[
            pl.BlockSpec((rows, D_MODEL), lambda b, c: (row(b, c), 0)),
            pl.BlockSpec((rows, D_MODEL), lambda b, c: (row(b, c), 1)),
            pl.BlockSpec((rows, D_MODEL), lambda b, c: (row(b, c), 2)),
            pl.BlockSpec((q, D_MODEL), lambda b, c: (nxt(b, c), 1)),
            pl.BlockSpec((q, D_MODEL), lambda b, c: (nxt(b, c), 2)),
            pl.BlockSpec((rows, LANES), lambda b, c: (row(b, c), 0)),
            pl.BlockSpec((SSD_CONV, SSD_CONV_DIM), lambda b, c: (0, 0)),
            vec(SSD_CONV_DIM), vec(LANES), vec(LANES), col, col, vec(D_MODEL), vec(D_MODEL),
            pl.BlockSpec(((SSD_CONV - 1) * q, 2 * q), lambda b, c: (0, 0)),
        ],
        out_specs=[
            pl.BlockSpec((rows, D_MODEL), lambda b, c: (row(b, c), 0)),
            pl.BlockSpec((None, D_MODEL, SSD_D_STATE), lambda b, c: (b, 0, 0)),
        ],
        out_shape=[
            jax.ShapeDtypeStruct((nbatch * seq, D_MODEL), BF16),
            jax.ShapeDtypeStruct((nbatch, D_MODEL, SSD_D_STATE), F32),
        ],
        scratch_shapes=[
            pltpu.VMEM((max(ncs, 2), q, SSD_CONV_DIM), F32),
            pltpu.VMEM((SSD_D_STATE, D_MODEL), F32),
            pltpu.VMEM((q, D_MODEL), F32),
        ],
        compiler_params=_cparams(("parallel", "arbitrary")),
        name="ssd_prompt",
    )(zx, zx, zx, zx, zx, dt, cw, cb, dtb, a_row, dtb_col, a_col, dsk, nw, shift01)


def _expand_heads(v, e01):
    return _dot_exact_rhs01(v, e01)


def _ssd_sample_rows_kernel(xs_ref, bc_ref, dt_ref, hist_ref, cw_ref, cb_ref, dtb_ref, a_ref, dtbc_ref, ac_ref,
                            dsk_ref, e01_ref,
                            ypart_ref, ea_ref, xdt_ref, bm_ref, cm_ref, cd_ref, ybuf, *, tdec):
    q = SSD_CHUNK
    x_raw = jnp.concatenate([xs_ref[...], bc_ref[...]], axis=1)
    hist = hist_ref[...]
    rowi = lax.broadcasted_iota(jnp.int32, (q, 1), 0)
    tpos = rowi % tdec
    acc = cb_ref[...] + cw_ref[SSD_CONV - 1:SSD_CONV, :] * x_raw
    for s in range(1, SSD_CONV):
        prev = jnp.where(tpos >= s, pltpu.roll(x_raw, s, 0), pltpu.roll(hist, q - tdec + s, 0))
        acc = acc + cw_ref[SSD_CONV - 1 - s:SSD_CONV - s, :] * prev
    xc = _silu(acc)
    x = xc[:, :D_MODEL]
    bm = xc[:, D_MODEL:D_MODEL + 512]
    cm = xc[:, D_MODEL + 512:]

    ri = lax.broadcasted_iota(jnp.int32, (q, q), 0)
    ci = lax.broadcasted_iota(jnp.int32, (q, q), 1)
    same = (ri // tdec) == (ci // tdec)
    mask = same & (ri >= ci)
    low01 = jnp.where(mask, 1.0, 0.0).astype(BF16)
    up01 = jnp.where(same & (ri <= ci), 1.0, 0.0).astype(BF16)
    same01 = jnp.where(same, 1.0, 0.0).astype(BF16)
    dt, acs, tot, dtd, dt_t, acs_t, dtd_t = _ssd_small(
        dt_ref[...], dtb_ref[...], a_ref[...], dtbc_ref[...], ac_ref[...], low01, up01, same01, same01)

    def write_y(p, y_pair):
        ybuf[:, p * LANES:(p + 1) * LANES] = y_pair

    _ssd_pairs(x, bm, cm, acs, acs_t, dt_t, dtd_t, mask, None, write_y, None)

    e01 = e01_ref[...]
    ypart_ref[...] = ybuf[...] + dsk_ref[...] * x
    ea_ref[...] = _expand_heads(jnp.exp2(acs), e01)
    xd = x * _expand_heads(dtd, e01)
    for p in range(D_MODEL // LANES):
        xdt_ref[p * LANES:(p + 1) * LANES, :] = xd[:, p * LANES:(p + 1) * LANES].T.astype(BF16)
    bm_ref[...] = bm.astype(BF16)
    cm_ref[...] = cm.astype(BF16)
    cd_ref[...] = jnp.exp2(tot)


def _ssd_sample_rows(zx, dt, hist, cw, cb, dtb, a_row, dtb_col, a_col, dsk, e01, *, tdec):
    t = zx.shape[0]
    q = SSD_CHUNK
    vec = lambda n: pl.BlockSpec((1, n), lambda i: (0, 0))
    col = pl.BlockSpec((LANES, 1), lambda i: (0, 0))
    kern = functools.partial(_ssd_sample_rows_kernel, tdec=tdec)
    return pl.pallas_call(
        kern,
        grid=(t // q,),
        in_specs=[
            pl.BlockSpec((q, D_MODEL), lambda i: (i, 1)),
            pl.BlockSpec((q, D_MODEL), lambda i: (i, 2)),
            pl.BlockSpec((q, LANES), lambda i: (i, 0)),
            pl.BlockSpec((q, SSD_CONV_DIM), lambda i: (i, 0)),
            pl.BlockSpec((SSD_CONV, SSD_CONV_DIM), lambda i: (0, 0)),
            vec(SSD_CONV_DIM), vec(LANES), vec(LANES), col, col, vec(D_MODEL),
            pl.BlockSpec((LANES, D_MODEL), lambda i: (0, 0)),
        ],
        out_specs=[
            pl.BlockSpec((q, D_MODEL), lambda i: (i, 0)),
            pl.BlockSpec((q, D_MODEL), lambda i: (i, 0)),
            pl.BlockSpec((D_MODEL, q), lambda i: (0, i)),
            pl.BlockSpec((q, 512), lambda i: (i, 0)),
            pl.BlockSpec((q, 512), lambda i: (i, 0)),
            pl.BlockSpec((q, LANES), lambda i: (i, 0)),
        ],
        out_shape=[
            jax.ShapeDtypeStruct((t, D_MODEL), F32),
            jax.ShapeDtypeStruct((t, D_MODEL), F32),
            jax.ShapeDtypeStruct((D_MODEL, t), BF16),
            jax.ShapeDtypeStruct((t, 512), BF16),
            jax.ShapeDtypeStruct((t, 512), BF16),
            jax.ShapeDtypeStruct((t, LANES), F32),
        ],
        scratch_shapes=[pltpu.VMEM((q, D_MODEL), F32)],
        compiler_params=_cparams(("parallel",)),
        name="ssd_sample_rows",
    )(zx, zx, dt, hist, cw, cb, dtb, a_row, dtb_col, a_col, dsk, e01)


def _ssd_sample_state_kernel(cd_ref, z_ref, ypart_ref, ea_ref, xdt_ref, bm_ref, cm_ref, st_ref, nw_ref,
                             y_ref, sto_ref, *, tdec, gseq):
    i = pl.program_id(0)
    q = SSD_CHUNK
    rows = gseq * tdec
    steps_per_blk = q // rows
    lane = lax.broadcasted_iota(jnp.int32, (1, q), 1)
    rg = 16
    rowi = lax.broadcasted_iota(jnp.int32, (rg, 1), 0)
    lane_base = (i % steps_per_blk) * rows
    seq_per_rg = rg // tdec
    yoff_groups = []
    for q8 in range(rows // rg):
        yoff8 = jnp.zeros((rg, D_MODEL), F32)
        for sj in range(seq_per_rg):
            jj = q8 * seq_per_rg + sj
            b = i * gseq + jj
            lmask = (lane >= lane_base + jj * tdec) & (lane < lane_base + (jj + 1) * tdec)
            parts = []
            for g in range(SSD_N_GROUPS):
                s_g = st_ref[jj, g * 256:(g + 1) * 256, :]
                c8 = cm_ref[q8 * rg:(q8 + 1) * rg, g * SSD_D_STATE:(g + 1) * SSD_D_STATE]
                parts.append(_dot_nt(c8, s_g.astype(BF16)))
                lhs = jnp.where(lmask, xdt_ref[g * 256:(g + 1) * 256, :], jnp.zeros((), BF16))
                add = _dot(lhs, bm_ref[:, g * SSD_D_STATE:(g + 1) * SSD_D_STATE])
                for hh in range(4):
                    h = 4 * g + hh
                    sl = slice(hh * SSD_HEAD_DIM, (hh + 1) * SSD_HEAD_DIM)
                    sto_ref[jj, g * 256 + hh * 64:g * 256 + (hh + 1) * 64, :] = (
                        s_g[sl, :] * cd_ref[b, h] + add[sl, :])
            yo = jnp.concatenate(parts, axis=1)
            in_seq = (rowi >= sj * tdec) & (rowi < (sj + 1) * tdec)
            yoff8 = jnp.where(in_seq, yo, yoff8)
        yoff_groups.append(yoff8)
    yoff = jnp.concatenate(yoff_groups, axis=0) if len(yoff_groups) > 1 else yoff_groups[0]
    y = (ypart_ref[...] + yoff * ea_ref[...]) * _silu_of_half(z_ref[...])
    y_ref[...] = _rms(y, nw_ref[...])


def _ssd_sample_state(cd, zx, ypart, ea, xdt, bmb, cmb, state, nw, *, tdec, gseq):
    t = zx.shape[0]
    nseq = t // tdec
    q = SSD_CHUNK
    rows = gseq * tdec
    spb = q // rows
    kern = functools.partial(_ssd_sample_state_kernel, tdec=tdec, gseq=gseq)
    return pl.pallas_call(
        kern,
        grid=(nseq // gseq,),
        in_specs=[
            pl.BlockSpec(memory_space=pltpu.SMEM),
            pl.BlockSpec((rows, D_MODEL), lambda i: (i, 0)),
            pl.BlockSpec((rows, D_MODEL), lambda i: (i, 0)),
            pl.BlockSpec((rows, D_MODEL), lambda i: (i, 0)),
            pl.BlockSpec((D_MODEL, q), lambda i: (0, i // spb)),
            pl.BlockSpec((q, 512), lambda i: (i // spb, 0)),
            pl.BlockSpec((rows, 512), lambda i: (i, 0)),
            pl.BlockSpec((gseq, D_MODEL, SSD_D_STATE), lambda i: (i, 0, 0)),
            pl.BlockSpec((1, D_MODEL), lambda i: (0, 0)),
        ],
        out_specs=[
            pl.BlockSpec((rows, D_MODEL), lambda i: (i, 0)),
            pl.BlockSpec((gseq, D_MODEL, SSD_D_STATE), lambda i: (i, 0, 0)),
        ],
        out_shape=[
            jax.ShapeDtypeStruct((t, D_MODEL), F32),
            jax.ShapeDtypeStruct((nseq, D_MODEL, SSD_D_STATE), F32),
        ],
        compiler_params=_cparams(("parallel",)),
        name="ssd_sample_state",
    )(cd, zx, ypart, ea, xdt, bmb, cmb, state, nw)


def _lse_lane(h):
    return 16 * h


def _attn_prompt_kernel(q_ref, kc_ref, kp_ref, vc_ref, vp_ref, o_ref, lse_ref, *, nr, nq):
    n = pl.program_id(2)
    nk = 128
    ks = [jnp.concatenate([kp_ref[rr], kc_ref[rr]], axis=0) for rr in range(nr)]
    vs = [jnp.concatenate([vp_ref[rr], vc_ref[rr]], axis=0) for rr in range(nr)]
    lane = lax.broadcasted_iota(jnp.int32, (1, LANES), 1)
    lo = lane < ATT_HEAD_DIM
    qi = lax.broadcasted_iota(jnp.int32, (2 * nk, 2 * nk), 0) % nk
    kj = lax.broadcasted_iota(jnp.int32, (2 * nk, 2 * nk), 1)
    in_band = (kj >= qi) & (kj <= qi + nk)
    band = jnp.where(in_band, 0.0, NEG_INF)
    band_first = jnp.where(in_band & ((kj >= nk) | (n > 0)), 0.0, NEG_INF)
    zero = jnp.zeros((), q_ref.dtype)
    npairs = ATT_HEADS // 2
    pair_lanes = [slice(p * LANES, (p + 1) * LANES) for p in range(npairs)]
    units = [(rr, j, p) for rr in range(nr) for j in range(nq) for p in range(npairs)]

    def scores(u):
        rr, j, p = u
        qp = q_ref[rr, j * nk:(j + 1) * nk, pair_lanes[p]]
        q2 = jnp.concatenate([jnp.where(lo, qp, zero), jnp.where(lo, zero, qp)], axis=0)
        kk = ks[rr][j * nk:(j + 2) * nk, pair_lanes[p]]
        return _dot_nt(q2, kk) + (band_first if j == 0 else band)

    def softmax(s):
        m = jnp.max(s, axis=-1, keepdims=True)
        e = jnp.exp2(s - m)
        den = jnp.sum(e, axis=-1, keepdims=True)
        return e.astype(BF16), den, m

    ahead = 3
    s_next = [scores(u) for u in units[:ahead]]
    m_full = den_full = None
    o_parts = []
    for i, (rr, j, p) in enumerate(units):
        e, den, m = softmax(s_next.pop(0))
        if i + ahead < len(units):
            s_next.append(scores(units[i + ahead]))
        r = _dot(e, vs[rr][j * nk:(j + 2) * nk, pair_lanes[p]])
        den_pair = jnp.where(lo, den[0:nk], den[nk:2 * nk])
        o_parts.append((jnp.where(lo, r[0:nk], r[nk:2 * nk]) / den_pair).astype(o_ref.dtype))
        if p == 0:
            m_full = jnp.zeros((nk, LANES), F32)
            den_full = jnp.ones((nk, LANES), F32)
        for hh in range(2):
            base = _lse_lane(2 * p + hh)
            grp = (lane >= base) & (lane < base + 16)
            m_full = jnp.where(grp, m[hh * nk:(hh + 1) * nk], m_full)
            den_full = jnp.where(grp, den[hh * nk:(hh + 1) * nk], den_full)
        if p == npairs - 1:
            o_ref[rr, j * nk:(j + 1) * nk, :] = jnp.concatenate(o_parts, axis=1)
            lse_ref[rr, j * nk:(j + 1) * nk, :] = (m_full + jnp.log2(den_full)) * LN2
            o_parts = []


ATT_BLOCKS_PER_STEP = 16


def _attn_prompt(qkv, *, d):
    nb, _, length, _ = qkv.shape
    nk = 128
    nq = next(c for c in (16, 8, 4, 2, 1) if c <= ATT_BLOCKS_PER_STEP and length % (c * nk) == 0)
    nr = next(c for c in (16, 8, 4, 2, 1) if c * nq <= ATT_BLOCKS_PER_STEP and d % c == 0)
    nblk = length // (nq * nk)
    cur = lambda which: pl.BlockSpec((None, nr, nq * nk, COL_TILE), lambda b, r, n: (b, r, n, which))
    prev = lambda which: pl.BlockSpec((None, nr, nk, COL_TILE),
                                      lambda b, r, n: (b, r, jnp.maximum(nq * n - 1, 0), which))
    return pl.pallas_call(
        functools.partial(_attn_prompt_kernel, nr=nr, nq=nq),
        grid=(nb, d // nr, nblk),
        in_specs=[cur(0), cur(1), prev(1), cur(2), prev(2)],
        out_specs=[
            pl.BlockSpec((None, nr, nq * nk, ATT_WIDTH), lambda b, r, n: (b, r, n, 0)),
            pl.BlockSpec((None, nr, nq * nk, LANES), lambda b, r, n: (b, r, n, 0)),
        ],
        out_shape=[
            jax.ShapeDtypeStruct((nb, d, length, ATT_WIDTH), BF16),
            jax.ShapeDtypeStruct((nb, d, length, LANES), F32),
        ],
        compiler_params=_cparams(("parallel", "parallel", "arbitrary")),
        name=f"attn_prompt_d{d}",
    )(qkv, qkv, qkv, qkv, qkv)


def _attn_sample_kernel(q0_ref, q1_ref, q2_ref, qm_ref, c0_ref, c1_ref, c2_ref, cm_ref, oa_ref, om_ref, *, tdec):
    nh = ATT_HEADS
    rows8 = 8
    spg = rows8 // tdec
    r64 = nh * rows8
    rowi = lax.broadcasted_iota(jnp.int32, (r64, 1), 0)
    head_of_row = rowi // rows8
    r8 = rowi % rows8
    seq_q = r8 // tdec
    t_q = r8 % tdec
    lane512 = lax.broadcasted_iota(jnp.int32, (1, ATT_WIDTH), 1)
    headmask = (lane512 // ATT_HEAD_DIM) == head_of_row
    col8 = lax.broadcasted_iota(jnp.int32, (1, rows8), 1)
    seq_k = col8 // tdec
    t_k = col8 % tdec

    res = [[] for _ in range(spg)]
    for g, (q_ref, c_ref, d) in enumerate(zip((q0_ref, q1_ref, q2_ref), (c0_ref, c1_ref, c2_ref), ATT_DILATIONS)):
        w = c_ref.shape[-1]
        qkv = q_ref[...]
        qv = qkv[:, :ATT_WIDTH]
        kn = qkv[:, ATT_WIDTH:2 * ATT_WIDTH]
        vn = qkv[:, 2 * ATT_WIDTH:]
        q64 = jnp.where(headmask, jnp.concatenate([qv] * nh, axis=0), 0.0)
        s_new = _dot_nt(q64, kn)
        if d == 1:
            valid_new = (seq_k == seq_q) & (t_k <= t_q)
        else:
            valid_new = (seq_k == seq_q) & (t_k == t_q)
        s_new = jnp.where(valid_new, s_new, NEG_INF)
        wi = lax.broadcasted_iota(jnp.int32, (1, w), 1)
        valid_c = (wi >= t_q) if d == 1 else ((wi % d) == t_q)
        m_new = jnp.max(s_new, axis=-1, keepdims=True)
        for sj in range(spg):
            s_c = jnp.where(valid_c, _dot(q64.astype(BF16), c_ref[sj, 0].astype(BF16)), NEG_INF)
            m = jnp.maximum(jnp.max(s_c, axis=-1, keepdims=True), m_new)
            e_c = jnp.exp2(s_c - m)
            e_n = jnp.exp2(s_new - m)
            den = jnp.sum(e_c, axis=-1, keepdims=True) + jnp.sum(e_n, axis=-1, keepdims=True)
            o = _dot(e_n, vn) + _dot_nt(e_c.astype(BF16), c_ref[sj, 1].astype(BF16))
            res[sj].append((o / den, (m + jnp.log2(den)) * LN2))

    row8 = lax.broadcasted_iota(jnp.int32, (rows8, 1), 0)
    out_a = jnp.zeros((rows8, ATT_WIDTH), F32)
    for sj in range(spg):
        (o0, l0), (o1, l1), (o2, l2) = res[sj]
        mx = jnp.maximum(jnp.maximum(l0, l1), l2)
        w0, w1, w2 = jnp.exp(l0 - mx), jnp.exp(l1 - mx), jnp.exp(l2 - mx)
        om = (w0 * o0 + w1 * o1 + w2 * o2) / (w0 + w1 + w2)
        om = jnp.where(headmask, om, 0.0)
        o8 = om[0:rows8]
        for h in range(1, nh):
            o8 = o8 + om[h * rows8:(h + 1) * rows8]
        out_a = jnp.where((row8 // tdec) == sj, o8, out_a)
    oa_ref[...] = out_a

    qm = jnp.concatenate([qm_ref[...] * (MEM_HEAD_DIM ** -0.5), jnp.zeros((rows8, MEM_WIDTH), F32)], axis=0)
    out_m = jnp.zeros((rows8, MEM_WIDTH), F32)
    for sj in range(spg):
        mparts = []
        for h in range(MEM_HEADS):
            kh = cm_ref[sj, pl.ds(h, MEM_LEN, stride=2 * MEM_HEADS), :].astype(BF16)
            vh = cm_ref[sj, pl.ds(MEM_HEADS + h, MEM_LEN, stride=2 * MEM_HEADS), :].astype(BF16)
            s = _dot_nt(qm[:, h * MEM_HEAD_DIM:(h + 1) * MEM_HEAD_DIM].astype(BF16), kh)
            mm = jnp.max(s, axis=-1, keepdims=True)
            e = jnp.exp(s - mm)
            den = jnp.sum(e, axis=-1, keepdims=True)
            mparts.append((_dot(e.astype(BF16), vh) / den)[0:rows8])
        out_m = jnp.where((row8 // tdec) == sj, jnp.concatenate(mparts, axis=1), out_m)
    om_ref[...] = out_m


def _attn_sample(q0, q1, q2, rest, c0, c1, c2, cmem, *, tdec):
    t = q0.shape[0]
    spg = 8 // tdec
    row_spec = pl.BlockSpec((8, 3 * COL_TILE), lambda i: (i, 0))
    cache_spec = lambda c: pl.BlockSpec((spg,) + c.shape[1:], lambda i: (i,) + (0,) * (c.ndim - 1))
    kern = functools.partial(_attn_sample_kernel, tdec=tdec)
    return pl.pallas_call(
        kern,
        grid=(t // 8,),
        in_specs=[
            row_spec, row_spec, row_spec,
            pl.BlockSpec((8, COL_TILE), lambda i: (i, 1)),
            cache_spec(c0), cache_spec(c1), cache_spec(c2), cache_spec(cmem),
        ],
        out_specs=[
            pl.BlockSpec((8, ATT_WIDTH), lambda i: (i, 0)),
            pl.BlockSpec((8, MEM_WIDTH), lambda i: (i, 0)),
        ],
        out_shape=[
            jax.ShapeDtypeStruct((t, ATT_WIDTH), F32),
            jax.ShapeDtypeStruct((t, MEM_WIDTH), F32),
        ],
        compiler_params=_cparams(("parallel",)),
        name="attn_sample",
    )(q0, q1, q2, rest, c0, c1, c2, cmem)


def _final_tail(x, p_ssd, y_att, y_mem, gate_raw, watt_ref, wmem_ref, wout_ref, nf_ref):
    t = jnp.tanh(gate_raw)
    p_att = _dot(y_att.astype(BF16), watt_ref[...])
    p_mem = _dot(y_mem.astype(BF16), wmem_ref[...])
    merged = ((t[:, 0:D_MODEL] * p_ssd + p_ssd) + (t[:, D_MODEL:2 * D_MODEL] * p_att + p_att)
              + (t[:, 2 * D_MODEL:] * p_mem + p_mem))
    return _rms(x + _dot(merged.astype(BF16), wout_ref[...]), nf_ref[...])


def _final_prompt_kernel(x_ref, ys_ref, o0_ref, o1_ref, o2_ref, l0_ref, l1_ref, l2_ref, rest_ref, mkv_ref,
                         wssd_ref, watt_ref, wmem_ref, wout_ref, nf_ref, e16_ref,
                         y_ref, obuf, lbuf, *, tm, dils, nsub):
    nslab = ATT_WIDTH // LANES
    o_refs, l_refs = (o0_ref, o1_ref, o2_ref), (l0_ref, l1_ref, l2_ref)
    for g, (o_ref, l_ref, d) in enumerate(zip(o_refs, l_refs, dils)):
        if d == 1:
            continue
        for r in range(d):
            blk = o_ref[r].astype(F32)
            for c in range(nslab):
                obuf[g, c, pl.ds(r, tm // d, stride=d), :] = blk[:, c * LANES:(c + 1) * LANES]
            lbuf[g, pl.ds(r, tm // d, stride=d), :] = l_ref[r]
    e16 = e16_ref[...]

    for rs in range(nsub):
        rows = slice(rs * (tm // nsub), (rs + 1) * (tm // nsub))
        on, ln = [], []
        for g in range(3):
            if dils[g] == 1:
                on.append(o_refs[g][0, rows, :].astype(F32))
                ln.append(l_refs[g][0, rows, :])
            else:
                on.append(jnp.concatenate([obuf[g, c, rows, :] for c in range(nslab)], axis=1))
                ln.append(lbuf[g, rows, :])
        mx = jnp.maximum(jnp.maximum(ln[0], ln[1]), ln[2])
        ws = [jnp.exp(l - mx) for l in ln]
        wsum = ws[0] + ws[1] + ws[2]
        o_att = jnp.zeros((tm // nsub, ATT_WIDTH), F32)
        for g in range(3):
            hi, mid, _ = _split3(ws[g] / wsum)
            o_att = o_att + _dot(jnp.concatenate([hi, mid], axis=1), e16) * on[g]

        rest = rest_ref[rows, :]
        z_att = rest[:, 0:512].astype(F32)
        q_mem = rest[:, 512:1024]
        z_mem = rest[:, 1024:1536].astype(F32)
        gate_raw = rest[:, 1536:].astype(F32)
        y_att = o_att * _silu_of_half(z_att)

        outs = []
        for h in range(MEM_HEADS):
            lanes = slice(h * MEM_HEAD_DIM, (h + 1) * MEM_HEAD_DIM)
            vh = mkv_ref[:, MEM_WIDTH + h * MEM_HEAD_DIM:MEM_WIDTH + (h + 1) * MEM_HEAD_DIM]
            s = _dot_nt(q_mem[:, lanes], mkv_ref[:, lanes]) * (MEM_HEAD_DIM ** -0.5)
            m = jnp.max(s, axis=-1, keepdims=True)
            e = jnp.exp(s - m)
            den = jnp.sum(e, axis=-1, keepdims=True)
            outs.append(_dot(e.astype(BF16), vh) / den)
        y_mem = jnp.concatenate(outs, axis=1) * _silu_of_half(z_mem)

        p_ssd = _dot(ys_ref[rows, :], wssd_ref[...])
        y_ref[rows, :] = _final_tail(x_ref[rows, :], p_ssd, y_att, y_mem, gate_raw,
                                     watt_ref, wmem_ref, wout_ref, nf_ref)


def _final_prompt(x2d, y_ssd, o_g, l_g, rest, mkv_bf, wssd, watt, wmem, wout, nf, e16, *, seq, tm, dils, nsub):
    t = x2d.shape[0]
    tps = seq // tm

    def res_spec(d, width):
        return pl.BlockSpec((None, d, tm // d, width), lambda i: (i // tps, 0, i % tps, 0))

    full = lambda a: pl.BlockSpec(a.shape, lambda i: (0,) * a.ndim)
    kern = functools.partial(_final_prompt_kernel, tm=tm, dils=dils, nsub=nsub)
    return pl.pallas_call(
        kern,
        grid=(t // tm,),
        in_specs=[
            pl.BlockSpec((tm, D_MODEL), lambda i: (i, 0)),
            pl.BlockSpec((tm, D_MODEL), lambda i: (i, 0)),
            res_spec(dils[0], ATT_WIDTH), res_spec(dils[1], ATT_WIDTH), res_spec(dils[2], ATT_WIDTH),
            res_spec(dils[0], LANES), res_spec(dils[1], LANES), res_spec(dils[2], LANES),
            pl.BlockSpec((tm, rest.shape[1]), lambda i: (i, 0)),
            pl.BlockSpec((None, MEM_LEN, 2 * MEM_WIDTH), lambda i: (i // tps, 0, 0)),
            full(wssd), full(watt), full(wmem), full(wout), full(nf), full(e16),
        ],
        out_specs=pl.BlockSpec((tm, D_MODEL), lambda i: (i, 0)),
        out_shape=jax.ShapeDtypeStruct((t, D_MODEL), F32),
        scratch_shapes=[
            pltpu.VMEM((3, ATT_WIDTH // LANES, tm, LANES), F32),
            pltpu.VMEM((3, tm, LANES), F32),
        ],
        compiler_params=_cparams(("parallel",)),
        name="final_prompt",
    )(x2d, y_ssd, *o_g, *l_g, rest, mkv_bf, wssd, watt, wmem, wout, nf, e16)


def _final_sample_kernel(x_ref, ys_ref, oa_ref, om_ref, rest_ref,
                         wssd_ref, watt_ref, wmem_ref, wout_ref, nf_ref, y_ref):
    p_ssd = _dot(ys_ref[...].astype(BF16), wssd_ref[...])
    rest = rest_ref[...]
    y_att = oa_ref[...] * _silu_of_half(rest[:, 0:512])
    y_mem = om_ref[...] * _silu_of_half(rest[:, 1024:1536])
    y_ref[...] = _final_tail(x_ref[...], p_ssd, y_att, y_mem, rest[:, 1536:], watt_ref, wmem_ref, wout_ref, nf_ref)


def _final_sample(x2d, y_ssd, o_att, o_mem, rest, wssd, watt, wmem, wout, nf, *, tm):
    t = x2d.shape[0]
    full = lambda a: pl.BlockSpec(a.shape, lambda i: (0,) * a.ndim)
    rows = lambda w: pl.BlockSpec((tm, w), lambda i: (i, 0))
    return pl.pallas_call(
        _final_sample_kernel,
        grid=(t // tm,),
        in_specs=[rows(D_MODEL), rows(D_MODEL), rows(ATT_WIDTH), rows(MEM_WIDTH), rows(rest.shape[1]),
                  full(wssd), full(watt), full(wmem), full(wout), full(nf)],
        out_specs=rows(D_MODEL),
        out_shape=jax.ShapeDtypeStruct((t, D_MODEL), F32),
        compiler_params=_cparams(("parallel",)),
        name="final_sample",
    )(x2d, y_ssd, o_att, o_mem, rest, wssd, watt, wmem, wout, nf)


def kernel(x_prompt, x_sample, mem_prompt, cache_win128_kv, cache_win512_kv, cache_win2048_kv, cache_mem_kv,
           state_conv, state_ssm, norm_in_w, w_in, conv_w, conv_b, dt_bias, a_log, d_skip, ssd_norm_w,
           mem_norm_w, w_mem_kv, w_br_ssd, w_br_att, w_br_mem, w_out, norm_f_w):
    bp, seq, _ = x_prompt.shape
    bs, tdec, _ = x_sample.shape
    dils = ATT_DILATIONS

    row = lambda v: v.reshape(1, -1).astype(F32)
    col = jnp.arange(w_in.shape[1])
    rest0 = DT_COL0 + SSD_N_HEADS + (REST_TILE0 - QKV_TILE0) * COL_TILE
    is_gate = (col < D_MODEL) | ((col >= rest0) & ((col < rest0 + ATT_WIDTH) | (col >= rest0 + ATT_WIDTH + MEM_WIDTH)))
    wt = (w_in.T * jnp.where(is_gate, 0.5, 1.0)[:, None]).astype(BF16)
    pad_heads = lambda v: jnp.pad(v.astype(F32), (0, LANES - SSD_N_HEADS)).reshape(1, LANES)
    dtb = pad_heads(dt_bias)
    a_row = jnp.pad(-jnp.exp(a_log.astype(F32)) * LOG2E, (0, LANES - SSD_N_HEADS)).reshape(1, LANES)
    dtb_col, a_col = dtb.reshape(LANES, 1), a_row.reshape(LANES, 1)
    dsk = jnp.repeat(d_skip.astype(F32), SSD_HEAD_DIM).reshape(1, D_MODEL)
    cw = conv_w.astype(F32)
    cb = row(conv_b)
    nin, nssd, nmem, nf = row(norm_in_w), row(ssd_norm_w), row(mem_norm_w), row(norm_f_w)
    wssd, watt, wmem = ((0.5 * w).astype(BF16) for w in (w_br_ssd, w_br_att, w_br_mem))
    wout = w_out.astype(BF16)
    lane = jnp.arange(LANES)
    e_ssd = (lane[:, None] == (jnp.arange(D_MODEL)[None, :] // SSD_HEAD_DIM)).astype(BF16)
    lse_lane_of_ch = jnp.repeat(jnp.array([_lse_lane(h) for h in range(ATT_HEADS)]), ATT_HEAD_DIM)
    e_att = (lane[:, None] == lse_lane_of_ch[None, :]).astype(BF16)
    e_att = jnp.concatenate([e_att, e_att], axis=0)

    xp2 = x_prompt.reshape(bp * seq, D_MODEL)
    mkv_f32, mkv_bf = _mem_kv(mem_prompt.reshape(bp * MEM_LEN, D_MODEL), nmem, w_mem_kv.astype(BF16))
    tm_p = 256
    tabs_p = _rope_tables(jnp.arange(seq))
    windows = tuple(min(w, seq) for w in ATT_WINDOWS)
    zx, q0, q1, q2, rest, dtr, kv0, kv1, kv2 = _in_proj(
        xp2, nin, wt, tabs_p, seq=seq, tm=tm_p, dils=dils, out_dtype=BF16, windows=windows)
    y_ssd, p_ssm = _ssd_prompt(zx, dtr, cw, cb, dtb, a_row, dtb_col, a_col, dsk, nssd, nbatch=bp, seq=seq)
    o_g, l_g = [], []
    for qkv, d in zip((q0, q1, q2), dils):
        o, l = _attn_prompt(qkv, d=d)
        o_g.append(o)
        l_g.append(l)
    y_prompt = _final_prompt(xp2, y_ssd, o_g, l_g, rest, mkv_bf.reshape(bp, MEM_LEN, 2 * MEM_WIDTH),
                             wssd, watt, wmem, wout, nf, e_att, seq=seq, tm=512, dils=dils, nsub=1)
    y_prompt = y_prompt.reshape(bp, seq, D_MODEL)

    p_win = [kv.reshape(bp, 2, ATT_HEADS, ATT_HEAD_DIM, kv.shape[-1]).transpose(0, 4, 1, 2, 3)
             for kv in (kv0, kv1, kv2)]
    p_mem_kv = mkv_f32.reshape(bp, MEM_LEN, 2, MEM_HEADS, MEM_HEAD_DIM)
    p_conv = zx.reshape(bp, seq, -1)[:, seq - (SSD_CONV - 1):, D_MODEL:].astype(F32)
    p_ssm = p_ssm.reshape(bp, SSD_N_HEADS, SSD_HEAD_DIM, SSD_D_STATE)

    ts = bs * tdec
    pos_s = PAST_LEN + jnp.tile(jnp.arange(tdec), bs)
    xs2 = x_sample.reshape(ts, D_MODEL)
    tabs_s = _rope_tables(pos_s)
    zx_s, q0_s, q1_s, q2_s, rest_s, dt_s = _in_proj(
        xs2, nin, wt, tabs_s, seq=ts, tm=min(ts, 128), dils=(1, 1, 1), out_dtype=F32)
    q_s = [a.reshape(ts, 3 * COL_TILE) for a in (q0_s, q1_s, q2_s)]
    hist = jnp.pad(state_conv.astype(F32), ((0, 0), (tdec - (SSD_CONV - 1), 0), (0, 0))).reshape(ts, SSD_CONV_DIM)
    ypart, ea, xdt, bmb, cmb, cd = _ssd_sample_rows(
        zx_s, dt_s, hist, cw, cb, dtb, a_row, dtb_col, a_col, dsk, e_ssd, tdec=tdec)
    cd_seq = cd.reshape(bs, tdec, LANES)[:, 0, :SSD_N_HEADS]
    y_ssd_s, s_ssm = _ssd_sample_state(cd_seq, zx_s, ypart, ea, xdt, bmb, cmb,
                                       state_ssm.reshape(bs, D_MODEL, SSD_D_STATE), nssd, tdec=tdec, gseq=8)
    to_kt = lambda c: c.transpose(0, 2, 3, 4, 1).reshape(bs, 2, ATT_WIDTH, c.shape[1])
    c0, c1, c2 = to_kt(cache_win128_kv), to_kt(cache_win512_kv), to_kt(cache_win2048_kv)
    cmem = cache_mem_kv.reshape(bs, MEM_LEN * 2 * MEM_HEADS, MEM_HEAD_DIM)
    o_att_s, o_mem_s = _attn_sample(*q_s, rest_s, c0, c1, c2, cmem, tdec=tdec)
    y_sample = _final_sample(xs2, y_ssd_s, o_att_s, o_mem_s, rest_s, wssd, watt, wmem, wout, nf, tm=min(ts, 256))
    y_sample = y_sample.reshape(bs, tdec, D_MODEL)

    s_win = [a[:, COL_TILE:].reshape(bs, tdec, 2, ATT_HEADS, ATT_HEAD_DIM) for a in q_s]
    s_conv = zx_s.reshape(bs, tdec, -1)[:, tdec - (SSD_CONV - 1):, D_MODEL:]
    s_ssm = s_ssm.reshape(bs, SSD_N_HEADS, SSD_HEAD_DIM, SSD_D_STATE)

    return (y_prompt, y_sample, p_win[0], p_win[1], p_win[2], p_mem_kv, p_conv, p_ssm,
            s_win[0], s_win[1], s_win[2], s_conv, s_ssm)
```

```python
import functools
import math

import jax
import jax.numpy as jnp
from jax import lax
from jax.experimental import pallas as pl
from jax.experimental.pallas import tpu as pltpu

F32 = jnp.float32
BF16 = jnp.bfloat16

D_MODEL = 1024
NORM_EPS = 1e-6
SSD_HEAD_DIM = 64
SSD_N_HEADS = 16
SSD_N_GROUPS = 4
SSD_D_STATE = 128
SSD_CONV = 4
SSD_CHUNK = 128
SSD_CONV_DIM = 2048
ATT_WINDOWS = (128, 512, 2048)
ATT_DILATIONS = (1, 4, 16)
ATT_HEADS = 8
ATT_HEAD_DIM = 64
ATT_WIDTH = 512
ROPE_DIM = 16
ROPE_THETA = 500000.0
MEM_LEN = 256
MEM_HEADS = 4
MEM_HEAD_DIM = 128
MEM_WIDTH = 512
NEG_INF = -1e30
PAST_LEN = 8192

LANES = 128
COL_TILE = 512
N_COL_TILES = 24
DT_COL0 = 3072
QKV_TILE0 = 6
REST_TILE0 = 15
VMEM_LIMIT = 56 * 1024 * 1024


def _cparams(sem):
    return pltpu.CompilerParams(dimension_semantics=sem, vmem_limit_bytes=VMEM_LIMIT)


def _dot(a, b):
    return jnp.dot(a, b, preferred_element_type=F32)


def _dot_nt(a, b):
    return lax.dot_general(a, b, (((1,), (1,)), ((), ())), preferred_element_type=F32)


def _rms(x, w):
    return x * lax.rsqrt(jnp.mean(x * x, axis=-1, keepdims=True) + NORM_EPS) * w


def _silu(x):
    u = 0.5 * x
    return u * jnp.tanh(u) + u


def _silu_of_half(u):
    return u * jnp.tanh(u) + u


LOG2E = 1.4426950408889634
LN2 = 0.6931471805599453


def _softplus(x):
    return jnp.maximum(x, 0.0) + jnp.log(1.0 + jnp.exp(-jnp.abs(x)))


def _split3(x):
    hi = x.astype(BF16)
    r1 = x - hi.astype(F32)
    mid = r1.astype(BF16)
    lo = (r1 - mid.astype(F32)).astype(BF16)
    return hi, mid, lo


def _dot_exact_lhs01(m01, x):
    hi, mid, lo = _split3(x)
    return _dot(m01, hi) + _dot(m01, mid) + _dot(m01, lo)


def _dot_exact_rhs01(x, m01):
    hi, mid, lo = _split3(x)
    return _dot(hi, m01) + _dot(mid, m01) + _dot(lo, m01)


def _mem_kv_kernel(x_ref, nw_ref, w_ref, o_ref, ob_ref):
    h = _rms(x_ref[...], nw_ref[...]).astype(BF16)
    acc = _dot(h, w_ref[...])
    o_ref[...] = acc
    ob_ref[...] = acc.astype(BF16)


def _mem_kv(mem2d, nw, w_bf):
    m = mem2d.shape[0]
    tm = 256
    return pl.pallas_call(
        _mem_kv_kernel,
        grid=(m // tm,),
        in_specs=[
            pl.BlockSpec((tm, D_MODEL), lambda i: (i, 0)),
            pl.BlockSpec((1, D_MODEL), lambda i: (0, 0)),
            pl.BlockSpec((D_MODEL, 2 * MEM_WIDTH), lambda i: (0, 0)),
        ],
        out_specs=[
            pl.BlockSpec((tm, 2 * MEM_WIDTH), lambda i: (i, 0)),
            pl.BlockSpec((tm, 2 * MEM_WIDTH), lambda i: (i, 0)),
        ],
        out_shape=[
            jax.ShapeDtypeStruct((m, 2 * MEM_WIDTH), F32),
            jax.ShapeDtypeStruct((m, 2 * MEM_WIDTH), BF16),
        ],
        compiler_params=_cparams(("parallel",)),
        name="mem_kv",
    )(mem2d, nw, w_bf)


def _rope_tables(pos):
    half = ROPE_DIM // 2
    inv = jnp.power(ROPE_THETA, -jnp.arange(half, dtype=F32) * 2.0 / ROPE_DIM)
    ang = pos.astype(F32)[:, None] * inv[None, :]
    cos, sin = jnp.cos(ang), jnp.sin(ang)
    m = jnp.arange(LANES) % ATT_HEAD_DIM
    idx = m % half
    c = jnp.where(m[None, :] < ROPE_DIM, cos[:, idx], 1.0)
    s1 = jnp.where(m[None, :] < half, -sin[:, idx], 0.0)
    s2 = jnp.where((m[None, :] >= half) & (m[None, :] < ROPE_DIM), sin[:, idx], 0.0)
    return c.astype(F32), s1.astype(F32), s2.astype(F32)


def _in_proj_kernel(x_ref, nw_ref, wt_ref, c_ref, s1_ref, s2_ref,
                    zx_ref, q0_ref, q1_ref, q2_ref, rest_ref, dt_ref, *more_refs, tm, dils, win_rows):
    pw_refs = more_refs[:3] if win_rows else ()
    h_ref, acc_ref, tmp_ref = more_refs[-3:]
    nslab = COL_TILE // LANES
    h_ref[...] = _rms(x_ref[...], nw_ref[...]).astype(BF16)
    dt_ref[...] = _dot_nt(h_ref[...], wt_ref[DT_COL0:DT_COL0 + LANES, :])
    q_refs = (q0_ref, q1_ref, q2_ref)
    strided = 0
    for j in range(N_COL_TILES):
        col0 = j * COL_TILE + (SSD_N_HEADS if j >= QKV_TILE0 else 0)
        acc = _dot_nt(h_ref[...], wt_ref[col0:col0 + COL_TILE, :])
        if j < QKV_TILE0:
            zx_ref[:, j * COL_TILE:(j + 1) * COL_TILE] = acc.astype(zx_ref.dtype)
            continue
        if j >= REST_TILE0:
            rest_ref[:, (j - REST_TILE0) * COL_TILE:(j - REST_TILE0 + 1) * COL_TILE] = acc.astype(rest_ref.dtype)
            continue
        g, kind = divmod(j - QKV_TILE0, 3)
        out_ref, d = q_refs[g], dils[g]
        for c in range(nslab):
            a = acc[:, c * LANES:(c + 1) * LANES]
            if kind < 2:
                a = (a * c_ref[...] + pltpu.roll(a, LANES - ROPE_DIM // 2, 1) * s1_ref[...]
                     + pltpu.roll(a, ROPE_DIM // 2, 1) * s2_ref[...])
            if kind == 0:
                a = a * (ATT_HEAD_DIM ** -0.5 * LOG2E)
            lanes = slice(kind * COL_TILE + c * LANES, kind * COL_TILE + (c + 1) * LANES)
            if win_rows and kind > 0:
                ch0 = (kind - 1) * COL_TILE + c * LANES
                pw_refs[g][ch0:ch0 + LANES, :] = a[tm - win_rows[g]:, :].T
            if d == 1:
                out_ref[0, :, lanes] = a.astype(out_ref.dtype)
            elif d == 4:
                acc_ref[strided, c] = a
                for r in range(4):
                    out_ref[r, :, lanes] = acc_ref[strided, c, pl.ds(r, tm // 4, stride=4), :].astype(out_ref.dtype)
            else:
                acc_ref[strided, c] = a
                q4 = tm // 4
                for r1 in range(4):
                    tmp_ref[kind, c, r1 * q4:(r1 + 1) * q4, :] = acc_ref[strided, c, pl.ds(r1, q4, stride=4), :]
                for r1 in range(4):
                    for r2 in range(4):
                        out_ref[r1 + 4 * r2, :, lanes] = tmp_ref[
                            kind, c, pl.ds(r1 * q4 + r2, tm // 16, stride=4), :].astype(out_ref.dtype)
        if d > 1:
            strided += 1


def _in_proj(x2d, nw, wt, tabs, *, seq, tm, dils, out_dtype, windows=None):
    t = x2d.shape[0]
    nb = t // seq
    tiles_per_seq = seq // tm
    tab_blocks = tabs[0].shape[0] // tm
    n_strided = 3 * sum(1 for d in dils if d > 1)
    nslab = COL_TILE // LANES

    def qkv_spec(d):
        return pl.BlockSpec((None, d, tm // d, 3 * COL_TILE),
                            lambda i: (i // tiles_per_seq, 0, i % tiles_per_seq, 0))

    win_rows, win_specs, win_shapes = None, [], []
    if windows:
        win_rows = tuple(min(tm, w) for w in windows)
        for w, wr in zip(windows, win_rows):
            first = tiles_per_seq - w // wr
            win_specs.append(pl.BlockSpec(
                (None, 2 * COL_TILE, wr),
                lambda i, first=first: (i // tiles_per_seq, 0, jnp.maximum(i % tiles_per_seq - first, 0))))
            win_shapes.append(jax.ShapeDtypeStruct((nb, 2 * COL_TILE, w), F32))

    tab_spec = pl.BlockSpec((tm, LANES), lambda i: (i % tab_blocks, 0))
    resident = lambda shape: pl.BlockSpec(shape, lambda i: (0, 0), pipeline_mode=pl.Buffered(1))
    kern = functools.partial(_in_proj_kernel, tm=tm, dils=dils, win_rows=win_rows)
    n_zx, n_rest = QKV_TILE0 * COL_TILE, (N_COL_TILES - REST_TILE0) * COL_TILE
    return pl.pallas_call(
        kern,
        grid=(t // tm,),
        in_specs=[
            pl.BlockSpec((tm, D_MODEL), lambda i: (i, 0)),
            resident((1, D_MODEL)),
            resident(wt.shape),
            tab_spec, tab_spec, tab_spec,
        ],
        out_specs=[
            pl.BlockSpec((tm, n_zx), lambda i: (i, 0)),
            qkv_spec(dils[0]), qkv_spec(dils[1]), qkv_spec(dils[2]),
            pl.BlockSpec((tm, n_rest), lambda i: (i, 0)),
            pl.BlockSpec((tm, LANES), lambda i: (i, 0)),
        ] + win_specs,
        out_shape=[
            jax.ShapeDtypeStruct((t, n_zx), out_dtype),
            jax.ShapeDtypeStruct((nb, dils[0], seq // dils[0], 3 * COL_TILE), out_dtype),
            jax.ShapeDtypeStruct((nb, dils[1], seq // dils[1], 3 * COL_TILE), out_dtype),
            jax.ShapeDtypeStruct((nb, dils[2], seq // dils[2], 3 * COL_TILE), out_dtype),
            jax.ShapeDtypeStruct((t, n_rest), out_dtype),
            jax.ShapeDtypeStruct((t, LANES), F32),
        ] + win_shapes,
        scratch_shapes=[
            pltpu.VMEM((tm, D_MODEL), BF16),
            pltpu.VMEM((max(n_strided, 1), nslab, tm, LANES), F32),
            pltpu.VMEM((3, nslab, tm, LANES), F32),
        ],
        compiler_params=_cparams(("arbitrary",)),
        name="in_proj",
    )(x2d, nw, wt, *tabs)


def _ssd_small(dt_raw, dtb_row, a_row, dtb_col, a_col, same01, same01_t, tot01, tot01_t):
    dt = _softplus(dt_raw + dtb_row)
    dta = dt * a_row
    acs = _dot_exact_lhs01(same01, dta)
    tot = _dot_exact_lhs01(tot01, dta)
    dtd = dt * jnp.exp2(tot - acs)
    dt_raw_t = dt_raw.T
    dt_t = _softplus(dt_raw_t + dtb_col)
    dta_t = dt_t * a_col
    acs_t = _dot_exact_rhs01(dta_t, same01_t)
    tot_t = _dot_exact_rhs01(dta_t, tot01_t)
    dtd_t = dt_t * jnp.exp2(tot_t - acs_t)
    return dt, acs, tot, dtd, dt_t, acs_t, dtd_t


def _ssd_pairs(x, bm, cm, acs, acs_t, dt_t, dtd_t, mask, st_ref, y_ref_write, st_scale_row, after_pair=None):
    lane = lax.broadcasted_iota(jnp.int32, (1, LANES), 1)
    half = [lane < ATT_HEAD_DIM, lane >= ATT_HEAD_DIM]
    bm_t = [bm[:, g * SSD_D_STATE:(g + 1) * SSD_D_STATE].T for g in range(SSD_N_GROUPS)]
    cms = [cm[:, g * SSD_D_STATE:(g + 1) * SSD_D_STATE] for g in range(SSD_N_GROUPS)]
    cb = [_dot(cms[g].astype(BF16), bm_t[g].astype(BF16)) for g in range(SSD_N_GROUPS)]
    for p in range(SSD_N_HEADS // 2):
        g = p // 2
        xp = x[:, p * LANES:(p + 1) * LANES]
        if st_ref is not None:
            stp = st_ref[:, p * LANES:(p + 1) * LANES]
        y_pair = jnp.zeros((SSD_CHUNK, LANES), F32)
        st_add = jnp.zeros((SSD_D_STATE, LANES), F32)
        for hh in range(2):
            h = 2 * p + hh
            col = acs[:, h:h + 1]
            row = acs_t[h:h + 1, :]
            lmat = jnp.exp2(jnp.where(mask, col - row, NEG_INF))
            m = cb[g] * lmat * dt_t[h:h + 1, :]
            xm = jnp.where(half[hh], xp, 0.0).astype(BF16)
            if st_ref is not None:
                cx = cms[g] * jnp.exp2(col)
                lhs = jnp.concatenate([m, cx], axis=1).astype(BF16)
                rhs = jnp.concatenate([xm, jnp.where(half[hh], stp, 0.0).astype(BF16)], axis=0)
                y_pair = y_pair + _dot(lhs, rhs)
                bt_h = (bm_t[g] * dtd_t[h:h + 1, :]).astype(BF16)
                st_add = st_add + _dot(bt_h, xm)
            else:
                y_pair = y_pair + _dot(m.astype(BF16), xm)
        y_ref_write(p, y_pair)
        if st_ref is not None:
            scale = jnp.where(half[0], st_scale_row[:, 2 * p:2 * p + 1], st_scale_row[:, 2 * p + 1:2 * p + 2])
            st_ref[:, p * LANES:(p + 1) * LANES] = stp * scale + st_add
        if after_pair is not None:
            after_pair(p)


def _ssd_prompt_kernel(z_ref, xs_ref, bc_ref, xsn_ref, bcn_ref, dt_ref, cw_ref, cb_ref, dtb_ref, a_ref, dtbc_ref,
                       ac_ref, dsk_ref, nw_ref, sh_ref, y_ref, ssm_ref, xc_ref, st_ref, ybuf, *, ncs):
    c = pl.program_id(1)
    q = SSD_CHUNK

    def conv_act(prev, cur, lanes):
        shifted = _dot(sh_ref[...], jnp.concatenate([prev, cur], axis=0))
        acc = cb_ref[:, lanes] + cw_ref[SSD_CONV - 1:SSD_CONV, lanes] * cur.astype(F32)
        for s in range(1, SSD_CONV):
            acc = acc + cw_ref[SSD_CONV - 1 - s:SSD_CONV - s, lanes] * shifted[(s - 1) * q:s * q]
        return _silu(acc)

    @pl.when(c == 0)
    def _():
        st_ref[...] = jnp.zeros(st_ref.shape, F32)
        no_history = jnp.zeros((q, D_MODEL), xs_ref.dtype)
        xc_ref[0, :, 0:D_MODEL] = conv_act(no_history, xs_ref[0:q, :], slice(0, D_MODEL))
        xc_ref[0, :, D_MODEL:SSD_CONV_DIM] = conv_act(no_history, bc_ref[0:q, :], slice(D_MODEL, SSD_CONV_DIM))

    slab = SSD_CONV_DIM // (SSD_N_HEADS // 2)
    ri = lax.broadcasted_iota(jnp.int32, (q, q), 0)
    ci = lax.broadcasted_iota(jnp.int32, (q, q), 1)
    tril = ri >= ci
    tril01 = jnp.where(tril, 1.0, 0.0).astype(BF16)
    triu01 = jnp.where(ri <= ci, 1.0, 0.0).astype(BF16)
    ones01 = jnp.ones((q, q), BF16)

    def write_y(p, y_pair):
        ybuf[:, p * LANES:(p + 1) * LANES] = y_pair

    def small_terms(i):
        return _ssd_small(dt_ref[i * q:(i + 1) * q, :], dtb_ref[...], a_ref[...], dtbc_ref[...], ac_ref[...],
                          tril01, triu01, ones01, ones01)

    small_next = small_terms(0)
    for i in range(ncs):
        rows = slice(i * q, (i + 1) * q)
        small_cur = small_next
        if i + 1 < ncs:
            small_next = small_terms(i + 1)

        def conv_next(p, i=i, rows=rows):
            lanes = slice(p * slab, (p + 1) * slab)
            src_ref, nxt_ref, off = (xs_ref, xsn_ref, 0) if lanes.start < D_MODEL else (bc_ref, bcn_ref, D_MODEL)
            src = slice(lanes.start - off, lanes.stop - off)
            cur = src_ref[(i + 1) * q:(i + 2) * q, src] if i + 1 < ncs else nxt_ref[:, src]
            xc_ref[(i + 1) % ncs, :, lanes] = conv_act(src_ref[rows, src], cur, lanes)

        x = xc_ref[i, :, 0:D_MODEL]
        bm = xc_ref[i, :, D_MODEL:D_MODEL + 512]
        cm = xc_ref[i, :, D_MODEL + 512:SSD_CONV_DIM]
        dt, acs, tot, dtd, dt_t, acs_t, dtd_t = small_cur
        decay_row = jnp.exp2(tot[0:1, :])
        _ssd_pairs(x, bm, cm, acs, acs_t, dt_t, dtd_t, tril, st_ref, write_y, decay_row, after_pair=conv_next)
        y = (ybuf[...] + dsk_ref[...] * x) * _silu_of_half(z_ref[rows, :].astype(F32))
        y_ref[rows, :] = _rms(y, nw_ref[...]).astype(y_ref.dtype)

    @pl.when(c == pl.num_programs(1) - 1)
    def _():
        for p in range(D_MODEL // LANES):
            ssm_ref[p * LANES:(p + 1) * LANES, :] = st_ref[:, p * LANES:(p + 1) * LANES].T


def _ssd_prompt(zx, dt, cw, cb, dtb, a_row, dtb_col, a_col, dsk, nw, *, nbatch, seq):
    q = SSD_CHUNK
    ncs = next(n for n in (8, 4, 2, 1) if (seq // q) % n == 0)
    rows = ncs * q
    nc = seq // q
    ns = nc // ncs
    row = lambda b, c: b * ns + c
    nxt = lambda b, c: b * nc + jnp.minimum((c + 1) * ncs, nc - 1)
    vec = lambda n: pl.BlockSpec((1, n), lambda b, c: (0, 0))
    col = pl.BlockSpec((LANES, 1), lambda b, c: (0, 0))
    nsh = SSD_CONV - 1
    out_t = jnp.arange(nsh * q) % q
    shift = 1 + jnp.arange(nsh * q) // q
    shift01 = (jnp.arange(2 * q)[None, :] == (q + out_t - shift)[:, None]).astype(BF16)
    return pl.pallas_call(
        functools.partial(_ssd_prompt_kernel, ncs=ncs),
        grid=(nbatch, ns),
        in_specs=[
            pl.BlockSpec((rows, D_MODEL), lambda b, c: (row(b, c), 0)),
            pl.BlockSpec((rows, D_MODEL), lambda b, c: (row(b, c), 1)),
            pl.BlockSpec((rows, D_MODEL), lambda b, c: (row(b, c), 2)),
            pl.BlockSpec((q, D_MODEL), lambda b, c: (nxt(b, c), 1)),
            pl.BlockSpec((q, D_MODEL), lambda b, c: (nxt(b, c), 2)),
            pl.BlockSpec((rows, LANES), lambda b, c: (row(b, c), 0)),
            pl.BlockSpec((SSD_CONV, SSD_CONV_DIM), lambda b, c: (0, 0)),
            vec(SSD_CONV_DIM), vec(LANES), vec(LANES), col, col, vec(D_MODEL), vec(D_MODEL),
            pl.BlockSpec(((SSD_CONV - 1) * q, 2 * q), lambda b, c: (0, 0)),
        ],
        out_specs=[
            pl.BlockSpec((rows, D_MODEL), lambda b, c: (row(b, c), 0)),
            pl.BlockSpec((None, D_MODEL, SSD_D_STATE), lambda b, c: (b, 0, 0)),
        ],
        out_shape=[
            jax.ShapeDtypeStruct((nbatch * seq, D_MODEL), BF16),
            jax.ShapeDtypeStruct((nbatch, D_MODEL, SSD_D_STATE), F32),
        ],
        scratch_shapes=[
            pltpu.VMEM((max(ncs, 2), q, SSD_CONV_DIM), F32),
            pltpu.VMEM((SSD_D_STATE, D_MODEL), F32),
            pltpu.VMEM((q, D_MODEL), F32),
        ],
        compiler_params=_cparams(("parallel", "arbitrary")),
        name="ssd_prompt",
    )(zx, zx, zx, zx, zx, dt, cw, cb, dtb, a_row, dtb_col, a_col, dsk, nw, shift01)


def _expand_heads(v, e01):
    return _dot_exact_rhs01(v, e01)


def _ssd_sample_rows_kernel(xs_ref, bc_ref, dt_ref, hist_ref, cw_ref, cb_ref, dtb_ref, a_ref, dtbc_ref, ac_ref,
                            dsk_ref, e01_ref,
                            ypart_ref, ea_ref, xdt_ref, bm_ref, cm_ref, cd_ref, ybuf, *, tdec):
    q = SSD_CHUNK
    x_raw = jnp.concatenate([xs_ref[...], bc_ref[...]], axis=1)
    hist = hist_ref[...]
    rowi = lax.broadcasted_iota(jnp.int32, (q, 1), 0)
    tpos = rowi % tdec
    acc = cb_ref[...] + cw_ref[SSD_CONV - 1:SSD_CONV, :] * x_raw
    for s in range(1, SSD_CONV):
        prev = jnp.where(tpos >= s, pltpu.roll(x_raw, s, 0), pltpu.roll(hist, q - tdec + s, 0))
        acc = acc + cw_ref[SSD_CONV - 1 - s:SSD_CONV - s, :] * prev
    xc = _silu(acc)
    x = xc[:, :D_MODEL]
    bm = xc[:, D_MODEL:D_MODEL + 512]
    cm = xc[:, D_MODEL + 512:]

    ri = lax.broadcasted_iota(jnp.int32, (q, q), 0)
    ci = lax.broadcasted_iota(jnp.int32, (q, q), 1)
    same = (ri // tdec) == (ci // tdec)
    mask = same & (ri >= ci)
    low01 = jnp.where(mask, 1.0, 0.0).astype(BF16)
    up01 = jnp.where(same & (ri <= ci), 1.0, 0.0).astype(BF16)
    same01 = jnp.where(same, 1.0, 0.0).astype(BF16)
    dt, acs, tot, dtd, dt_t, acs_t, dtd_t = _ssd_small(
        dt_ref[...], dtb_ref[...], a_ref[...], dtbc_ref[...], ac_ref[...], low01, up01, same01, same01)

    def write_y(p, y_pair):
        ybuf[:, p * LANES:(p + 1) * LANES] = y_pair

    _ssd_pairs(x, bm, cm, acs, acs_t, dt_t, dtd_t, mask, None, write_y, None)

    e01 = e01_ref[...]
    ypart_ref[...] = ybuf[...] + dsk_ref[...] * x
    ea_ref[...] = _expand_heads(jnp.exp2(acs), e01)
    xd = x * _expand_heads(dtd, e01)
    for p in range(D_MODEL // LANES):
        xdt_ref[p * LANES:(p + 1) * LANES, :] = xd[:, p * LANES:(p + 1) * LANES].T.astype(BF16)
    bm_ref[...] = bm.astype(BF16)
    cm_ref[...] = cm.astype(BF16)
    cd_ref[...] = jnp.exp2(tot)


def _ssd_sample_rows(zx, dt, hist, cw, cb, dtb, a_row, dtb_col, a_col, dsk, e01, *, tdec):
    t = zx.shape[0]
    q = SSD_CHUNK
    vec = lambda n: pl.BlockSpec((1, n), lambda i: (0, 0))
    col = pl.BlockSpec((LANES, 1), lambda i: (0, 0))
    kern = functools.partial(_ssd_sample_rows_kernel, tdec=tdec)
    return pl.pallas_call(
        kern,
        grid=(t // q,),
        in_specs=[
            pl.BlockSpec((q, D_MODEL), lambda i: (i, 1)),
            pl.BlockSpec((q, D_MODEL), lambda i: (i, 2)),
            pl.BlockSpec((q, LANES), lambda i: (i, 0)),
            pl.BlockSpec((q, SSD_CONV_DIM), lambda i: (i, 0)),
            pl.BlockSpec((SSD_CONV, SSD_CONV_DIM), lambda i: (0, 0)),
            vec(SSD_CONV_DIM), vec(LANES), vec(LANES), col, col, vec(D_MODEL),
            pl.BlockSpec((LANES, D_MODEL), lambda i: (0, 0)),
        ],
        out_specs=[
            pl.BlockSpec((q, D_MODEL), lambda i: (i, 0)),
            pl.BlockSpec((q, D_MODEL), lambda i: (i, 0)),
            pl.BlockSpec((D_MODEL, q), lambda i: (0, i)),
            pl.BlockSpec((q, 512), lambda i: (i, 0)),
            pl.BlockSpec((q, 512), lambda i: (i, 0)),
            pl.BlockSpec((q, LANES), lambda i: (i, 0)),
        ],
        out_shape=[
            jax.ShapeDtypeStruct((t, D_MODEL), F32),
            jax.ShapeDtypeStruct((t, D_MODEL), F32),
            jax.ShapeDtypeStruct((D_MODEL, t), BF16),
            jax.ShapeDtypeStruct((t, 512), BF16),
            jax.ShapeDtypeStruct((t, 512), BF16),
            jax.ShapeDtypeStruct((t, LANES), F32),
        ],
        scratch_shapes=[pltpu.VMEM((q, D_MODEL), F32)],
        compiler_params=_cparams(("parallel",)),
        name="ssd_sample_rows",
    )(zx, zx, dt, hist, cw, cb, dtb, a_row, dtb_col, a_col, dsk, e01)


def _ssd_sample_state_kernel(cd_ref, z_ref, ypart_ref, ea_ref, xdt_ref, bm_ref, cm_ref, st_ref, nw_ref,
                             y_ref, sto_ref, *, tdec, gseq):
    i = pl.program_id(0)
    q = SSD_CHUNK
    rows = gseq * tdec
    steps_per_blk = q // rows
    lane = lax.broadcasted_iota(jnp.int32, (1, q), 1)
    rg = 16
    rowi = lax.broadcasted_iota(jnp.int32, (rg, 1), 0)
    lane_base = (i % steps_per_blk) * rows
    seq_per_rg = rg // tdec
    yoff_groups = []
    for q8 in range(rows // rg):
        yoff8 = jnp.zeros((rg, D_MODEL), F32)
        for sj in range(seq_per_rg):
            jj = q8 * seq_per_rg + sj
            b = i * gseq + jj
            lmask = (lane >= lane_base + jj * tdec) & (lane < lane_base + (jj + 1) * tdec)
            parts = []
            for g in range(SSD_N_GROUPS):
                s_g = st_ref[jj, g * 256:(g + 1) * 256, :]
                c8 = cm_ref[q8 * rg:(q8 + 1) * rg, g * SSD_D_STATE:(g + 1) * SSD_D_STATE]
                parts.append(_dot_nt(c8, s_g.astype(BF16)))
                lhs = jnp.where(lmask, xdt_ref[g * 256:(g + 1) * 256, :], jnp.zeros((), BF16))
                add = _dot(lhs, bm_ref[:, g * SSD_D_STATE:(g + 1) * SSD_D_STATE])
                for hh in range(4):
                    h = 4 * g + hh
                    sl = slice(hh * SSD_HEAD_DIM, (hh + 1) * SSD_HEAD_DIM)
                    sto_ref[jj, g * 256 + hh * 64:g * 256 + (hh + 1) * 64, :] = (
                        s_g[sl, :] * cd_ref[b, h] + add[sl, :])
            yo = jnp.concatenate(parts, axis=1)
            in_seq = (rowi >= sj * tdec) & (rowi < (sj + 1) * tdec)
            yoff8 = jnp.where(in_seq, yo, yoff8)
        yoff_groups.append(yoff8)
    yoff = jnp.concatenate(yoff_groups, axis=0) if len(yoff_groups) > 1 else yoff_groups[0]
    y = (ypart_ref[...] + yoff * ea_ref[...]) * _silu_of_half(z_ref[...])
    y_ref[...] = _rms(y, nw_ref[...])


def _ssd_sample_state(cd, zx, ypart, ea, xdt, bmb, cmb, state, nw, *, tdec, gseq):
    t = zx.shape[0]
    nseq = t // tdec
    q = SSD_CHUNK
    rows = gseq * tdec
    spb = q // rows
    kern = functools.partial(_ssd_sample_state_kernel, tdec=tdec, gseq=gseq)
    return pl.pallas_call(
        kern,
        grid=(nseq // gseq,),
        in_specs=[
            pl.BlockSpec(memory_space=pltpu.SMEM),
            pl.BlockSpec((rows, D_MODEL), lambda i: (i, 0)),
            pl.BlockSpec((rows, D_MODEL), lambda i: (i, 0)),
            pl.BlockSpec((rows, D_MODEL), lambda i: (i, 0)),
            pl.BlockSpec((D_MODEL, q), lambda i: (0, i // spb)),
            pl.BlockSpec((q, 512), lambda i: (i // spb, 0)),
            pl.BlockSpec((rows, 512), lambda i: (i, 0)),
            pl.BlockSpec((gseq, D_MODEL, SSD_D_STATE), lambda i: (i, 0, 0)),
            pl.BlockSpec((1, D_MODEL), lambda i: (0, 0)),
        ],
        out_specs=[
            pl.BlockSpec((rows, D_MODEL), lambda i: (i, 0)),
            pl.BlockSpec((gseq, D_MODEL, SSD_D_STATE), lambda i: (i, 0, 0)),
        ],
        out_shape=[
            jax.ShapeDtypeStruct((t, D_MODEL), F32),
            jax.ShapeDtypeStruct((nseq, D_MODEL, SSD_D_STATE), F32),
        ],
        compiler_params=_cparams(("parallel",)),
        name="ssd_sample_state",
    )(cd, zx, ypart, ea, xdt, bmb, cmb, state, nw)


def _lse_lane(h):
    return 16 * h


def _attn_prompt_kernel(q_ref, kc_ref, kp_ref, vc_ref, vp_ref, o_ref, lse_ref, *, nr, nq):
    n = pl.program_id(2)
    nk = 128
    ks = [jnp.concatenate([kp_ref[rr], kc_ref[rr]], axis=0) for rr in range(nr)]
    vs = [jnp.concatenate([vp_ref[rr], vc_ref[rr]], axis=0) for rr in range(nr)]
    lane = lax.broadcasted_iota(jnp.int32, (1, LANES), 1)
    lo = lane < ATT_HEAD_DIM
    qi = lax.broadcasted_iota(jnp.int32, (2 * nk, 2 * nk), 0) % nk
    kj = lax.broadcasted_iota(jnp.int32, (2 * nk, 2 * nk), 1)
    in_band = (kj >= qi) & (kj <= qi + nk)
    band = jnp.where(in_band, 0.0, NEG_INF)
    band_first = jnp.where(in_band & ((kj >= nk) | (n > 0)), 0.0, NEG_INF)
    zero = jnp.zeros((), q_ref.dtype)
    npairs = ATT_HEADS // 2
    pair_lanes = [slice(p * LANES, (p + 1) * LANES) for p in range(npairs)]
    units = [(rr, j, p) for rr in range(nr) for j in range(nq) for p in range(npairs)]

    def scores(u):
        rr, j, p = u
        qp = q_ref[rr, j * nk:(j + 1) * nk, pair_lanes[p]]
        q2 = jnp.concatenate([jnp.where(lo, qp, zero), jnp.where(lo, zero, qp)], axis=0)
        kk = ks[rr][j * nk:(j + 2) * nk, pair_lanes[p]]
        return _dot_nt(q2, kk) + (band_first if j == 0 else band)

    def softmax(s):
        m = jnp.max(s, axis=-1, keepdims=True)
        e = jnp.exp2(s - m)
        den = jnp.sum(e, axis=-1, keepdims=True)
        return e.astype(BF16), den, m

    ahead = 5
    s_next = [scores(u) for u in units[:ahead]]
    m_full = den_full = None
    o_parts = []
    for i, (rr, j, p) in enumerate(units):
        e, den, m = softmax(s_next.pop(0))
        if i + ahead < len(units):
            s_next.append(scores(units[i + ahead]))
        r = _dot(e, vs[rr][j * nk:(j + 2) * nk, pair_lanes[p]])
        den_pair = jnp.where(lo, den[0:nk], den[nk:2 * nk])
        o_parts.append((jnp.where(lo, r[0:nk], r[nk:2 * nk]) / den_pair).astype(o_ref.dtype))
        if p == 0:
            m_full = jnp.zeros((nk, LANES), F32)
            den_full = jnp.ones((nk, LANES), F32)
        for hh in range(2):
            base = _lse_lane(2 * p + hh)
            grp = (lane >= base) & (lane < base + 16)
            m_full = jnp.where(grp, m[hh * nk:(hh + 1) * nk], m_full)
            den_full = jnp.where(grp, den[hh * nk:(hh + 1) * nk], den_full)
        if p == npairs - 1:
            o_ref[rr, j * nk:(j + 1) * nk, :] = jnp.concatenate(o_parts, axis=1)
            lse_ref[rr, j * nk:(j + 1) * nk, :] = (m_full + jnp.log2(den_full)) * LN2
            o_parts = []


ATT_BLOCKS_PER_STEP = 16


def _attn_prompt(qkv, *, d):
    nb, _, length, _ = qkv.shape
    nk = 128
    nq = next(c for c in (16, 8, 4, 2, 1) if c <= ATT_BLOCKS_PER_STEP and length % (c * nk) == 0)
    nr = next(c for c in (16, 8, 4, 2, 1) if c * nq <= ATT_BLOCKS_PER_STEP and d % c == 0)
    nblk = length // (nq * nk)
    cur = lambda which: pl.BlockSpec((None, nr, nq * nk, COL_TILE), lambda b, r, n: (b, r, n, which))
    prev = lambda which: pl.BlockSpec((None, nr, nk, COL_TILE),
                                      lambda b, r, n: (b, r, jnp.maximum(nq * n - 1, 0), which))
    return pl.pallas_call(
        functools.partial(_attn_prompt_kernel, nr=nr, nq=nq),
        grid=(nb, d // nr, nblk),
        in_specs=[cur(0), cur(1), prev(1), cur(2), prev(2)],
        out_specs=[
            pl.BlockSpec((None, nr, nq * nk, ATT_WIDTH), lambda b, r, n: (b, r, n, 0)),
            pl.BlockSpec((None, nr, nq * nk, LANES), lambda b, r, n: (b, r, n, 0)),
        ],
        out_shape=[
            jax.ShapeDtypeStruct((nb, d, length, ATT_WIDTH), BF16),
            jax.ShapeDtypeStruct((nb, d, length, LANES), F32),
        ],
        compiler_params=_cparams(("parallel", "parallel", "arbitrary")),
        name=f"attn_prompt_d{d}",
    )(qkv, qkv, qkv, qkv, qkv)


def _attn_sample_kernel(q0_ref, q1_ref, q2_ref, qm_ref, c0_ref, c1_ref, c2_ref, cm_ref, oa_ref, om_ref, *, tdec):
    nh = ATT_HEADS
    rows8 = 8
    spg = rows8 // tdec
    r64 = nh * rows8
    rowi = lax.broadcasted_iota(jnp.int32, (r64, 1), 0)
    head_of_row = rowi // rows8
    r8 = rowi % rows8
    seq_q = r8 // tdec
    t_q = r8 % tdec
    lane512 = lax.broadcasted_iota(jnp.int32, (1, ATT_WIDTH), 1)
    headmask = (lane512 // ATT_HEAD_DIM) == head_of_row
    col8 = lax.broadcasted_iota(jnp.int32, (1, rows8), 1)
    seq_k = col8 // tdec
    t_k = col8 % tdec

    res = [[] for _ in range(spg)]
    for g, (q_ref, c_ref, d) in enumerate(zip((q0_ref, q1_ref, q2_ref), (c0_ref, c1_ref, c2_ref), ATT_DILATIONS)):
        w = c_ref.shape[-1]
        qkv = q_ref[...]
        qv = qkv[:, :ATT_WIDTH]
        kn = qkv[:, ATT_WIDTH:2 * ATT_WIDTH]
        vn = qkv[:, 2 * ATT_WIDTH:]
        q64 = jnp.where(headmask, jnp.concatenate([qv] * nh, axis=0), 0.0)
        s_new = _dot_nt(q64, kn)
        if d == 1:
            valid_new = (seq_k == seq_q) & (t_k <= t_q)
        else:
            valid_new = (seq_k == seq_q) & (t_k == t_q)
        s_new = jnp.where(valid_new, s_new, NEG_INF)
        wi = lax.broadcasted_iota(jnp.int32, (1, w), 1)
        valid_c = (wi >= t_q) if d == 1 else ((wi % d) == t_q)
        m_new = jnp.max(s_new, axis=-1, keepdims=True)
        for sj in range(spg):
            s_c = jnp.where(valid_c, _dot(q64.astype(BF16), c_ref[sj, 0].astype(BF16)), NEG_INF)
            m = jnp.maximum(jnp.max(s_c, axis=-1, keepdims=True), m_new)
            e_c = jnp.exp2(s_c - m)
            e_n = jnp.exp2(s_new - m)
            den = jnp.sum(e_c, axis=-1, keepdims=True) + jnp.sum(e_n, axis=-1, keepdims=True)
            o = _dot(e_n, vn) + _dot_nt(e_c.astype(BF16), c_ref[sj, 1].astype(BF16))
            res[sj].append((o / den, (m + jnp.log2(den)) * LN2))

    row8 = lax.broadcasted_iota(jnp.int32, (rows8, 1), 0)
    out_a = jnp.zeros((rows8, ATT_WIDTH), F32)
    for sj in range(spg):
        (o0, l0), (o1, l1), (o2, l2) = res[sj]
        mx = jnp.maximum(jnp.maximum(l0, l1), l2)
        w0, w1, w2 = jnp.exp(l0 - mx), jnp.exp(l1 - mx), jnp.exp(l2 - mx)
        om = (w0 * o0 + w1 * o1 + w2 * o2) / (w0 + w1 + w2)
        om = jnp.where(headmask, om, 0.0)
        o8 = om[0:rows8]
        for h in range(1, nh):
            o8 = o8 + om[h * rows8:(h + 1) * rows8]
        out_a = jnp.where((row8 // tdec) == sj, o8, out_a)
    oa_ref[...] = out_a

    qm = jnp.concatenate([qm_ref[...] * (MEM_HEAD_DIM ** -0.5), jnp.zeros((rows8, MEM_WIDTH), F32)], axis=0)
    out_m = jnp.zeros((rows8, MEM_WIDTH), F32)
    for sj in range(spg):
        mparts = []
        for h in range(MEM_HEADS):
            kh = cm_ref[sj, pl.ds(h, MEM_LEN, stride=2 * MEM_HEADS), :].astype(BF16)
            vh = cm_ref[sj, pl.ds(MEM_HEADS + h, MEM_LEN, stride=2 * MEM_HEADS), :].astype(BF16)
            s = _dot_nt(qm[:, h * MEM_HEAD_DIM:(h + 1) * MEM_HEAD_DIM].astype(BF16), kh)
            mm = jnp.max(s, axis=-1, keepdims=True)
            e = jnp.exp(s - mm)
            den = jnp.sum(e, axis=-1, keepdims=True)
            mparts.append((_dot(e.astype(BF16), vh) / den)[0:rows8])
        out_m = jnp.where((row8 // tdec) == sj, jnp.concatenate(mparts, axis=1), out_m)
    om_ref[...] = out_m


def _attn_sample(q0, q1, q2, rest, c0, c1, c2, cmem, *, tdec):
    t = q0.shape[0]
    spg = 8 // tdec
    row_spec = pl.BlockSpec((8, 3 * COL_TILE), lambda i: (i, 0))
    cache_spec = lambda c: pl.BlockSpec((spg,) + c.shape[1:], lambda i: (i,) + (0,) * (c.ndim - 1))
    kern = functools.partial(_attn_sample_kernel, tdec=tdec)
    return pl.pallas_call(
        kern,
        grid=(t // 8,),
        in_specs=[
            row_spec, row_spec, row_spec,
            pl.BlockSpec((8, COL_TILE), lambda i: (i, 1)),
            cache_spec(c0), cache_spec(c1), cache_spec(c2), cache_spec(cmem),
        ],
        out_specs=[
            pl.BlockSpec((8, ATT_WIDTH), lambda i: (i, 0)),
            pl.BlockSpec((8, MEM_WIDTH), lambda i: (i, 0)),
        ],
        out_shape=[
            jax.ShapeDtypeStruct((t, ATT_WIDTH), F32),
            jax.ShapeDtypeStruct((t, MEM_WIDTH), F32),
        ],
        compiler_params=_cparams(("parallel",)),
        name="attn_sample",
    )(q0, q1, q2, rest, c0, c1, c2, cmem)


def _final_tail(x, p_ssd, y_att, y_mem, gate_raw, watt_ref, wmem_ref, wout_ref, nf_ref):
    t = jnp.tanh(gate_raw)
    p_att = _dot(y_att.astype(BF16), watt_ref[...])
    p_mem = _dot(y_mem.astype(BF16), wmem_ref[...])
    merged = ((t[:, 0:D_MODEL] * p_ssd + p_ssd) + (t[:, D_MODEL:2 * D_MODEL] * p_att + p_att)
              + (t[:, 2 * D_MODEL:] * p_mem + p_mem))
    return _rms(x + _dot(merged.astype(BF16), wout_ref[...]), nf_ref[...])


def _final_prompt_kernel(x_ref, ys_ref, o0_ref, o1_ref, o2_ref, l0_ref, l1_ref, l2_ref, rest_ref, mkv_ref,
                         wssd_ref, watt_ref, wmem_ref, wout_ref, nf_ref, e16_ref,
                         y_ref, obuf, lbuf, *, tm, dils, nsub):
    nslab = ATT_WIDTH // LANES
    o_refs, l_refs = (o0_ref, o1_ref, o2_ref), (l0_ref, l1_ref, l2_ref)
    for g, (o_ref, l_ref, d) in enumerate(zip(o_refs, l_refs, dils)):
        if d == 1:
            continue
        for r in range(d):
            blk = o_ref[r].astype(F32)
            for c in range(nslab):
                obuf[g, c, pl.ds(r, tm // d, stride=d), :] = blk[:, c * LANES:(c + 1) * LANES]
            lbuf[g, pl.ds(r, tm // d, stride=d), :] = l_ref[r]
    e16 = e16_ref[...]

    for rs in range(nsub):
        rows = slice(rs * (tm // nsub), (rs + 1) * (tm // nsub))
        on, ln = [], []
        for g in range(3):
            if dils[g] == 1:
                on.append(o_refs[g][0, rows, :].astype(F32))
                ln.append(l_refs[g][0, rows, :])
            else:
                on.append(jnp.concatenate([obuf[g, c, rows, :] for c in range(nslab)], axis=1))
                ln.append(lbuf[g, rows, :])
        mx = jnp.maximum(jnp.maximum(ln[0], ln[1]), ln[2])
        ws = [jnp.exp(l - mx) for l in ln]
        wsum = ws[0] + ws[1] + ws[2]
        o_att = jnp.zeros((tm // nsub, ATT_WIDTH), F32)
        for g in range(3):
            hi, mid, _ = _split3(ws[g] / wsum)
            o_att = o_att + _dot(jnp.concatenate([hi, mid], axis=1), e16) * on[g]

        rest = rest_ref[rows, :]
        z_att = rest[:, 0:512].astype(F32)
        q_mem = rest[:, 512:1024]
        z_mem = rest[:, 1024:1536].astype(F32)
        gate_raw = rest[:, 1536:].astype(F32)
        y_att = o_att * _silu_of_half(z_att)

        outs = []
        for h in range(MEM_HEADS):
            lanes = slice(h * MEM_HEAD_DIM, (h + 1) * MEM_HEAD_DIM)
            vh = mkv_ref[:, MEM_WIDTH + h * MEM_HEAD_DIM:MEM_WIDTH + (h + 1) * MEM_HEAD_DIM]
            s = _dot_nt(q_mem[:, lanes], mkv_ref[:, lanes]) * (MEM_HEAD_DIM ** -0.5)
            m = jnp.max(s, axis=-1, keepdims=True)
            e = jnp.exp(s - m)
            den = jnp.sum(e, axis=-1, keepdims=True)
            outs.append(_dot(e.astype(BF16), vh) / den)
        y_mem = jnp.concatenate(outs, axis=1) * _silu_of_half(z_mem)

        p_ssd = _dot(ys_ref[rows, :], wssd_ref[...])
        y_ref[rows, :] = _final_tail(x_ref[rows, :], p_ssd, y_att, y_mem, gate_raw,
                                     watt_ref, wmem_ref, wout_ref, nf_ref)


def _final_prompt(x2d, y_ssd, o_g, l_g, rest, mkv_bf, wssd, watt, wmem, wout, nf, e16, *, seq, tm, dils, nsub):
    t = x2d.shape[0]
    tps = seq // tm

    def res_spec(d, width):
        return pl.BlockSpec((None, d, tm // d, width), lambda i: (i // tps, 0, i % tps, 0))

    full = lambda a: pl.BlockSpec(a.shape, lambda i: (0,) * a.ndim)
    kern = functools.partial(_final_prompt_kernel, tm=tm, dils=dils, nsub=nsub)
    return pl.pallas_call(
        kern,
        grid=(t // tm,),
        in_specs=[
            pl.BlockSpec((tm, D_MODEL), lambda i: (i, 0)),
            pl.BlockSpec((tm, D_MODEL), lambda i: (i, 0)),
            res_spec(dils[0], ATT_WIDTH), res_spec(dils[1], ATT_WIDTH), res_spec(dils[2], ATT_WIDTH),
            res_spec(dils[0], LANES), res_spec(dils[1], LANES), res_spec(dils[2], LANES),
            pl.BlockSpec((tm, rest.shape[1]), lambda i: (i, 0)),
            pl.BlockSpec((None, MEM_LEN, 2 * MEM_WIDTH), lambda i: (i // tps, 0, 0)),
            full(wssd), full(watt), full(wmem), full(wout), full(nf), full(e16),
        ],
        out_specs=pl.BlockSpec((tm, D_MODEL), lambda i: (i, 0)),
        out_shape=jax.ShapeDtypeStruct((t, D_MODEL), F32),
        scratch_shapes=[
            pltpu.VMEM((3, ATT_WIDTH // LANES, tm, LANES), F32),
            pltpu.VMEM((3, tm, LANES), F32),
        ],
        compiler_params=_cparams(("parallel",)),
        name="final_prompt",
    )(x2d, y_ssd, *o_g, *l_g, rest, mkv_bf, wssd, watt, wmem, wout, nf, e16)


def _final_sample_kernel(x_ref, ys_ref, oa_ref, om_ref, rest_ref,
                         wssd_ref, watt_ref, wmem_ref, wout_ref, nf_ref, y_ref):
    p_ssd = _dot(ys_ref[...].astype(BF16), wssd_ref[...])
    rest = rest_ref[...]
    y_att = oa_ref[...] * _silu_of_half(rest[:, 0:512])
    y_mem = om_ref[...] * _silu_of_half(rest[:, 1024:1536])
    y_ref[...] = _final_tail(x_ref[...], p_ssd, y_att, y_mem, rest[:, 1536:], watt_ref, wmem_ref, wout_ref, nf_ref)


def _final_sample(x2d, y_ssd, o_att, o_mem, rest, wssd, watt, wmem, wout, nf, *, tm):
    t = x2d.shape[0]
    full = lambda a: pl.BlockSpec(a.shape, lambda i: (0,) * a.ndim)
    rows = lambda w: pl.BlockSpec((tm, w), lambda i: (i, 0))
    return pl.pallas_call(
        _final_sample_kernel,
        grid=(t // tm,),
        in_specs=[rows(D_MODEL), rows(D_MODEL), rows(ATT_WIDTH), rows(MEM_WIDTH), rows(rest.shape[1]),
                  full(wssd), full(watt), full(wmem), full(wout), full(nf)],
        out_specs=rows(D_MODEL),
        out_shape=jax.ShapeDtypeStruct((t, D_MODEL), F32),
        compiler_params=_cparams(("parallel",)),
        name="final_sample",
    )(x2d, y_ssd, o_att, o_mem, rest, wssd, watt, wmem, wout, nf)


def kernel(x_prompt, x_sample, mem_prompt, cache_win128_kv, cache_win512_kv, cache_win2048_kv, cache_mem_kv,
           state_conv, state_ssm, norm_in_w, w_in, conv_w, conv_b, dt_bias, a_log, d_skip, ssd_norm_w,
           mem_norm_w, w_mem_kv, w_br_ssd, w_br_att, w_br_mem, w_out, norm_f_w):
    bp, seq, _ = x_prompt.shape
    bs, tdec, _ = x_sample.shape
    dils = ATT_DILATIONS

    row = lambda v: v.reshape(1, -1).astype(F32)
    col = jnp.arange(w_in.shape[1])
    rest0 = DT_COL0 + SSD_N_HEADS + (REST_TILE0 - QKV_TILE0) * COL_TILE
    is_gate = (col < D_MODEL) | ((col >= rest0) & ((col < rest0 + ATT_WIDTH) | (col >= rest0 + ATT_WIDTH + MEM_WIDTH)))
    wt = (w_in.T * jnp.where(is_gate, 0.5, 1.0)[:, None]).astype(BF16)
    pad_heads = lambda v: jnp.pad(v.astype(F32), (0, LANES - SSD_N_HEADS)).reshape(1, LANES)
    dtb = pad_heads(dt_bias)
    a_row = jnp.pad(-jnp.exp(a_log.astype(F32)) * LOG2E, (0, LANES - SSD_N_HEADS)).reshape(1, LANES)
    dtb_col, a_col = dtb.reshape(LANES, 1), a_row.reshape(LANES, 1)
    dsk = jnp.repeat(d_skip.astype(F32), SSD_HEAD_DIM).reshape(1, D_MODEL)
    cw = conv_w.astype(F32)
    cb = row(conv_b)
    nin, nssd, nmem, nf = row(norm_in_w), row(ssd_norm_w), row(mem_norm_w), row(norm_f_w)
    wssd, watt, wmem = ((0.5 * w).astype(BF16) for w in (w_br_ssd, w_br_att, w_br_mem))
    wout = w_out.astype(BF16)
    lane = jnp.arange(LANES)
    e_ssd = (lane[:, None] == (jnp.arange(D_MODEL)[None, :] // SSD_HEAD_DIM)).astype(BF16)
    lse_lane_of_ch = jnp.repeat(jnp.array([_lse_lane(h) for h in range(ATT_HEADS)]), ATT_HEAD_DIM)
    e_att = (lane[:, None] == lse_lane_of_ch[None, :]).astype(BF16)
    e_att = jnp.concatenate([e_att, e_att], axis=0)

    xp2 = x_prompt.reshape(bp * seq, D_MODEL)
    mkv_f32, mkv_bf = _mem_kv(mem_prompt.reshape(bp * MEM_LEN, D_MODEL), nmem, w_mem_kv.astype(BF16))
    tm_p = 256
    tabs_p = _rope_tables(jnp.arange(seq))
    windows = tuple(min(w, seq) for w in ATT_WINDOWS)
    zx, q0, q1, q2, rest, dtr, kv0, kv1, kv2 = _in_proj(
        xp2, nin, wt, tabs_p, seq=seq, tm=tm_p, dils=dils, out_dtype=BF16, windows=windows)
    y_ssd, p_ssm = _ssd_prompt(zx, dtr, cw, cb, dtb, a_row, dtb_col, a_col, dsk, nssd, nbatch=bp, seq=seq)
    o_g, l_g = [], []
    for qkv, d in zip((q0, q1, q2), dils):
        o, l = _attn_prompt(qkv, d=d)
        o_g.append(o)
        l_g.append(l)
    y_prompt = _final_prompt(xp2, y_ssd, o_g, l_g, rest, mkv_bf.reshape(bp, MEM_LEN, 2 * MEM_WIDTH),
                             wssd, watt, wmem, wout, nf, e_att, seq=seq, tm=512, dils=dils, nsub=1)
    y_prompt = y_prompt.reshape(bp, seq, D_MODEL)

    p_win = [kv.reshape(bp, 2, ATT_HEADS, ATT_HEAD_DIM, kv.shape[-1]).transpose(0, 4, 1, 2, 3)
             for kv in (kv0, kv1, kv2)]
    p_mem_kv = mkv_f32.reshape(bp, MEM_LEN, 2, MEM_HEADS, MEM_HEAD_DIM)
    p_conv = zx.reshape(bp, seq, -1)[:, seq - (SSD_CONV - 1):, D_MODEL:].astype(F32)
    p_ssm = p_ssm.reshape(bp, SSD_N_HEADS, SSD_HEAD_DIM, SSD_D_STATE)

    ts = bs * tdec
    pos_s = PAST_LEN + jnp.tile(jnp.arange(tdec), bs)
    xs2 = x_sample.reshape(ts, D_MODEL)
    tabs_s = _rope_tables(pos_s)
    zx_s, q0_s, q1_s, q2_s, rest_s, dt_s = _in_proj(
        xs2, nin, wt, tabs_s, seq=ts, tm=min(ts, 128), dils=(1, 1, 1), out_dtype=F32)
    q_s = [a.reshape(ts, 3 * COL_TILE) for a in (q0_s, q1_s, q2_s)]
    hist = jnp.pad(state_conv.astype(F32), ((0, 0), (tdec - (SSD_CONV - 1), 0), (0, 0))).reshape(ts, SSD_CONV_DIM)
    ypart, ea, xdt, bmb, cmb, cd = _ssd_sample_rows(
        zx_s, dt_s, hist, cw, cb, dtb, a_row, dtb_col, a_col, dsk, e_ssd, tdec=tdec)
    cd_seq = cd.reshape(bs, tdec, LANES)[:, 0, :SSD_N_HEADS]
    y_ssd_s, s_ssm = _ssd_sample_state(cd_seq, zx_s, ypart, ea, xdt, bmb, cmb,
                                       state_ssm.reshape(bs, D_MODEL, SSD_D_STATE), nssd, tdec=tdec, gseq=8)
    to_kt = lambda c: c.transpose(0, 2, 3, 4, 1).reshape(bs, 2, ATT_WIDTH, c.shape[1])
    c0, c1, c2 = to_kt(cache_win128_kv), to_kt(cache_win512_kv), to_kt(cache_win2048_kv)
    cmem = cache_mem_kv.reshape(bs, MEM_LEN * 2 * MEM_HEADS, MEM_HEAD_DIM)
    o_att_s, o_mem_s = _attn_sample(*q_s, rest_s, c0, c1, c2, cmem, tdec=tdec)
    y_sample = _final_sample(xs2, y_ssd_s, o_att_s, o_mem_s, rest_s, wssd, watt, wmem, wout, nf, tm=min(ts, 256))
    y_sample = y_sample.reshape(bs, tdec, D_MODEL)

    s_win = [a[:, COL_TILE:].reshape(bs, tdec, 2, ATT_HEADS, ATT_HEAD_DIM) for a in q_s]
    s_conv = zx_s.reshape(bs, tdec, -1)[:, tdec - (SSD_CONV - 1):, D_MODEL:]
    s_ssm = s_ssm.reshape(bs, SSD_N_HEADS, SSD_HEAD_DIM, SSD_D_STATE)

    return (y_prompt, y_sample, p_win[0], p_win[1], p_win[2], p_mem_kv, p_conv, p_ssm,
            s_win[0], s_win[1], s_win[2], s_conv, s_ssm)
```
